```python
import numpy as np
import jax
import jax.numpy as jnp
from jax import lax

D_MODEL = 1024
BATCH = 4
SEQ = 8192
DEPTH = 2

GRID_W = 64
CTX_LEN = 256
HEAD_DIM = 64
QBLOCK = 128
ROPE_THETA = 10000.0
EPS = 1e-6
NEG_INF = -1e30

NA_HEADS = 4
NA_KH = 8
NA_KW = 16
GQA_HEADS = 4
GQA_KV_HEADS = 2
SWA_HEADS = 4
SWA_KV_HEADS = 2
SWA_WINDOW = 128
MLA_HEADS = 4
MLA_NOPE = 64
MLA_ROPE = 32
MLA_V = 64
MLA_Q_LORA = 192
MLA_KV_LORA = 128
N_BRANCH = 4
BRANCH_W = 256
N_EXPERTS = 16
EXPERT_FF = 1024
CAPACITY = 2
N_MOD = 6
DEEPNORM_ALPHA = (2 * DEPTH) ** 0.25
DEEPNORM_BETA = (8 * DEPTH) ** -0.25

IN_SPLITS = (
    NA_HEADS * HEAD_DIM, NA_HEADS * HEAD_DIM, NA_HEADS * HEAD_DIM,
    GQA_HEADS * HEAD_DIM, GQA_KV_HEADS * HEAD_DIM, GQA_KV_HEADS * HEAD_DIM,
    SWA_HEADS * HEAD_DIM, SWA_KV_HEADS * HEAD_DIM, SWA_KV_HEADS * HEAD_DIM,
    MLA_Q_LORA, MLA_KV_LORA, MLA_ROPE,
    N_BRANCH * D_MODEL,
)
IN_WIDTH = sum(IN_SPLITS)

kernel_name = 'hybrid_diffusion_parallel_mixers_ec_moe'


def layer_norm(x, g=None, b=None):
    xf = x.astype(jnp.float32)
    mu = jnp.mean(xf, axis=-1, keepdims=True)
    var = jnp.mean(jnp.square(xf - mu), axis=-1, keepdims=True)
    y = (xf - mu) * lax.rsqrt(var + EPS)
    if g is not None:
        y = y * g.astype(jnp.float32) + b.astype(jnp.float32)
    return y.astype(x.dtype)


def rms_norm(x, g):
    xf = x.astype(jnp.float32)
    y = xf * lax.rsqrt(jnp.mean(jnp.square(xf), axis=-1, keepdims=True) + EPS) * g.astype(jnp.float32)
    return y.astype(x.dtype)


def modulate(x, shift, scale):
    return layer_norm(x) * (1.0 + scale) + shift


def rope_1d(x, pos):
    half = x.shape[-1] // 2
    freqs = ROPE_THETA ** (-jnp.arange(half, dtype=jnp.float32) / half)
    ang = pos[:, None] * freqs[None, :]
    cos = jnp.cos(ang)[:, None, :].astype(x.dtype)
    sin = jnp.sin(ang)[:, None, :].astype(x.dtype)
    x1, x2 = x[..., :half], x[..., half:]
    return jnp.concatenate([x1 * cos - x2 * sin, x1 * sin + x2 * cos], axis=-1)


def axial_rope(x, row, col):
    half = x.shape[-1] // 2
    return jnp.concatenate([rope_1d(x[..., :half], row), rope_1d(x[..., half:], col)], axis=-1)


def split_in(z):
    offs = np.cumsum(IN_SPLITS)[:-1].tolist()
    return jnp.split(z, offs, axis=-1)


def heads(t, nh):
    return t.reshape(t.shape[:2] + (nh, -1))


def group(q, nkv):
    return q.reshape(q.shape[:2] + (nkv, q.shape[2] // nkv, q.shape[3]))


def cat(a, b):
    return jnp.concatenate([a, b], axis=1)


def attend(q, k, v, sink=None):
    s = jnp.einsum('bqkgd,bmkd->bkgqm', q, k).astype(jnp.float32) * (q.shape[-1] ** -0.5)
    if sink is not None:
        col = jnp.broadcast_to(sink.astype(jnp.float32)[None, :, :, None, None], s.shape[:-1] + (1,))
        p = jax.nn.softmax(jnp.concatenate([s, col], axis=-1), axis=-1)[..., :-1]
    else:
        p = jax.nn.softmax(s, axis=-1)
    return jnp.einsum('bkgqm,bmkd->bqkgd', p.astype(v.dtype), v)


def dense_blocks(q, k, v):
    B, n = q.shape[:2]
    nb = n // QBLOCK
    qb = jnp.moveaxis(q.reshape((B, nb, QBLOCK) + q.shape[2:]), 1, 0)
    ob = lax.map(lambda qi: attend(qi, k, v), qb)
    return jnp.moveaxis(ob, 0, 1).reshape((B, n) + ob.shape[3:])


def window_attn(q, k, v, kc, vc, sink):
    B, n, K, G, d = q.shape
    nb = n // QBLOCK

    def bands(t):
        tp = jnp.pad(t, ((0, 0), (QBLOCK, QBLOCK), (0, 0), (0, 0)))
        tp = tp.reshape((B, nb + 2, QBLOCK) + t.shape[2:])
        return jnp.concatenate([tp[:, :-2], tp[:, 1:-1], tp[:, 2:]], axis=2)

    kb, vb = bands(k), bands(v)
    qb = q.reshape(B, nb, QBLOCK, K, G, d)
    qpos = jnp.arange(n).reshape(nb, QBLOCK)
    kpos = (jnp.arange(nb)[:, None] - 1) * QBLOCK + jnp.arange(3 * QBLOCK)[None, :]
    ok = ((jnp.abs(qpos[:, :, None] - kpos[:, None, :]) <= SWA_WINDOW)
          & (kpos[:, None, :] >= 0) & (kpos[:, None, :] < n))
    scale = d ** -0.5
    s_lat = jnp.einsum('bnqkgd,bnmkd->bnkgqm', qb, kb).astype(jnp.float32) * scale
    s_lat = jnp.where(ok[None, :, None, None], s_lat, NEG_INF)
    s_ctx = jnp.einsum('bnqkgd,bmkd->bnkgqm', qb, kc).astype(jnp.float32) * scale
    s_snk = jnp.broadcast_to(sink.astype(jnp.float32)[None, None, :, :, None, None], s_lat.shape[:-1] + (1,))
    p = jax.nn.softmax(jnp.concatenate([s_lat, s_ctx, s_snk], axis=-1), axis=-1).astype(v.dtype)
    m = 3 * QBLOCK
    lc = kc.shape[1]
    o = (jnp.einsum('bnkgqm,bnmkd->bnqkgd', p[..., :m], vb)
         + jnp.einsum('bnkgqm,bmkd->bnqkgd', p[..., m:m + lc], vc))
    return o.reshape(B, n, K, G, -1)


def neighborhood_attn(q, k, v, kc, vc, rpb):
    B, n, H, d = q.shape
    rows = n // GRID_W
    kh = min(NA_KH, rows)
    scale = d ** -0.5
    q = q.reshape(B, rows, GRID_W, H, d)
    k = k.reshape(B, rows, GRID_W, H, d)
    v = v.reshape(B, rows, GRID_W, H, d)
    r = jnp.arange(rows)
    key_rows = jnp.clip(r - kh // 2, 0, rows - kh)[:, None] + jnp.arange(kh)[None, :]
    kg = k[:, key_rows]
    vg = v[:, key_rows]
    w = jnp.arange(GRID_W)
    col_start = jnp.clip(w - NA_KW // 2, 0, GRID_W - NA_KW)
    col_ok = (w[None, :] >= col_start[:, None]) & (w[None, :] < col_start[:, None] + NA_KW)
    dr_idx = key_rows - r[:, None] + NA_KH - 1
    dc_idx = jnp.clip(w[None, :] - w[:, None], 1 - NA_KW, NA_KW - 1) + NA_KW - 1
    bias = rpb[:, dr_idx[:, None, :, None], dc_idx[None, :, None, :]].astype(jnp.float32)
    s_lat = jnp.einsum('brqhd,brjmhd->bhrqjm', q, kg).astype(jnp.float32) * scale + bias[None]
    s_lat = jnp.where(col_ok[:, None, :], s_lat, NEG_INF).reshape(B, H, rows, GRID_W, kh * GRID_W)
    s_ctx = jnp.einsum('brqhd,bchd->bhrqc', q, kc).astype(jnp.float32) * scale
    p = jax.nn.softmax(jnp.concatenate([s_lat, s_ctx], axis=-1), axis=-1).astype(v.dtype)
    p_lat = p[..., :kh * GRID_W].reshape(B, H, rows, GRID_W, kh, GRID_W)
    o = (jnp.einsum('bhrqjm,brjmhd->brqhd', p_lat, vg)
         + jnp.einsum('bhrqc,bchd->brqhd', p[..., kh * GRID_W:], vc))
    return o.reshape(B, n, H, d)


def mla_q(cq, p, row=None, col=None):
    B, n = cq.shape[:2]
    q = (rms_norm(cq, p['mla_q_norm']) @ p['mla_w_qb']).reshape(B, n, MLA_HEADS, MLA_NOPE + MLA_ROPE)
    if row is None:
        return q
    return jnp.concatenate([q[..., :MLA_NOPE], axial_rope(q[..., MLA_NOPE:], row, col)], axis=-1)


def mla_kv(ckv, kr, p, row=None, col=None):
    B, n = ckv.shape[:2]
    kv = (rms_norm(ckv, p['mla_kv_norm']) @ p['mla_w_kvb']).reshape(B, n, MLA_HEADS, MLA_NOPE + MLA_V)
    k_rot = kr[:, :, None, :]
    if row is not None:
        k_rot = axial_rope(k_rot, row, col)
    k = jnp.concatenate([kv[..., :MLA_NOPE], jnp.broadcast_to(k_rot, (B, n, MLA_HEADS, MLA_ROPE))], axis=-1)
    return k, kv[..., MLA_NOPE:]


def merge_branches(outs, gates_raw, p):
    B, n = gates_raw.shape[:2]
    g = jax.nn.sigmoid(gates_raw.reshape(B, n, N_BRANCH, D_MODEL))
    merged = g[:, :, 0] * (outs[0].reshape(B, n, BRANCH_W) @ p['w_branch'][0])
    for i in range(1, N_BRANCH):
        merged = merged + g[:, :, i] * (outs[i].reshape(B, n, BRANCH_W) @ p['w_branch'][i])
    return merged @ p['w_out']


def token_mixers(h, hc, row, col, p, need_ctx):
    (na_q, na_k, na_v, gq_q, gq_k, gq_v, sw_q, sw_k, sw_v,
     ml_cq, ml_ckv, ml_kr, gates) = split_in(h @ p['w_in'])
    (nac_q, nac_k, nac_v, gqc_q, gqc_k, gqc_v, swc_q, swc_k, swc_v,
     mlc_cq, mlc_ckv, mlc_kr, gates_c) = split_in(hc @ p['w_in'])
    sink = p['swa_sink'].reshape(SWA_KV_HEADS, SWA_HEADS // SWA_KV_HEADS)

    kac, vac = heads(nac_k, NA_HEADS), heads(nac_v, NA_HEADS)
    kbc = rms_norm(heads(gqc_k, GQA_KV_HEADS), p['gqa_k_norm'])
    vbc = heads(gqc_v, GQA_KV_HEADS)
    kcc, vcc = heads(swc_k, SWA_KV_HEADS), heads(swc_v, SWA_KV_HEADS)
    kdc, vdc = mla_kv(mlc_ckv, mlc_kr, p)

    o_a = neighborhood_attn(heads(na_q, NA_HEADS), heads(na_k, NA_HEADS), heads(na_v, NA_HEADS),
                            kac, vac, p['na_rpb'])
    qb = axial_rope(rms_norm(heads(gq_q, GQA_HEADS), p['gqa_q_norm']), row, col)
    kb = axial_rope(rms_norm(heads(gq_k, GQA_KV_HEADS), p['gqa_k_norm']), row, col)
    o_b = dense_blocks(group(qb, GQA_KV_HEADS), cat(kb, kbc), cat(heads(gq_v, GQA_KV_HEADS), vbc))
    qc = axial_rope(heads(sw_q, SWA_HEADS), row, col)
    kc = axial_rope(heads(sw_k, SWA_KV_HEADS), row, col)
    o_c = window_attn(group(qc, SWA_KV_HEADS), kc, heads(sw_v, SWA_KV_HEADS), kcc, vcc, sink)
    qd = mla_q(ml_cq, p, row, col)
    kd, vd = mla_kv(ml_ckv, ml_kr, p, row, col)
    o_d = dense_blocks(qd[:, :, :, None], cat(kd, kdc), cat(vd, vdc))
    y = merge_branches((o_a, o_b, o_c, o_d), gates, p)
    if not need_ctx:
        return y, None

    oc_a = attend(heads(nac_q, NA_HEADS)[:, :, :, None], kac, vac)
    oc_b = attend(group(rms_norm(heads(gqc_q, GQA_HEADS), p['gqa_q_norm']), GQA_KV_HEADS), kbc, vbc)
    oc_c = attend(group(heads(swc_q, SWA_HEADS), SWA_KV_HEADS), kcc, vcc, sink)
    oc_d = attend(mla_q(mlc_cq, p)[:, :, :, None], kdc, vdc)
    yc = merge_branches((oc_a, oc_b, oc_c, oc_d), gates_c, p)
    return y, yc


def expert_choice_ffn(h, w_router, w_gate, w_up, w_down):
    B, n, D = h.shape
    cap = CAPACITY * n // N_EXPERTS
    aff = jax.nn.softmax((h @ w_router).astype(jnp.float32), axis=-1)
    g, idx = lax.top_k(jnp.swapaxes(aff, 1, 2), cap)
    xg = jax.vmap(lambda hb, ib: hb[ib])(h, idx)
    a = jnp.einsum('becd,edf->becf', xg, w_gate)
    u = jnp.einsum('becd,edf->becf', xg, w_up)
    out = jnp.einsum('becf,efd->becd', jax.nn.silu(a) * u, w_down) * g[..., None].astype(h.dtype)
    return jax.vmap(lambda ob, ib: jnp.zeros((n, D), ob.dtype).at[ib.reshape(-1)].add(ob.reshape(-1, D)))(out, idx)


def hybrid_layer(x, xc, mod, mod_c, row, col, p, need_ctx):
    sh_a, sc_a, g_a, sh_f, sc_f, g_f = jnp.split(mod, N_MOD, axis=-1)
    shc_a, scc_a, gc_a, shc_f, scc_f, gc_f = jnp.split(mod_c, N_MOD, axis=-1)
    h = modulate(x, sh_a[:, None], sc_a[:, None])
    hc = modulate(xc, shc_a, scc_a)
    y, yc = token_mixers(h, hc, row, col, p, need_ctx)
    x = layer_norm(DEEPNORM_ALPHA * x + g_a[:, None] * y, p['ln1_g'], p['ln1_b'])
    h = modulate(x, sh_f[:, None], sc_f[:, None])
    y = expert_choice_ffn(h, p['w_router'], p['w_gate'], p['w_up'], p['w_down'])
    x = layer_norm(DEEPNORM_ALPHA * x + g_f[:, None] * y, p['ln2_g'], p['ln2_b'])
    if need_ctx:
        xc = layer_norm(DEEPNORM_ALPHA * xc + gc_a * yc, p['ln1_g'], p['ln1_b'])
        hc = modulate(xc, shc_f, scc_f)
        yc = expert_choice_ffn(hc, p['w_router'], p['w_gate'], p['w_up'], p['w_down'])
        xc = layer_norm(DEEPNORM_ALPHA * xc + gc_f * yc, p['ln2_g'], p['ln2_b'])
    return x, xc


def setup_inputs(seed: int = 0) -> dict:
    key = jax.random.key(seed)
    ks = jax.random.split(key, 25)

    def nrm(k, shape, scale):
        return jax.random.normal(k, shape, jnp.float32) * scale

    L, D = DEPTH, D_MODEL
    return {
        'x': nrm(ks[0], (BATCH, SEQ, D), 1.0),
        'c': nrm(ks[1], (BATCH, D), 1.0),
        'ctx': nrm(ks[2], (BATCH, CTX_LEN, D), 1.0),
        'c_ctx': nrm(ks[3], (D,), 1.0),
        'w_mod': nrm(ks[4], (L, D, N_MOD * D), 0.5 * D ** -0.5),
        'b_mod': nrm(ks[5], (L, N_MOD * D), 0.02),
        'w_in': nrm(ks[6], (L, D, IN_WIDTH), D ** -0.5),
        'na_rpb': nrm(ks[7], (L, NA_HEADS, 2 * NA_KH - 1, 2 * NA_KW - 1), 0.1),
        'gqa_q_norm': 1.0 + nrm(ks[8], (L, HEAD_DIM), 0.02),
        'gqa_k_norm': 1.0 + nrm(ks[9], (L, HEAD_DIM), 0.02),
        'swa_sink': nrm(ks[10], (L, SWA_HEADS), 0.5),
        'mla_q_norm': 1.0 + nrm(ks[11], (L, MLA_Q_LORA), 0.02),
        'mla_kv_norm': 1.0 + nrm(ks[12], (L, MLA_KV_LORA), 0.02),
        'mla_w_qb': nrm(ks[13], (L, MLA_Q_LORA, MLA_HEADS * (MLA_NOPE + MLA_ROPE)), MLA_Q_LORA ** -0.5),
        'mla_w_kvb': nrm(ks[14], (L, MLA_KV_LORA, MLA_HEADS * (MLA_NOPE + MLA_V)), MLA_KV_LORA ** -0.5),
        'w_branch': nrm(ks[15], (L, N_BRANCH, BRANCH_W, D), BRANCH_W ** -0.5),
        'w_out': nrm(ks[16], (L, D, D), DEEPNORM_BETA * D ** -0.5),
        'ln1_g': 1.0 + nrm(ks[17], (L, D), 0.02),
        'ln1_b': nrm(ks[18], (L, D), 0.02),
        'ln2_g': 1.0 + nrm(ks[19], (L, D), 0.02),
        'ln2_b': nrm(ks[20], (L, D), 0.02),
        'w_router': nrm(ks[21], (L, D, N_EXPERTS), D ** -0.5),
        'w_gate': nrm(ks[22], (L, N_EXPERTS, D, EXPERT_FF), D ** -0.5),
        'w_up': nrm(ks[23], (L, N_EXPERTS, D, EXPERT_FF), D ** -0.5),
        'w_down': nrm(ks[24], (L, N_EXPERTS, EXPERT_FF, D), DEEPNORM_BETA * EXPERT_FF ** -0.5),
    }


def reference(x, c, ctx, c_ctx, w_mod, b_mod, w_in, na_rpb, gqa_q_norm, gqa_k_norm, swa_sink,
              mla_q_norm, mla_kv_norm, mla_w_qb, mla_w_kvb, w_branch, w_out,
              ln1_g, ln1_b, ln2_g, ln2_b, w_router, w_gate, w_up, w_down):
    n = x.shape[1]
    pos = jnp.arange(n, dtype=jnp.int32)
    row = (pos // GRID_W).astype(jnp.float32)
    col = (pos % GRID_W).astype(jnp.float32)
    xc = ctx
    for l in range(DEPTH):
        p = {
            'w_in': w_in[l], 'na_rpb': na_rpb[l],
            'gqa_q_norm': gqa_q_norm[l], 'gqa_k_norm': gqa_k_norm[l], 'swa_sink': swa_sink[l],
            'mla_q_norm': mla_q_norm[l], 'mla_kv_norm': mla_kv_norm[l],
            'mla_w_qb': mla_w_qb[l], 'mla_w_kvb': mla_w_kvb[l],
            'w_branch': w_branch[l], 'w_out': w_out[l],
            'ln1_g': ln1_g[l], 'ln1_b': ln1_b[l], 'ln2_g': ln2_g[l], 'ln2_b': ln2_b[l],
            'w_router': w_router[l], 'w_gate': w_gate[l], 'w_up': w_up[l], 'w_down': w_down[l],
        }
        mod = jax.nn.silu(c) @ w_mod[l] + b_mod[l]
        mod_c = jax.nn.silu(c_ctx) @ w_mod[l] + b_mod[l]
        x, xc = hybrid_layer(x, xc, mod, mod_c, row, col, p, l < DEPTH - 1)
    return x
```

```python
import functools

import numpy as np
import jax
import jax.numpy as jnp
from jax import lax
from jax.experimental import pallas as pl
from jax.experimental.pallas import tpu as pltpu

F32 = jnp.float32
BF16 = jnp.bfloat16

GRID_W = 64
HEAD_DIM = 64
ROPE_THETA = 10000.0
EPS = 1e-6
NEG_INF = -1e30
NA_HEADS = 4
NA_KH = 8
NA_KW = 16
NA_WIN_ROWS = 10
SWA_WINDOW = 128
QBLOCK = 128
MLA_HEADS = 4
MLA_NOPE = 64
MLA_ROPE = 32
MLA_V = 64
MLA_Q_LORA = 192
MLA_KV_LORA = 128
N_BRANCH = 4
BRANCH_W = 256
N_EXPERTS = 16
CAPACITY = 2
LANES = 128
TOKEN_BLOCK = 128
VMEM_LIMIT = 56 * 1024 * 1024
HEAD_PERM = (0, 2, 1, 3)


def _cparams(*sem):
    return pltpu.CompilerParams(dimension_semantics=sem, vmem_limit_bytes=VMEM_LIMIT)


def _dot(a, b):
    return jnp.dot(a, b, preferred_element_type=F32)


def _dot_nt(a, b):
    return lax.dot_general(a, b, (((1,), (1,)), ((), ())), preferred_element_type=F32)


def _ln(x):
    mu = jnp.mean(x, axis=-1, keepdims=True)
    xc = x - mu
    var = jnp.mean(xc * xc, axis=-1, keepdims=True)
    return xc * lax.rsqrt(var + EPS)


def _modulate(x, shift, scale):
    return _ln(x) * (1.0 + scale) + shift


def _lane(shape, dim=None):
    return lax.broadcasted_iota(jnp.int32, shape, len(shape) - 1 if dim is None else dim)


def _rope(y, cos, sin_signed, half):
    w = y.shape[-1]
    low = (_lane((1, w)) & (2 * half - 1)) < half
    partner = jnp.where(low, pltpu.roll(y, w - half, 1), pltpu.roll(y, half, 1))
    return y * cos + partner * sin_signed


def _mod_kernel(c_ref, w_ref, b_ref, o_ref):
    c = c_ref[...]
    s = c / (1.0 + jnp.exp(-c))
    o_ref[0] = _dot(s.astype(BF16), w_ref[0].astype(BF16)) + b_ref[0]


def _mod_call(cc, w_mod, b_mod):
    L, D, N = w_mod.shape
    tn = N // 4
    return pl.pallas_call(
        _mod_kernel,
        grid=(L, N // tn),
        in_specs=[pl.BlockSpec((8, D), lambda l, j: (0, 0)),
                  pl.BlockSpec((1, D, tn), lambda l, j: (l, 0, j)),
                  pl.BlockSpec((1, 1, tn), lambda l, j: (l, 0, j))],
        out_specs=pl.BlockSpec((1, 8, tn), lambda l, j: (l, 0, j)),
        out_shape=jax.ShapeDtypeStruct((L, 8, N), F32),
        compiler_params=_cparams("arbitrary", "arbitrary"),
        name="mod_vectors",
    )(cc, w_mod, b_mod.reshape(L, 1, N))


_C_NA = 0
_C_BQ = 768
_C_BK = 1280
_C_BV = 1408
_C_CQ = 1536
_C_CK = 2048
_C_CV = 2176
_C_DQ = 2304
_C_DKV = 2560
_C_DKR = 2688
_C_END = 2816


def _inproj_kernel(x_ref, sh_ref, sc_ref, w_ref, cos_ref, sin_ref, cosd_ref, sind_ref,
                   gqb_ref, gk2_ref, gmq_ref, gmkv_ref, wqb_ref, wkvb_ref,
                   naq, nak, nav, bq, bk, bv, cq, ck, cv, dq, dk, dv):
    hb = _modulate(x_ref[0], sh_ref[0], sc_ref[0]).astype(BF16)

    def seg(a, b):
        return _dot(hb, w_ref[:, a:b])

    cos = cos_ref[...]
    sin = sin_ref[...]
    cosd = cosd_ref[...]
    sind = sind_ref[...]
    qscale = HEAD_DIM ** -0.5

    z = seg(_C_NA, _C_NA + 768)
    naq[0] = (z[:, 0:256] * qscale).astype(BF16)
    nak[0] = z[:, 256:512].astype(BF16)
    nav[0] = z[:, 512:768].astype(BF16)

    z = seg(_C_BQ, _C_BK)
    for s in range(4):
        zs = z[:, s * LANES:(s + 1) * LANES]
        ss = jnp.sum(zs * zs, axis=-1, keepdims=True) * (1.0 / HEAD_DIM)
        y = zs * lax.rsqrt(ss + EPS) * gqb_ref[s:s + 1, :]
        bq[0, s] = (_rope(y, cos, sin, 16) * qscale).astype(BF16)
    z = seg(_C_BK, _C_BV)
    left = _lane((1, LANES)) < HEAD_DIM
    z2 = z * z
    ss0 = jnp.sum(jnp.where(left, z2, 0.0), axis=-1, keepdims=True)
    ss1 = jnp.sum(jnp.where(left, 0.0, z2), axis=-1, keepdims=True)
    inv = jnp.where(left, lax.rsqrt(ss0 * (1.0 / HEAD_DIM) + EPS), lax.rsqrt(ss1 * (1.0 / HEAD_DIM) + EPS))
    bk[0] = _rope(z * inv * gk2_ref[...], cos, sin, 16).astype(BF16)
    bv[0] = seg(_C_BV, _C_CQ).astype(BF16)

    z = seg(_C_CQ, _C_CK)
    for s in range(4):
        cq[0, s] = (_rope(z[:, s * LANES:(s + 1) * LANES], cos, sin, 16) * qscale).astype(BF16)
    ck[0] = _rope(seg(_C_CK, _C_CV), cos, sin, 16).astype(BF16)
    cv[0] = seg(_C_CV, _C_DQ).astype(BF16)

    z = seg(_C_DQ, _C_DKV)
    ss = jnp.sum(z * z, axis=-1, keepdims=True) * (1.0 / MLA_Q_LORA)
    cqn = (z * lax.rsqrt(ss + EPS) * gmq_ref[...]).astype(BF16)
    qd = _dot(cqn, wqb_ref[...])
    dscale = (MLA_NOPE + MLA_ROPE) ** -0.5
    for h in range(MLA_HEADS):
        nope = qd[:, h * 256:h * 256 + LANES] * dscale
        rot = _rope(qd[:, h * 256 + LANES:(h + 1) * 256], cosd, sind, 8) * dscale
        dq[0, h] = jnp.concatenate([nope, rot], axis=-1).astype(BF16)
    z = seg(_C_DKV, _C_DKR)
    ss = jnp.sum(z * z, axis=-1, keepdims=True) * (1.0 / MLA_KV_LORA)
    ckvn = (z * lax.rsqrt(ss + EPS) * gmkv_ref[...]).astype(BF16)
    kv = _dot(ckvn, wkvb_ref[...])
    kr = _rope(seg(_C_DKR, _C_END), cosd, sind, 8)
    for p in range(2):
        dk[0, p] = jnp.concatenate([kv[:, p * LANES:(p + 1) * LANES], kr], axis=-1).astype(BF16)
    dv[0] = kv[:, 256:512].astype(BF16)


def _inproj_call(x, shift, scale, lw, tabs):
    B, n, D = x.shape
    tm = min(512, n)
    cos, sin, cosd, sind = tabs
    row = lambda b, i: (b, i, 0)
    hrow = lambda b, i: (b, 0, i, 0)
    const2 = lambda b, i: (0, 0)
    tab = pl.BlockSpec((tm, LANES), lambda b, i: (i, 0))
    mod = pl.BlockSpec((1, 1, D), lambda b, i: (b, 0, 0))
    outs = [((B, n, 256), pl.BlockSpec((1, tm, 256), row))] * 3
    for kd in (LANES, LANES):
        outs += [((B, 4, n, kd), pl.BlockSpec((1, 4, tm, kd), hrow)),
                 ((B, n, LANES), pl.BlockSpec((1, tm, LANES), row)),
                 ((B, n, LANES), pl.BlockSpec((1, tm, LANES), row))]
    outs += [((B, 4, n, 256), pl.BlockSpec((1, 4, tm, 256), hrow)),
             ((B, 2, n, 256), pl.BlockSpec((1, 2, tm, 256), hrow)),
             ((B, n, 256), pl.BlockSpec((1, tm, 256), row))]
    return pl.pallas_call(
        _inproj_kernel,
        grid=(B, n // tm),
        in_specs=[pl.BlockSpec((1, tm, D), row), mod, mod,
                  pl.BlockSpec((D, _C_END), const2), tab, tab, tab, tab,
                  pl.BlockSpec((4, LANES), const2), pl.BlockSpec((1, LANES), const2),
                  pl.BlockSpec((1, 256), const2), pl.BlockSpec((1, LANES), const2),
                  pl.BlockSpec((256, 1024), const2), pl.BlockSpec((LANES, 512), const2)],
        out_specs=[o[1] for o in outs],
        out_shape=[jax.ShapeDtypeStruct(o[0], BF16) for o in outs],
        compiler_params=_cparams("arbitrary", "arbitrary"),
        name="in_proj",
    )(x, shift, scale, lw["w_qkv"], cos, sin, cosd, sind,
      lw["gqb"], lw["gk2"], lw["gmq"], lw["gmkv"], lw["wqb"], lw["wkvb"])


def _flash_kernel(q_ref, k_ref, v_ref, kc_ref, vc_ref, o_ref, acc_ref, m_ref, l_ref, *, tq, tk, nk):
    q = q_ref[0].reshape(2 * tq, q_ref.shape[-1])
    m_ref[...] = jnp.full(m_ref.shape, NEG_INF, F32)
    l_ref[...] = jnp.zeros(l_ref.shape, F32)
    acc_ref[...] = jnp.zeros(acc_ref.shape, F32)

    def step(k, v):
        s = _dot_nt(q, k)
        m_prev = m_ref[...]
        m_new = jnp.maximum(m_prev, jnp.max(s, axis=-1, keepdims=True))
        a = jnp.exp(m_prev - m_new)
        p = jnp.exp(s - m_new)
        l_ref[...] = a * l_ref[...] + jnp.sum(p, axis=-1, keepdims=True)
        acc_ref[...] = a * acc_ref[...] + _dot(p.astype(BF16), v)
        m_ref[...] = m_new

    def body(j, carry):
        off = pl.multiple_of(j * tk, tk)
        step(k_ref[0, 0, pl.ds(off, tk), :], v_ref[0, pl.ds(off, tk), :])
        return carry

    lax.fori_loop(0, nk, body, 0)
    step(kc_ref[0, 0], vc_ref[0])
    o = acc_ref[...] / l_ref[...]
    o_ref[0] = jnp.where(_lane((1, LANES)) < HEAD_DIM, o[:tq], o[tq:]).astype(BF16)


def _flash_call(q, k, v, kc, vc, *, k_per_pair, v_per_pair):
    B, _, n, kd = q.shape
    C = kc.shape[2]
    tq = min(256, n)
    tk = min(512, n)
    kidx = (lambda b, p, i: (b, p, 0, 0)) if k_per_pair else (lambda b, p, i: (b, 0, 0, 0))
    vidx = (lambda b, p, i: (b, 0, p)) if v_per_pair else (lambda b, p, i: (b, 0, 0))
    kern = functools.partial(_flash_kernel, tq=tq, tk=tk, nk=n // tk)
    return pl.pallas_call(
        kern,
        grid=(B, 2, n // tq),
        in_specs=[pl.BlockSpec((1, 2, tq, kd), lambda b, p, i: (b, p, i, 0)),
                  pl.BlockSpec((1, 1, n, kd), kidx),
                  pl.BlockSpec((1, n, LANES), vidx),
                  pl.BlockSpec((1, 1, C, kd), kidx),
                  pl.BlockSpec((1, C, LANES), vidx)],
        out_specs=pl.BlockSpec((1, tq, LANES), lambda b, p, i: (b, i, p)),
        out_shape=jax.ShapeDtypeStruct((B, n, 256), BF16),
        scratch_shapes=[pltpu.VMEM((2 * tq, LANES), F32), pltpu.VMEM((2 * tq, 1), F32),
                        pltpu.VMEM((2 * tq, 1), F32)],
        compiler_params=_cparams("arbitrary", "arbitrary", "arbitrary"),
        name="dense_attn",
    )(q, k, v, kc, vc)


def _na_kernel(q_ref, k_ref, v_ref, kc_ref, vc_ref, tl_ref, tr_ref, o_ref, *, rows):
    i = pl.program_id(1)
    r0 = 2 * i
    w0 = jnp.clip(r0 - NA_KH // 2, 0, rows - NA_WIN_ROWS)
    start = pl.multiple_of(w0 * GRID_W, GRID_W)
    kw = k_ref[0, pl.ds(start, NA_WIN_ROWS * GRID_W), :]
    vw = v_ref[0, pl.ds(start, NA_WIN_ROWS * GRID_W), :]
    kc = kc_ref[0]
    vc = vc_ref[0]
    q = q_ref[0]
    head_of_lane = _lane((1, 256)) >> 6

    def table_index(a, j):
        qr = r0 + a
        kr = w0 + j
        st = jnp.clip(qr - NA_KH // 2, 0, rows - NA_KH)
        ok = (kr >= st) & (kr < st + NA_KH)
        return jnp.where(ok, kr - qr + NA_KH - 1, 2 * NA_KH - 1)

    out = jnp.zeros((2 * GRID_W, 256), F32)
    for h in range(NA_HEADS):
        qm = jnp.where(head_of_lane == h, q, jnp.zeros_like(q))
        bias_rows = []
        for a in range(2):
            blocks = [tl_ref[h * 16 + table_index(a, 2 * jp)] + tr_ref[h * 16 + table_index(a, 2 * jp + 1)]
                      for jp in range(NA_WIN_ROWS // 2)]
            bias_rows.append(jnp.concatenate(blocks, axis=1))
        s_lat = _dot_nt(qm, kw) + jnp.concatenate(bias_rows, axis=0)
        s_ctx = _dot_nt(qm, kc)
        m = jnp.maximum(jnp.max(s_lat, axis=-1, keepdims=True), jnp.max(s_ctx, axis=-1, keepdims=True))
        p_lat = jnp.exp(s_lat - m)
        p_ctx = jnp.exp(s_ctx - m)
        l = jnp.sum(p_lat, axis=-1, keepdims=True) + jnp.sum(p_ctx, axis=-1, keepdims=True)
        pv = _dot(p_lat.astype(BF16), vw) + _dot(p_ctx.astype(BF16), vc)
        out = jnp.where(head_of_lane == h, pv / l, out)
    o_ref[0] = out.astype(BF16)


def _na_call(q, k, v, kc, vc, tl, tr):
    B, n, _ = q.shape
    C = kc.shape[1]
    rows = n // GRID_W
    assert rows >= NA_WIN_ROWS and rows % 2 == 0
    full = lambda b, i: (b, 0, 0)
    return pl.pallas_call(
        functools.partial(_na_kernel, rows=rows),
        grid=(B, rows // 2),
        in_specs=[pl.BlockSpec((1, 2 * GRID_W, 256), lambda b, i: (b, i, 0)),
                  pl.BlockSpec((1, n, 256), full), pl.BlockSpec((1, n, 256), full),
                  pl.BlockSpec((1, C, 256), full), pl.BlockSpec((1, C, 256), full),
                  pl.BlockSpec(tl.shape, lambda b, i: (0, 0, 0)),
                  pl.BlockSpec(tr.shape, lambda b, i: (0, 0, 0))],
        out_specs=pl.BlockSpec((1, 2 * GRID_W, 256), lambda b, i: (b, i, 0)),
        out_shape=jax.ShapeDtypeStruct((B, n, 256), BF16),
        compiler_params=_cparams("arbitrary", "arbitrary"),
        name="nbr_attn",
    )(q, k, v, kc, vc, tl, tr)


def _win_kernel(sink_ref, q_ref, k_ref, v_ref, kc_ref, vc_ref, o_ref, *, n):
    i = pl.program_id(1)
    band = 3 * QBLOCK
    ws = pl.multiple_of(jnp.clip((i - 1) * QBLOCK, 0, n - band), QBLOCK)
    kw = k_ref[0, pl.ds(ws, band), :]
    vw = v_ref[0, pl.ds(ws, band), :]
    kc = kc_ref[0]
    vc = vc_ref[0]
    qpos = i * QBLOCK + lax.broadcasted_iota(jnp.int32, (QBLOCK, band), 0)
    kpos = ws + lax.broadcasted_iota(jnp.int32, (QBLOCK, band), 1)
    ok = jnp.abs(qpos - kpos) <= SWA_WINDOW
    outs = []
    for s in range(4):
        q = q_ref[0, s]
        s_lat = jnp.where(ok, _dot_nt(q, kw), NEG_INF)
        s_ctx = _dot_nt(q, kc)
        snk = sink_ref[s]
        m = jnp.maximum(jnp.maximum(jnp.max(s_lat, axis=-1, keepdims=True),
                                    jnp.max(s_ctx, axis=-1, keepdims=True)), snk)
        p_lat = jnp.exp(s_lat - m)
        p_ctx = jnp.exp(s_ctx - m)
        l = (jnp.sum(p_lat, axis=-1, keepdims=True) + jnp.sum(p_ctx, axis=-1, keepdims=True)
             + jnp.exp(snk - m))
        outs.append((_dot(p_lat.astype(BF16), vw) + _dot(p_ctx.astype(BF16), vc)) / l)
    left = _lane((1, LANES)) < HEAD_DIM
    o_ref[0] = jnp.concatenate([jnp.where(left, outs[0], outs[1]),
                                jnp.where(left, outs[2], outs[3])], axis=-1).astype(BF16)


def _win_call(sink, q, k, v, kc, vc):
    B, _, n, _ = q.shape
    C = kc.shape[1]
    assert n >= 3 * QBLOCK
    full = lambda b, i, s: (b, 0, 0)
    return pl.pallas_call(
        functools.partial(_win_kernel, n=n),
        grid_spec=pltpu.PrefetchScalarGridSpec(
            num_scalar_prefetch=1,
            grid=(B, n // QBLOCK),
            in_specs=[pl.BlockSpec((1, 4, QBLOCK, LANES), lambda b, i, s: (b, 0, i, 0)),
                      pl.BlockSpec((1, n, LANES), full), pl.BlockSpec((1, n, LANES), full),
                      pl.BlockSpec((1, C, LANES), full), pl.BlockSpec((1, C, LANES), full)],
            out_specs=pl.BlockSpec((1, QBLOCK, 256), lambda b, i, s: (b, i, 0))),
        out_shape=jax.ShapeDtypeStruct((B, n, 256), BF16),
        compiler_params=_cparams("arbitrary", "arbitrary"),
        name="window_attn",
    )(sink, q, k, v, kc, vc)


def _attend(q, k, v, sink=None):
    s = _dot_nt(q, k)
    m = jnp.max(s, axis=-1, keepdims=True)
    if sink is not None:
        m = jnp.maximum(m, sink)
    p = jnp.exp(s - m)
    l = jnp.sum(p, axis=-1, keepdims=True)
    if sink is not None:
        l = l + jnp.exp(sink - m)
    return _dot(p.astype(BF16), v) / l


def _ctx_attn_kernel(sink_ref, naq, nak, nav, bq, bk, bv, cq, ck, cv, dq, dk, dv, oa, ob, oc, od):
    left = _lane((1, LANES)) < HEAD_DIM
    head_of_lane = _lane((1, 256)) >> 6
    q = naq[0]
    out = jnp.zeros(q.shape, F32)
    for h in range(NA_HEADS):
        qm = jnp.where(head_of_lane == h, q, jnp.zeros_like(q))
        out = jnp.where(head_of_lane == h, _attend(qm, nak[0], nav[0]), out)
    oa[0] = out.astype(BF16)
    o = [_attend(bq[0, s], bk[0], bv[0]) for s in range(4)]
    ob[0] = jnp.concatenate([jnp.where(left, o[0], o[1]), jnp.where(left, o[2], o[3])], axis=-1).astype(BF16)
    o = [_attend(cq[0, s], ck[0], cv[0], sink_ref[s]) for s in range(4)]
    oc[0] = jnp.concatenate([jnp.where(left, o[0], o[1]), jnp.where(left, o[2], o[3])], axis=-1).astype(BF16)
    o = [_attend(dq[0, h], dk[0, h // 2], dv[0][:, (h // 2) * LANES:(h // 2 + 1) * LANES])
         for h in range(MLA_HEADS)]
    od[0] = jnp.concatenate([jnp.where(left, o[0], o[1]), jnp.where(left, o[2], o[3])], axis=-1).astype(BF16)


def _ctx_attn_call(sink, pc):
    B, C, _ = pc[0].shape
    names = pc
    specs = []
    for a in names:
        nd = a.ndim
        specs.append(pl.BlockSpec((1,) + a.shape[1:], (lambda b, s, nd=nd: (b,) + (0,) * (nd - 1))))
    out_spec = pl.BlockSpec((1, C, 256), lambda b, s: (b, 0, 0))
    return pl.pallas_call(
        _ctx_attn_kernel,
        grid_spec=pltpu.PrefetchScalarGridSpec(
            num_scalar_prefetch=1, grid=(B,), in_specs=specs, out_specs=[out_spec] * 4),
        out_shape=[jax.ShapeDtypeStruct((B, C, 256), BF16)] * 4,
        compiler_params=_cparams("arbitrary"),
        name="ctx_attn",
    )(sink, *pc)


def _merge_kernel(x_ref, oa, ob, oc, od, sha, sca, ga, shf, scf, wg_ref, wb_ref, wo_ref,
                  l1g, l1b, wrh_ref, wrl_ref, x1_ref, h2_ref, aff_ref, *, alpha):
    x = x_ref[0]
    D = x.shape[-1]
    hb = _modulate(x, sha[0], sca[0]).astype(BF16)
    merged = None
    for i, o in enumerate((oa, ob, oc, od)):
        g = 1.0 / (1.0 + jnp.exp(-_dot(hb, wg_ref[:, i * D:(i + 1) * D])))
        term = g * _dot(o[0], wb_ref[i])
        merged = term if merged is None else merged + term
    y = _dot(merged.astype(BF16), wo_ref[...])
    x1 = _ln(alpha * x + ga[0] * y) * l1g[...] + l1b[...]
    x1_ref[0] = x1
    h2 = _modulate(x1, shf[0], scf[0])
    h2_hi = h2.astype(BF16)
    h2_ref[0] = h2_hi
    h2_lo = (h2 - h2_hi.astype(F32)).astype(BF16)
    logits = _dot(h2_hi, wrh_ref[...]) + _dot(h2_hi, wrl_ref[...]) + _dot(h2_lo, wrh_ref[...])
    logits = jnp.where(_lane((1, LANES)) < N_EXPERTS, logits, NEG_INF)
    e = jnp.exp(logits - jnp.max(logits, axis=-1, keepdims=True))
    aff_t = (e / jnp.sum(e, axis=-1, keepdims=True)).T
    for k in range(aff_ref.shape[1]):
        aff_ref[0, k] = aff_t[:N_EXPERTS, k * TOKEN_BLOCK:(k + 1) * TOKEN_BLOCK]


def _merge_call(x, outs, mods, lw, alpha):
    B, n, D = x.shape
    tm = min(256, n)
    nb = tm // TOKEN_BLOCK
    row = lambda b, i: (b, i, 0)
    mod = pl.BlockSpec((1, 1, D), lambda b, i: (b, 0, 0))
    c2 = lambda b, i: (0, 0)
    obr = pl.BlockSpec((1, tm, 256), row)
    return pl.pallas_call(
        functools.partial(_merge_kernel, alpha=alpha),
        grid=(B, n // tm),
        in_specs=[pl.BlockSpec((1, tm, D), row), obr, obr, obr, obr, mod, mod, mod, mod, mod,
                  pl.BlockSpec((D, N_BRANCH * D), c2),
                  pl.BlockSpec((N_BRANCH, BRANCH_W, D), lambda b, i: (0, 0, 0)),
                  pl.BlockSpec((D, D), c2), pl.BlockSpec((1, D), c2), pl.BlockSpec((1, D), c2),
                  pl.BlockSpec((D, LANES), c2), pl.BlockSpec((D, LANES), c2)],
        out_specs=[pl.BlockSpec((1, tm, D), row), pl.BlockSpec((1, tm, D), row),
                   pl.BlockSpec((1, nb, N_EXPERTS, TOKEN_BLOCK), lambda b, i: (b, i, 0, 0))],
        out_shape=[jax.ShapeDtypeStruct((B, n, D), F32), jax.ShapeDtypeStruct((B, n, D), BF16),
                   jax.ShapeDtypeStruct((B, n // TOKEN_BLOCK, N_EXPERTS, TOKEN_BLOCK), F32)],
        compiler_params=_cparams("arbitrary", "arbitrary"),
        name="merge_router",
    )(x, *outs, *mods, lw["w_gates"], lw["w_branch"], lw["w_out"], lw["ln1_g"], lw["ln1_b"],
      lw["wr_hi"], lw["wr_lo"])


def _select_kernel(aff_ref, pos_ref, off_ref, tlo_ref, thi_ref, *, cap, slot_tile):
    a = aff_ref[0]
    nb = a.shape[0]
    rows = nb * N_EXPERTS
    bits = lax.bitcast_convert_type(a, jnp.int32)
    capf = jnp.float32(cap)

    def count(mask):
        c = jnp.sum(jnp.where(mask, 1.0, 0.0), axis=0)
        return jnp.broadcast_to(jnp.sum(c, axis=-1, keepdims=True), c.shape)

    def search(it, lo):
        cand = lo | lax.shift_left(jnp.int32(1), 30 - it)
        return jnp.where(count(bits >= cand[None]) >= capf, cand, lo)

    thr = lax.fori_loop(0, 31, search, jnp.zeros((N_EXPERTS, TOKEN_BLOCK), jnp.int32))

    r = lax.broadcasted_iota(jnp.int32, (rows, rows), 0)
    c = lax.broadcasted_iota(jnp.int32, (rows, rows), 1)
    earlier = jnp.where(((r & (N_EXPERTS - 1)) == (c & (N_EXPERTS - 1))) & ((c >> 4) < (r >> 4)), 1.0, 0.0).astype(BF16)
    ti = lax.broadcasted_iota(jnp.int32, (TOKEN_BLOCK, TOKEN_BLOCK), 0)
    tj = lax.broadcasted_iota(jnp.int32, (TOKEN_BLOCK, TOKEN_BLOCK), 1)
    tri = jnp.where(ti <= tj, 1.0, 0.0).astype(BF16)
    ones = jnp.ones((TOKEN_BLOCK, TOKEN_BLOCK), BF16)

    def prefix(mask):
        m2 = jnp.where(mask, 1.0, 0.0).reshape(rows, TOKEN_BLOCK)
        mb = m2.astype(BF16)
        within = _dot(mb, tri)
        tot = _dot(mb, ones)
        off = _dot(earlier, tot.astype(BF16))
        shp = (nb, N_EXPERTS, TOKEN_BLOCK)
        return (off + within - m2).reshape(shp), off.reshape(shp), tot.reshape(shp)

    gt = bits > thr[None]
    eq = bits == thr[None]
    need = capf - count(gt)
    eq_rank, _, _ = prefix(eq)
    sel = gt | (eq & (eq_rank < need[None]))
    excl, off, tot = prefix(sel)
    pos_ref[0] = jnp.where(sel, excl, -1.0)
    off_ref[0] = off.astype(jnp.int32)
    tile_start = (_lane((1, 1, TOKEN_BLOCK)) * slot_tile).astype(F32)
    tlo_ref[0] = jnp.sum(jnp.where(off + tot <= tile_start, 1, 0), axis=0).astype(jnp.int32)
    thi_ref[0] = jnp.sum(jnp.where(off < tile_start + slot_tile, 1, 0), axis=0).astype(jnp.int32)


def _select_call(aff, cap, slot_tile):
    B, nb, E, _ = aff.shape
    blk = pl.BlockSpec((1, nb, E, TOKEN_BLOCK), lambda b: (b, 0, 0, 0))
    rng = pl.BlockSpec((1, E, TOKEN_BLOCK), lambda b: (b, 0, 0))
    return pl.pallas_call(
        functools.partial(_select_kernel, cap=cap, slot_tile=slot_tile),
        grid=(B,),
        in_specs=[blk],
        out_specs=[blk, blk, rng, rng],
        out_shape=[jax.ShapeDtypeStruct(aff.shape, F32), jax.ShapeDtypeStruct(aff.shape, jnp.int32),
                   jax.ShapeDtypeStruct((B, E, TOKEN_BLOCK), jnp.int32),
                   jax.ShapeDtypeStruct((B, E, TOKEN_BLOCK), jnp.int32)],
        compiler_params=_cparams("arbitrary"),
        name="expert_select",
    )(aff)


def _gather_kernel(tlo_ref, thi_ref, pos_ref, aff_ref, h_ref, xg_ref, g_ref, acc_ref, gacc_ref, *, slot_tile):
    b = pl.program_id(0)
    e = pl.program_id(1)
    n_tiles = xg_ref.shape[2] // slot_tile
    for t in range(n_tiles):
        acc_ref[...] = jnp.zeros(acc_ref.shape, F32)
        gacc_ref[...] = jnp.zeros(gacc_ref.shape, F32)
        slot = (t * slot_tile + lax.broadcasted_iota(jnp.int32, (slot_tile, TOKEN_BLOCK), 0)).astype(F32)

        def body(blk, carry):
            p = pos_ref[0, blk, pl.ds(e, 1), :]
            a = aff_ref[0, blk, pl.ds(e, 1), :]
            hit = p == slot
            tok = pl.multiple_of(blk * TOKEN_BLOCK, TOKEN_BLOCK)
            acc_ref[...] += _dot(jnp.where(hit, 1.0, 0.0).astype(BF16), h_ref[0, pl.ds(tok, TOKEN_BLOCK), :])
            gacc_ref[...] += jnp.where(hit, a, 0.0)
            return carry

        lax.fori_loop(tlo_ref[b, e, t], thi_ref[b, e, t], body, 0)
        xg_ref[0, 0, t * slot_tile:(t + 1) * slot_tile, :] = acc_ref[...].astype(BF16)
        g_ref[0, 0, t * slot_tile:(t + 1) * slot_tile, :] = jnp.sum(gacc_ref[...], axis=-1, keepdims=True)


def _gather_call(tlo, thi, pos, aff, h2, cap_pad, slot_tile):
    B, n, D = h2.shape
    nb = n // TOKEN_BLOCK
    blk = pl.BlockSpec((1, nb, N_EXPERTS, TOKEN_BLOCK), lambda b, e, *_: (b, 0, 0, 0))
    return pl.pallas_call(
        functools.partial(_gather_kernel, slot_tile=slot_tile),
        grid_spec=pltpu.PrefetchScalarGridSpec(
            num_scalar_prefetch=2,
            grid=(B, N_EXPERTS),
            in_specs=[blk, blk, pl.BlockSpec((1, n, D), lambda b, e, *_: (b, 0, 0))],
            out_specs=[pl.BlockSpec((1, 1, cap_pad, D), lambda b, e, *_: (b, e, 0, 0)),
                       pl.BlockSpec((1, 1, cap_pad, 1), lambda b, e, *_: (b, e, 0, 0))],
            scratch_shapes=[pltpu.VMEM((slot_tile, D), F32), pltpu.VMEM((slot_tile, TOKEN_BLOCK), F32)]),
        out_shape=[jax.ShapeDtypeStruct((B, N_EXPERTS, cap_pad, D), BF16),
                   jax.ShapeDtypeStruct((B, N_EXPERTS, cap_pad, 1), F32)],
        compiler_params=_cparams("arbitrary", "arbitrary"),
        name="expert_gather",
    )(tlo, thi, pos, aff, h2)


def _ffn_kernel(xg_ref, g_ref, wg_ref, wu_ref, wd_ref, y_ref):
    xg = xg_ref[0, 0]
    a = _dot(xg, wg_ref[0])
    u = _dot(xg, wu_ref[0])
    hmid = (a / (1.0 + jnp.exp(-a)) * u).astype(BF16)
    y_ref[0, 0] = (_dot(hmid, wd_ref[0]) * g_ref[0, 0]).astype(BF16)


def _ffn_call(xg, g, lw):
    B, E, cp, D = xg.shape
    F = lw["w_gate"].shape[-1]
    tok = lambda e, b: (b, e, 0, 0)
    wsp = lambda e, b: (e, 0, 0)
    return pl.pallas_call(
        _ffn_kernel,
        grid=(E, B),
        in_specs=[pl.BlockSpec((1, 1, cp, D), tok), pl.BlockSpec((1, 1, cp, 1), tok),
                  pl.BlockSpec((1, D, F), wsp), pl.BlockSpec((1, D, F), wsp), pl.BlockSpec((1, F, D), wsp)],
        out_specs=pl.BlockSpec((1, 1, cp, D), tok),
        out_shape=jax.ShapeDtypeStruct((B, E, cp, D), BF16),
        compiler_params=_cparams("arbitrary", "arbitrary"),
        name="expert_mlp",
    )(xg, g, lw["w_gate"], lw["w_up"], lw["w_down"])


def _combine_kernel(off_ref, pos_ref, yw_ref, y_ref, *, window, blocks_per_step):
    b = pl.program_id(0)
    t = pl.program_id(2)
    cap_pad = yw_ref.shape[2]
    nb_total = pl.num_programs(2) * blocks_per_step
    for k in range(blocks_per_step):
        blk = t * blocks_per_step + k
        acc = jnp.zeros((TOKEN_BLOCK, yw_ref.shape[-1]), F32)
        for e in range(N_EXPERTS):
            start = off_ref[(b * nb_total + blk) * N_EXPERTS + e]
            w0 = pl.multiple_of(jnp.minimum(start & -16, cap_pad - window), 16)
            p = pos_ref[0, k, e:e + 1, :]
            slot = (w0 + lax.broadcasted_iota(jnp.int32, (window, TOKEN_BLOCK), 0)).astype(F32)
            hit = jnp.where(p == slot, 1.0, 0.0).T.astype(BF16)
            acc = acc + _dot(hit, yw_ref[0, e, pl.ds(w0, window), :])
        y_ref[0, k * TOKEN_BLOCK:(k + 1) * TOKEN_BLOCK, :] = acc


def _combine_call(off_flat, pos, yw, n, D):
    B, E, cap_pad, _ = yw.shape
    nb = n // TOKEN_BLOCK
    window = min(256, cap_pad)
    dh = min(512, D)
    bps = min(4, nb)
    return pl.pallas_call(
        functools.partial(_combine_kernel, window=window, blocks_per_step=bps),
        grid_spec=pltpu.PrefetchScalarGridSpec(
            num_scalar_prefetch=1,
            grid=(B, D // dh, nb // bps),
            in_specs=[pl.BlockSpec((1, bps, E, TOKEN_BLOCK), lambda b, d, t, *_: (b, t, 0, 0)),
                      pl.BlockSpec((1, E, cap_pad, dh), lambda b, d, t, *_: (b, 0, 0, d))],
            out_specs=pl.BlockSpec((1, bps * TOKEN_BLOCK, dh), lambda b, d, t, *_: (b, t, d))),
        out_shape=jax.ShapeDtypeStruct((B, n, D), F32),
        compiler_params=_cparams("arbitrary", "arbitrary", "arbitrary"),
        name="expert_combine",
    )(off_flat, pos, yw)


def _ln2_kernel(x_ref, y_ref, g_ref, lg, lb, o_ref, *, alpha):
    o_ref[0] = _ln(alpha * x_ref[0] + g_ref[0] * y_ref[0]) * lg[...] + lb[...]


def _ln2_call(x1, y, gf, lg, lb, alpha):
    B, n, D = x1.shape
    tm = min(512, n)
    row = pl.BlockSpec((1, tm, D), lambda b, i: (b, i, 0))
    vec = pl.BlockSpec((1, D), lambda b, i: (0, 0))
    return pl.pallas_call(
        functools.partial(_ln2_kernel, alpha=alpha),
        grid=(B, n // tm),
        in_specs=[row, row, pl.BlockSpec((1, 1, D), lambda b, i: (b, 0, 0)), vec, vec],
        out_specs=row,
        out_shape=jax.ShapeDtypeStruct((B, n, D), F32),
        compiler_params=_cparams("arbitrary", "arbitrary"),
        name="ffn_residual_ln",
    )(x1, y, gf, lg, lb)


def _slot_pad(w, slot):
    z = jnp.zeros_like(w)
    return jnp.concatenate([w, z] if slot == 0 else [z, w], axis=-1)


def _prep_layer(l, p):
    w_in = p["w_in"][l]
    D = w_in.shape[0]
    hd = HEAD_DIM

    def gq_cols(base):
        q = [_slot_pad(w_in[:, base + h * hd: base + (h + 1) * hd], h // 2) for h in HEAD_PERM]
        return q + [w_in[:, base + 256: base + 512]]

    zc = lambda k: jnp.zeros((D, k), F32)
    cols = [w_in[:, 0:768]] + gq_cols(768) + gq_cols(1280)
    cols += [w_in[:, 1792:1984], zc(64), w_in[:, 1984:2112], w_in[:, 2112:2144], zc(96)]
    w_qkv = jnp.concatenate(cols, axis=-1).astype(BF16)
    assert w_qkv.shape[1] == _C_END

    gq = p["gqa_q_norm"][l]
    gk = p["gqa_k_norm"][l]
    zg = jnp.zeros_like(gq)
    gqb = jnp.stack([jnp.concatenate([gq, zg] if h // 2 == 0 else [zg, gq]) for h in HEAD_PERM])
    gk2 = jnp.concatenate([gk, gk])[None]
    gmq = jnp.concatenate([p["mla_q_norm"][l], jnp.zeros((64,), F32)])[None]
    gmkv = p["mla_kv_norm"][l][None]

    wq = p["mla_w_qb"][l]
    qcols = []
    for h in range(MLA_HEADS):
        nope = wq[:, h * 96: h * 96 + 64]
        rot = wq[:, h * 96 + 64: (h + 1) * 96]
        qcols += [_slot_pad(nope, h % 2), rot, jnp.zeros((MLA_Q_LORA, 96), F32)]
    wqb = jnp.concatenate(qcols, axis=-1)
    wqb = jnp.concatenate([wqb, jnp.zeros((64, wqb.shape[1]), F32)], axis=0).astype(BF16)
    wkv = p["mla_w_kvb"][l]
    wkvb = jnp.concatenate([wkv[:, h * 128: h * 128 + 64] for h in range(MLA_HEADS)]
                           + [wkv[:, h * 128 + 64: (h + 1) * 128] for h in range(MLA_HEADS)],
                           axis=-1).astype(BF16)

    wb = p["w_branch"][l]
    perm = np.concatenate([np.arange(h * hd, (h + 1) * hd) for h in HEAD_PERM])
    w_branch = jnp.stack([wb[0], wb[1][perm], wb[2][perm], wb[3]]).astype(BF16)

    wr = jnp.concatenate([p["w_router"][l], jnp.zeros((D, LANES - N_EXPERTS), F32)], axis=-1)
    wr_hi = wr.astype(BF16)
    wr_lo = (wr - wr_hi.astype(F32)).astype(BF16)

    w = jnp.arange(GRID_W)
    col_start = jnp.clip(w - NA_KW // 2, 0, GRID_W - NA_KW)
    col_ok = (w[None, :] >= col_start[:, None]) & (w[None, :] < col_start[:, None] + NA_KW)
    dc_idx = jnp.clip(w[None, :] - w[:, None], 1 - NA_KW, NA_KW - 1) + NA_KW - 1
    t = jnp.where(col_ok[None, None], p["na_rpb"][l][:, :, dc_idx], NEG_INF)
    t = jnp.concatenate([t, jnp.full((NA_HEADS, 1, GRID_W, GRID_W), NEG_INF, F32)], axis=1)
    t = t.reshape(NA_HEADS * 16, GRID_W, GRID_W)
    zt = jnp.zeros_like(t)
    return dict(
        w_qkv=w_qkv, gqb=gqb, gk2=gk2, gmq=gmq, gmkv=gmkv, wqb=wqb, wkvb=wkvb,
        w_gates=w_in[:, 2144:].astype(BF16), w_branch=w_branch, w_out=p["w_out"][l].astype(BF16),
        ln1_g=p["ln1_g"][l][None], ln1_b=p["ln1_b"][l][None],
        ln2_g=p["ln2_g"][l][None], ln2_b=p["ln2_b"][l][None],
        wr_hi=wr_hi, wr_lo=wr_lo,
        w_gate=p["w_gate"][l].astype(BF16), w_up=p["w_up"][l].astype(BF16),
        w_down=p["w_down"][l].astype(BF16),
        tl=jnp.concatenate([t, zt], axis=-1), tr=jnp.concatenate([zt, t], axis=-1),
        sink=p["swa_sink"][l][np.array(HEAD_PERM)],
    )


def _rope_tables(n, ctx_len):
    pos = jnp.arange(n, dtype=jnp.int32)
    row = (pos // GRID_W).astype(F32)
    col = (pos % GRID_W).astype(F32)

    def axial(dim):
        half = dim // 4
        freqs = ROPE_THETA ** (-jnp.arange(half, dtype=F32) / half)
        parts_c, parts_s = [], []
        for pvec in (row, col):
            ang = pvec[:, None] * freqs[None, :]
            c, s = jnp.cos(ang), jnp.sin(ang)
            parts_c += [c, c]
            parts_s += [-s, s]
        return jnp.concatenate(parts_c, axis=-1), jnp.concatenate(parts_s, axis=-1)

    c64, s64 = axial(HEAD_DIM)
    cos = jnp.concatenate([c64, c64], axis=-1)
    sin = jnp.concatenate([s64, s64], axis=-1)
    c32, s32 = axial(MLA_ROPE)
    cosd = jnp.concatenate([c32, jnp.ones((n, LANES - MLA_ROPE), F32)], axis=-1)
    sind = jnp.concatenate([s32, jnp.zeros((n, LANES - MLA_ROPE), F32)], axis=-1)
    one = jnp.ones((ctx_len, LANES), F32)
    zero = jnp.zeros((ctx_len, LANES), F32)
    return (cos, sin, cosd, sind), (one, zero, one, zero)


def _expert_ffn(h2, aff, lw):
    B, n, D = h2.shape
    cap = CAPACITY * n // N_EXPERTS
    cap_pad = max(cap, TOKEN_BLOCK)
    slot_tile = TOKEN_BLOCK
    pos, off, tlo, thi = _select_call(aff, cap, slot_tile)
    xg, g = _gather_call(tlo, thi, pos, aff, h2, cap_pad, slot_tile)
    yw = _ffn_call(xg, g, lw)
    return _combine_call(off[..., 0].reshape(-1), pos, yw, n, D)


def kernel(x, c, ctx, c_ctx, w_mod, b_mod, w_in, na_rpb, gqa_q_norm, gqa_k_norm, swa_sink, mla_q_norm, mla_kv_norm, mla_w_qb, mla_w_kvb, w_branch, w_out, ln1_g, ln1_b, ln2_g, ln2_b, w_router, w_gate, w_up, w_down):
    p = dict(w_in=w_in, na_rpb=na_rpb, gqa_q_norm=gqa_q_norm, gqa_k_norm=gqa_k_norm, swa_sink=swa_sink,
             mla_q_norm=mla_q_norm, mla_kv_norm=mla_kv_norm, mla_w_qb=mla_w_qb, mla_w_kvb=mla_w_kvb,
             w_branch=w_branch, w_out=w_out, ln1_g=ln1_g, ln1_b=ln1_b, ln2_g=ln2_g, ln2_b=ln2_b,
             w_router=w_router, w_gate=w_gate, w_up=w_up, w_down=w_down)
    B, n, D = x.shape
    depth = w_in.shape[0]
    C = ctx.shape[1]
    alpha = (2 * depth) ** 0.25
    assert B + 1 <= 8
    cc = jnp.concatenate([c, c_ctx[None], jnp.zeros((8 - B - 1, D), F32)], axis=0)
    mod_all = _mod_call(cc, w_mod, b_mod)
    tabs, tabs_ctx = _rope_tables(n, C)
    xc = ctx
    for l in range(depth):
        lw = _prep_layer(l, p)
        need_ctx = l < depth - 1
        mods = [mod_all[l, :B, k * D:(k + 1) * D][:, None, :] for k in range(6)]
        mods_c = [jnp.broadcast_to(mod_all[l, B, k * D:(k + 1) * D][None, None, :], (B, 1, D)) for k in range(6)]
        pl_ = _inproj_call(x, mods[0], mods[1], lw, tabs)
        pc = _inproj_call(xc, mods_c[0], mods_c[1], lw, tabs_ctx)
        (naq, nak, nav, bq, bk, bv, cq, ck, cv, dq, dk, dv) = pl_
        (_, nakc, navc, _, bkc, bvc, _, ckc, cvc, _, dkc, dvc) = pc
        o_a = _na_call(naq, nak, nav, nakc, navc, lw["tl"], lw["tr"])
        o_b = _flash_call(bq, bk[:, None], bv, bkc[:, None], bvc, k_per_pair=False, v_per_pair=False)
        o_c = _win_call(lw["sink"], cq, ck, cv, ckc, cvc)
        o_d = _flash_call(dq, dk, dv, dkc, dvc, k_per_pair=True, v_per_pair=True)
        x1, h2, aff = _merge_call(x, (o_a, o_b, o_c, o_d), mods[:5], lw, alpha)
        y = _expert_ffn(h2, aff, lw)
        x = _ln2_call(x1, y, mods[5], lw["ln2_g"], lw["ln2_b"], alpha)
        if need_ctx:
            oc = _ctx_attn_call(lw["sink"], pc)
            xc1, hc2, affc = _merge_call(xc, oc, mods_c[:5], lw, alpha)
            yc = _expert_ffn(hc2, affc, lw)
            xc = _ln2_call(xc1, yc, mods_c[5], lw["ln2_g"], lw["ln2_b"], alpha)
    return x
```

```python
import functools

import numpy as np
import jax
import jax.numpy as jnp
from jax import lax
from jax.experimental import pallas as pl
from jax.experimental.pallas import tpu as pltpu

F32 = jnp.float32
BF16 = jnp.bfloat16

GRID_W = 64
HEAD_DIM = 64
ROPE_THETA = 10000.0
EPS = 1e-6
NEG_INF = -1e30
NA_HEADS = 4
NA_KH = 8
NA_KW = 16
NA_WIN_ROWS = 10
SWA_WINDOW = 128
QBLOCK = 128
MLA_HEADS = 4
MLA_NOPE = 64
MLA_ROPE = 32
MLA_V = 64
MLA_Q_LORA = 192
MLA_KV_LORA = 128
N_BRANCH = 4
BRANCH_W = 256
N_EXPERTS = 16
CAPACITY = 2
LANES = 128
TOKEN_BLOCK = 128
VMEM_LIMIT = 56 * 1024 * 1024
HEAD_PERM = (0, 2, 1, 3)
VT_ROWS = 80


def _cparams(*sem):
    return pltpu.CompilerParams(dimension_semantics=sem, vmem_limit_bytes=VMEM_LIMIT)


def _dot(a, b):
    return jnp.dot(a, b, preferred_element_type=F32)


def _dot_nt(a, b):
    return lax.dot_general(a, b, (((1,), (1,)), ((), ())), preferred_element_type=F32)


def _ln(x):
    mu = jnp.mean(x, axis=-1, keepdims=True)
    xc = x - mu
    var = jnp.mean(xc * xc, axis=-1, keepdims=True)
    return xc * lax.rsqrt(var + EPS)


def _modulate(x, shift, scale):
    return _ln(x) * (1.0 + scale) + shift


def _lane(shape, dim=None):
    return lax.broadcasted_iota(jnp.int32, shape, len(shape) - 1 if dim is None else dim)


def _rope(y, cos, sin_signed, half):
    w = y.shape[-1]
    low = (_lane((1, w)) & (2 * half - 1)) < half
    partner = jnp.where(low, pltpu.roll(y, w - half, 1), pltpu.roll(y, half, 1))
    return y * cos + partner * sin_signed


def _mod_kernel(c_ref, w_ref, b_ref, o_ref):
    c = c_ref[...]
    s = c / (1.0 + jnp.exp(-c))
    o_ref[0] = _dot(s.astype(BF16), w_ref[0].astype(BF16)) + b_ref[0]


def _mod_call(cc, w_mod, b_mod):
    L, D, N = w_mod.shape
    tn = N // 4
    return pl.pallas_call(
        _mod_kernel,
        grid=(L, N // tn),
        in_specs=[pl.BlockSpec((8, D), lambda l, j: (0, 0)),
                  pl.BlockSpec((1, D, tn), lambda l, j: (l, 0, j)),
                  pl.BlockSpec((1, 1, tn), lambda l, j: (l, 0, j))],
        out_specs=pl.BlockSpec((1, 8, tn), lambda l, j: (l, 0, j)),
        out_shape=jax.ShapeDtypeStruct((L, 8, N), F32),
        compiler_params=_cparams("arbitrary", "arbitrary"),
        name="mod_vectors",
    )(cc, w_mod, b_mod.reshape(L, 1, N))


_C_NA = 0
_C_BQ = 768
_C_BK = 1280
_C_BV = 1408
_C_CQ = 1536
_C_CK = 2048
_C_CV = 2176
_C_DQ = 2304
_C_DKV = 2560
_C_DKR = 2688
_C_END = 2816


def _inproj_kernel(x_ref, sh_ref, sc_ref, w_ref, cos_ref, sin_ref, cosd_ref, sind_ref,
                   gqb_ref, gk2_ref, gmq_ref, gmkv_ref, wqb_ref, wkvb_ref,
                   naq, nak, nav, bq, bk, bv, cq, ck, cv, dq, dk, dv):
    hb = _modulate(x_ref[0], sh_ref[0], sc_ref[0]).astype(BF16)

    def seg(a, b):
        return _dot(hb, w_ref[:, a:b])

    cos = cos_ref[...]
    sin = sin_ref[...]
    cosd = cosd_ref[...]
    sind = sind_ref[...]
    qscale = HEAD_DIM ** -0.5
    log2e = 1.4426950408889634

    def put_values_t(ref, pair, v):
        vt = v.T
        ones = jnp.ones((VT_ROWS - HEAD_DIM, vt.shape[1]), F32)
        for s in range(2):
            ref[0, pair, 0, s] = jnp.concatenate([vt[s * HEAD_DIM:(s + 1) * HEAD_DIM], ones], axis=0).astype(BF16)

    z = seg(_C_NA, _C_NA + 768)
    naq[0] = (z[:, 0:256] * qscale).astype(BF16)
    nak[0] = z[:, 256:512].astype(BF16)
    nav[0] = z[:, 512:768].astype(BF16)

    z = seg(_C_BQ, _C_BK)
    for s in range(4):
        zs = z[:, s * LANES:(s + 1) * LANES]
        ss = jnp.sum(zs * zs, axis=-1, keepdims=True) * (1.0 / HEAD_DIM)
        y = zs * lax.rsqrt(ss + EPS) * gqb_ref[s:s + 1, :]
        bq[0, s] = (_rope(y, cos, sin, 16) * (qscale * log2e)).astype(BF16)
    z = seg(_C_BK, _C_BV)
    left = _lane((1, LANES)) < HEAD_DIM
    z2 = z * z
    ss0 = jnp.sum(jnp.where(left, z2, 0.0), axis=-1, keepdims=True)
    ss1 = jnp.sum(jnp.where(left, 0.0, z2), axis=-1, keepdims=True)
    inv = jnp.where(left, lax.rsqrt(ss0 * (1.0 / HEAD_DIM) + EPS), lax.rsqrt(ss1 * (1.0 / HEAD_DIM) + EPS))
    bk[0] = _rope(z * inv * gk2_ref[...], cos, sin, 16).astype(BF16)
    put_values_t(bv, 0, seg(_C_BV, _C_CQ))

    z = seg(_C_CQ, _C_CK)
    for s in range(4):
        cq[0, s] = (_rope(z[:, s * LANES:(s + 1) * LANES], cos, sin, 16) * qscale).astype(BF16)
    ck[0] = _rope(seg(_C_CK, _C_CV), cos, sin, 16).astype(BF16)
    cv[0] = seg(_C_CV, _C_DQ).astype(BF16)

    z = seg(_C_DQ, _C_DKV)
    ss = jnp.sum(z * z, axis=-1, keepdims=True) * (1.0 / MLA_Q_LORA)
    cqn = (z * lax.rsqrt(ss + EPS) * gmq_ref[...]).astype(BF16)
    qd = _dot(cqn, wqb_ref[...])
    dscale = (MLA_NOPE + MLA_ROPE) ** -0.5 * log2e
    for h in range(MLA_HEADS):
        nope = qd[:, h * 256:h * 256 + LANES] * dscale
        rot = _rope(qd[:, h * 256 + LANES:(h + 1) * 256], cosd, sind, 8) * dscale
        dq[0, h] = jnp.concatenate([nope, rot], axis=-1).astype(BF16)
    z = seg(_C_DKV, _C_DKR)
    ss = jnp.sum(z * z, axis=-1, keepdims=True) * (1.0 / MLA_KV_LORA)
    ckvn = (z * lax.rsqrt(ss + EPS) * gmkv_ref[...]).astype(BF16)
    kv = _dot(ckvn, wkvb_ref[...])
    kr = _rope(seg(_C_DKR, _C_END), cosd, sind, 8)
    for p in range(2):
        dk[0, p] = jnp.concatenate([kv[:, p * LANES:(p + 1) * LANES], kr], axis=-1).astype(BF16)
        put_values_t(dv, p, kv[:, 256 + p * LANES:256 + (p + 1) * LANES])


def _inproj_call(x, shift, scale, lw, tabs):
    B, n, D = x.shape
    tm = min(512, n)
    cos, sin, cosd, sind = tabs
    row = lambda b, i: (b, i, 0)
    hrow = lambda b, i: (b, 0, i, 0)
    const2 = lambda b, i: (0, 0)
    tab = pl.BlockSpec((tm, LANES), lambda b, i: (i, 0))
    mod = pl.BlockSpec((1, 1, D), lambda b, i: (b, 0, 0))
    nt = n // tm
    vt_spec = lambda pairs: ((B, pairs, nt, 2, VT_ROWS, tm),
                             pl.BlockSpec((1, pairs, 1, 2, VT_ROWS, tm), lambda b, i: (b, 0, i, 0, 0, 0)))
    slab = lambda kd: ((B, 4, n, kd), pl.BlockSpec((1, 4, tm, kd), hrow))
    tokm = lambda w: ((B, n, w), pl.BlockSpec((1, tm, w), row))
    outs = [tokm(256)] * 3
    outs += [slab(LANES), tokm(LANES), vt_spec(1)]
    outs += [slab(LANES), tokm(LANES), tokm(LANES)]
    outs += [slab(256), ((B, 2, n, 256), pl.BlockSpec((1, 2, tm, 256), hrow)), vt_spec(2)]
    return pl.pallas_call(
        _inproj_kernel,
        grid=(B, n // tm),
        in_specs=[pl.BlockSpec((1, tm, D), row), mod, mod,
                  pl.BlockSpec((D, _C_END), const2), tab, tab, tab, tab,
                  pl.BlockSpec((4, LANES), const2), pl.BlockSpec((1, LANES), const2),
                  pl.BlockSpec((1, 256), const2), pl.BlockSpec((1, LANES), const2),
                  pl.BlockSpec((256, 1024), const2), pl.BlockSpec((LANES, 512), const2)],
        out_specs=[o[1] for o in outs],
        out_shape=[jax.ShapeDtypeStruct(o[0], BF16) for o in outs],
        compiler_params=_cparams("arbitrary", "arbitrary"),
        name="in_proj",
    )(x, shift, scale, lw["w_qkv"], cos, sin, cosd, sind,
      lw["gqb"], lw["gk2"], lw["gmq"], lw["gmkv"], lw["wqb"], lw["wkvb"])


def _flash_kernel(q_ref, k_ref, vt_ref, kc_ref, vct_ref, o_ref, acc_ref, m_ref, *, tq, nk, unroll):
    q = q_ref[0].reshape(2 * tq, q_ref.shape[-1])
    m_ref[...] = jnp.full(m_ref.shape, NEG_INF, F32)
    acc_ref[...] = jnp.zeros(acc_ref.shape, F32)

    def update(s, vt):
        m_prev = m_ref[...]
        m_new = jnp.maximum(m_prev, jnp.max(s, axis=0, keepdims=True))
        a = jnp.exp2(m_prev - m_new)
        p = jnp.exp2(s - m_new).astype(BF16)
        for h in range(2):
            cols = slice(h * tq, (h + 1) * tq)
            acc_ref[h] = a[:, cols] * acc_ref[h] + _dot(vt[h], p[:, cols])
        m_ref[...] = m_new

    def body(j, carry):
        scores = [_dot_nt(k_ref[0, 0, j * unroll + u], q) for u in range(unroll)]
        for u in range(unroll):
            update(scores[u], vt_ref[0, 0, j * unroll + u])
        return carry

    lax.fori_loop(0, nk // unroll, body, 0)
    update(_dot_nt(kc_ref[0, 0, 0], q), vct_ref[0, 0, 0])
    o = [acc_ref[h][:HEAD_DIM] / acc_ref[h][HEAD_DIM:HEAD_DIM + 1] for h in range(2)]
    o_ref[0] = jnp.concatenate(o, axis=0).T.astype(BF16)


def _flash_call(q, k, vt, kc, vct, *, per_pair):
    B, _, n, kd = q.shape
    C = kc.shape[2]
    nk, tk = vt.shape[2], vt.shape[5]
    tq = min(256, n)
    pidx = (lambda b, p, i: (b, p, 0, 0, 0)) if per_pair else (lambda b, p, i: (b, 0, 0, 0, 0))
    vidx = (lambda b, p, i: (b, p, 0, 0, 0, 0)) if per_pair else (lambda b, p, i: (b, 0, 0, 0, 0, 0))
    kern = functools.partial(_flash_kernel, tq=tq, nk=nk, unroll=4 if nk % 4 == 0 else (2 if nk % 2 == 0 else 1))
    return pl.pallas_call(
        kern,
        grid=(B, 2, n // tq),
        in_specs=[pl.BlockSpec((1, 2, tq, kd), lambda b, p, i: (b, p, i, 0)),
                  pl.BlockSpec((1, 1, nk, tk, kd), pidx),
                  pl.BlockSpec((1, 1, nk, 2, VT_ROWS, tk), vidx),
                  pl.BlockSpec((1, 1, 1, C, kd), pidx),
                  pl.BlockSpec((1, 1, 1, 2, VT_ROWS, C), vidx)],
        out_specs=pl.BlockSpec((1, tq, LANES), lambda b, p, i: (b, i, p)),
        out_shape=jax.ShapeDtypeStruct((B, n, 256), BF16),
        scratch_shapes=[pltpu.VMEM((2, VT_ROWS, tq), F32), pltpu.VMEM((1, 2 * tq), F32)],
        compiler_params=_cparams("arbitrary", "arbitrary", "arbitrary"),
        name="dense_attn",
    )(q, k.reshape(k.shape[0], k.shape[1], nk, tk, kd), vt, kc[:, :, None], vct)


def _na_kernel(q_ref, k_ref, v_ref, kc_ref, vc_ref, tl_ref, tr_ref, o_ref, *, rows):
    i = pl.program_id(1)
    r0 = 2 * i
    w0 = jnp.clip(r0 - NA_KH // 2, 0, rows - NA_WIN_ROWS)
    start = pl.multiple_of(w0 * GRID_W, GRID_W)
    kw = k_ref[0, pl.ds(start, NA_WIN_ROWS * GRID_W), :]
    vw = v_ref[0, pl.ds(start, NA_WIN_ROWS * GRID_W), :]
    kc = kc_ref[0]
    vc = vc_ref[0]
    q = q_ref[0]
    head_of_lane = _lane((1, 256)) >> 6

    def table_index(a, j):
        qr = r0 + a
        kr = w0 + j
        st = jnp.clip(qr - NA_KH // 2, 0, rows - NA_KH)
        ok = (kr >= st) & (kr < st + NA_KH)
        return jnp.where(ok, kr - qr + NA_KH - 1, 2 * NA_KH - 1)

    out = jnp.zeros((2 * GRID_W, 256), F32)
    for h in range(NA_HEADS):
        qm = jnp.where(head_of_lane == h, q, jnp.zeros_like(q))
        bias_rows = []
        for a in range(2):
            blocks = [tl_ref[h * 16 + table_index(a, 2 * jp)] + tr_ref[h * 16 + table_index(a, 2 * jp + 1)]
                      for jp in range(NA_WIN_ROWS // 2)]
            bias_rows.append(jnp.concatenate(blocks, axis=1))
        s_lat = _dot_nt(qm, kw) + jnp.concatenate(bias_rows, axis=0)
        s_ctx = _dot_nt(qm, kc)
        m = jnp.maximum(jnp.max(s_lat, axis=-1, keepdims=True), jnp.max(s_ctx, axis=-1, keepdims=True))
        p_lat = jnp.exp(s_lat - m)
        p_ctx = jnp.exp(s_ctx - m)
        l = jnp.sum(p_lat, axis=-1, keepdims=True) + jnp.sum(p_ctx, axis=-1, keepdims=True)
        pv = _dot(p_lat.astype(BF16), vw) + _dot(p_ctx.astype(BF16), vc)
        out = jnp.where(head_of_lane == h, pv / l, out)
    o_ref[0] = out.astype(BF16)


def _na_call(q, k, v, kc, vc, tl, tr):
    B, n, _ = q.shape
    C = kc.shape[1]
    rows = n // GRID_W
    assert rows >= NA_WIN_ROWS and rows % 2 == 0
    full = lambda b, i: (b, 0, 0)
    return pl.pallas_call(
        functools.partial(_na_kernel, rows=rows),
        grid=(B, rows // 2),
        in_specs=[pl.BlockSpec((1, 2 * GRID_W, 256), lambda b, i: (b, i, 0)),
                  pl.BlockSpec((1, n, 256), full), pl.BlockSpec((1, n, 256), full),
                  pl.BlockSpec((1, C, 256), full), pl.BlockSpec((1, C, 256), full),
                  pl.BlockSpec(tl.shape, lambda b, i: (0, 0, 0)),
                  pl.BlockSpec(tr.shape, lambda b, i: (0, 0, 0))],
        out_specs=pl.BlockSpec((1, 2 * GRID_W, 256), lambda b, i: (b, i, 0)),
        out_shape=jax.ShapeDtypeStruct((B, n, 256), BF16),
        compiler_params=_cparams("arbitrary", "arbitrary"),
        name="nbr_attn",
    )(q, k, v, kc, vc, tl, tr)


def _win_kernel(sink_ref, q_ref, k_ref, v_ref, kc_ref, vc_ref, o_ref, *, n):
    i = pl.program_id(1)
    band = 3 * QBLOCK
    ws = pl.multiple_of(jnp.clip((i - 1) * QBLOCK, 0, n - band), QBLOCK)
    kw = k_ref[0, pl.ds(ws, band), :]
    vw = v_ref[0, pl.ds(ws, band), :]
    kc = kc_ref[0]
    vc = vc_ref[0]
    qpos = i * QBLOCK + lax.broadcasted_iota(jnp.int32, (QBLOCK, band), 0)
    kpos = ws + lax.broadcasted_iota(jnp.int32, (QBLOCK, band), 1)
    ok = jnp.abs(qpos - kpos) <= SWA_WINDOW
    outs = []
    for s in range(4):
        q = q_ref[0, s]
        s_lat = jnp.where(ok, _dot_nt(q, kw), NEG_INF)
        s_ctx = _dot_nt(q, kc)
        snk = sink_ref[s]
        m = jnp.maximum(jnp.maximum(jnp.max(s_lat, axis=-1, keepdims=True),
                                    jnp.max(s_ctx, axis=-1, keepdims=True)), snk)
        p_lat = jnp.exp(s_lat - m)
        p_ctx = jnp.exp(s_ctx - m)
        l = (jnp.sum(p_lat, axis=-1, keepdims=True) + jnp.sum(p_ctx, axis=-1, keepdims=True)
             + jnp.exp(snk - m))
        outs.append((_dot(p_lat.astype(BF16), vw) + _dot(p_ctx.astype(BF16), vc)) / l)
    left = _lane((1, LANES)) < HEAD_DIM
    o_ref[0] = jnp.concatenate([jnp.where(left, outs[0], outs[1]),
                                jnp.where(left, outs[2], outs[3])], axis=-1).astype(BF16)


def _win_call(sink, q, k, v, kc, vc):
    B, _, n, _ = q.shape
    C = kc.shape[1]
    assert n >= 3 * QBLOCK
    full = lambda b, i, s: (b, 0, 0)
    return pl.pallas_call(
        functools.partial(_win_kernel, n=n),
        grid_spec=pltpu.PrefetchScalarGridSpec(
            num_scalar_prefetch=1,
            grid=(B, n // QBLOCK),
            in_specs=[pl.BlockSpec((1, 4, QBLOCK, LANES), lambda b, i, s: (b, 0, i, 0)),
                      pl.BlockSpec((1, n, LANES), full), pl.BlockSpec((1, n, LANES), full),
                      pl.BlockSpec((1, C, LANES), full), pl.BlockSpec((1, C, LANES), full)],
            out_specs=pl.BlockSpec((1, QBLOCK, 256), lambda b, i, s: (b, i, 0))),
        out_shape=jax.ShapeDtypeStruct((B, n, 256), BF16),
        compiler_params=_cparams("arbitrary", "arbitrary"),
        name="window_attn",
    )(sink, q, k, v, kc, vc)


def _attend(q, k, v, sink=None, v_transposed=False):
    s = _dot_nt(q, k)
    m = jnp.max(s, axis=-1, keepdims=True)
    if sink is not None:
        m = jnp.maximum(m, sink)
    p = jnp.exp2(s - m) if v_transposed else jnp.exp(s - m)
    l = jnp.sum(p, axis=-1, keepdims=True)
    if sink is not None:
        l = l + jnp.exp(sink - m)
    pv = _dot_nt(p.astype(BF16), v) if v_transposed else _dot(p.astype(BF16), v)
    return pv / l


def _ctx_attn_kernel(sink_ref, naq, nak, nav, bq, bk, bv, cq, ck, cv, dq, dk, dv, oa, ob, oc, od):
    left = _lane((1, LANES)) < HEAD_DIM
    head_of_lane = _lane((1, 256)) >> 6
    q = naq[0]
    out = jnp.zeros(q.shape, F32)
    for h in range(NA_HEADS):
        qm = jnp.where(head_of_lane == h, q, jnp.zeros_like(q))
        out = jnp.where(head_of_lane == h, _attend(qm, nak[0], nav[0]), out)
    oa[0] = out.astype(BF16)
    def both_heads(vt):
        return jnp.concatenate([vt[0, :HEAD_DIM], vt[1, :HEAD_DIM]], axis=0)

    o = [_attend(bq[0, s], bk[0], both_heads(bv[0, 0, 0]), v_transposed=True) for s in range(4)]
    ob[0] = jnp.concatenate([jnp.where(left, o[0], o[1]), jnp.where(left, o[2], o[3])], axis=-1).astype(BF16)
    o = [_attend(cq[0, s], ck[0], cv[0], sink_ref[s]) for s in range(4)]
    oc[0] = jnp.concatenate([jnp.where(left, o[0], o[1]), jnp.where(left, o[2], o[3])], axis=-1).astype(BF16)
    o = [_attend(dq[0, h], dk[0, h // 2], both_heads(dv[0, h // 2, 0]), v_transposed=True)
         for h in range(MLA_HEADS)]
    od[0] = jnp.concatenate([jnp.where(left, o[0], o[1]), jnp.where(left, o[2], o[3])], axis=-1).astype(BF16)


def _ctx_attn_call(sink, pc):
    B, C, _ = pc[0].shape
    names = pc
    specs = []
    for a in names:
        nd = a.ndim
        specs.append(pl.BlockSpec((1,) + a.shape[1:], (lambda b, s, nd=nd: (b,) + (0,) * (nd - 1))))
    out_spec = pl.BlockSpec((1, C, 256), lambda b, s: (b, 0, 0))
    return pl.pallas_call(
        _ctx_attn_kernel,
        grid_spec=pltpu.PrefetchScalarGridSpec(
            num_scalar_prefetch=1, grid=(B,), in_specs=specs, out_specs=[out_spec] * 4),
        out_shape=[jax.ShapeDtypeStruct((B, C, 256), BF16)] * 4,
        compiler_params=_cparams("arbitrary"),
        name="ctx_attn",
    )(sink, *pc)


def _merge_kernel(x_ref, oa, ob, oc, od, sha, sca, ga, shf, scf, wg_ref, wb_ref, wo_ref,
                  l1g, l1b, wrh_ref, wrl_ref, x1_ref, h2_ref, aff_ref, *, alpha):
    x = x_ref[0]
    D = x.shape[-1]
    hb = _modulate(x, sha[0], sca[0]).astype(BF16)
    merged = None
    for i, o in enumerate((oa, ob, oc, od)):
        g = 1.0 / (1.0 + jnp.exp(-_dot(hb, wg_ref[:, i * D:(i + 1) * D])))
        term = g * _dot(o[0], wb_ref[i])
        merged = term if merged is None else merged + term
    y = _dot(merged.astype(BF16), wo_ref[...])
    x1 = _ln(alpha * x + ga[0] * y) * l1g[...] + l1b[...]
    x1_ref[0] = x1
    h2 = _modulate(x1, shf[0], scf[0])
    h2_hi = h2.astype(BF16)
    h2_ref[0] = h2_hi
    h2_lo = (h2 - h2_hi.astype(F32)).astype(BF16)
    logits = _dot(h2_hi, wrh_ref[...]) + _dot(h2_hi, wrl_ref[...]) + _dot(h2_lo, wrh_ref[...])
    logits = jnp.where(_lane((1, LANES)) < N_EXPERTS, logits, NEG_INF)
    e = jnp.exp(logits - jnp.max(logits, axis=-1, keepdims=True))
    aff_t = (e / jnp.sum(e, axis=-1, keepdims=True)).T
    for k in range(aff_ref.shape[1]):
        aff_ref[0, k] = aff_t[:N_EXPERTS, k * TOKEN_BLOCK:(k + 1) * TOKEN_BLOCK]


def _merge_call(x, outs, mods, lw, alpha):
    B, n, D = x.shape
    tm = min(256, n)
    nb = tm // TOKEN_BLOCK
    row = lambda b, i: (b, i, 0)
    mod = pl.BlockSpec((1, 1, D), lambda b, i: (b, 0, 0))
    c2 = lambda b, i: (0, 0)
    obr = pl.BlockSpec((1, tm, 256), row)
    return pl.pallas_call(
        functools.partial(_merge_kernel, alpha=alpha),
        grid=(B, n // tm),
        in_specs=[pl.BlockSpec((1, tm, D), row), obr, obr, obr, obr, mod, mod, mod, mod, mod,
                  pl.BlockSpec((D, N_BRANCH * D), c2),
                  pl.BlockSpec((N_BRANCH, BRANCH_W, D), lambda b, i: (0, 0, 0)),
                  pl.BlockSpec((D, D), c2), pl.BlockSpec((1, D), c2), pl.BlockSpec((1, D), c2),
                  pl.BlockSpec((D, LANES), c2), pl.BlockSpec((D, LANES), c2)],
        out_specs=[pl.BlockSpec((1, tm, D), row), pl.BlockSpec((1, tm, D), row),
                   pl.BlockSpec((1, nb, N_EXPERTS, TOKEN_BLOCK), lambda b, i: (b, i, 0, 0))],
        out_shape=[jax.ShapeDtypeStruct((B, n, D), F32), jax.ShapeDtypeStruct((B, n, D), BF16),
                   jax.ShapeDtypeStruct((B, n // TOKEN_BLOCK, N_EXPERTS, TOKEN_BLOCK), F32)],
        compiler_params=_cparams("arbitrary", "arbitrary"),
        name="merge_router",
    )(x, *outs, *mods, lw["w_gates"], lw["w_branch"], lw["w_out"], lw["ln1_g"], lw["ln1_b"],
      lw["wr_hi"], lw["wr_lo"])


def _select_kernel(aff_ref, pos_ref, off_ref, tlo_ref, thi_ref, *, cap, slot_tile):
    a = aff_ref[0]
    nb = a.shape[0]
    rows = nb * N_EXPERTS
    bits = lax.bitcast_convert_type(a, jnp.int32)
    capf = jnp.float32(cap)

    def count(mask):
        c = jnp.sum(jnp.where(mask, 1.0, 0.0), axis=0)
        return jnp.broadcast_to(jnp.sum(c, axis=-1, keepdims=True), c.shape)

    def search(it, lo):
        cand = lo | lax.shift_left(jnp.int32(1), 30 - it)
        return jnp.where(count(bits >= cand[None]) >= capf, cand, lo)

    thr = lax.fori_loop(0, 31, search, jnp.zeros((N_EXPERTS, TOKEN_BLOCK), jnp.int32))

    r = lax.broadcasted_iota(jnp.int32, (rows, rows), 0)
    c = lax.broadcasted_iota(jnp.int32, (rows, rows), 1)
    earlier = jnp.where(((r & (N_EXPERTS - 1)) == (c & (N_EXPERTS - 1))) & ((c >> 4) < (r >> 4)), 1.0, 0.0).astype(BF16)
    ti = lax.broadcasted_iota(jnp.int32, (TOKEN_BLOCK, TOKEN_BLOCK), 0)
    tj = lax.broadcasted_iota(jnp.int32, (TOKEN_BLOCK, TOKEN_BLOCK), 1)
    tri = jnp.where(ti <= tj, 1.0, 0.0).astype(BF16)
    ones = jnp.ones((TOKEN_BLOCK, TOKEN_BLOCK), BF16)

    def prefix(mask):
        m2 = jnp.where(mask, 1.0, 0.0).reshape(rows, TOKEN_BLOCK)
        mb = m2.astype(BF16)
        within = _dot(mb, tri)
        tot = _dot(mb, ones)
        off = _dot(earlier, tot.astype(BF16))
        shp = (nb, N_EXPERTS, TOKEN_BLOCK)
        return (off + within - m2).reshape(shp), off.reshape(shp), tot.reshape(shp)

    gt = bits > thr[None]
    eq = bits == thr[None]
    need = capf - count(gt)
    eq_rank, _, _ = prefix(eq)
    sel = gt | (eq & (eq_rank < need[None]))
    excl, off, tot = prefix(sel)
    pos_ref[0] = jnp.where(sel, excl, -1.0)
    off_ref[0] = off.astype(jnp.int32)
    tile_start = (_lane((1, 1, TOKEN_BLOCK)) * slot_tile).astype(F32)
    tlo_ref[0] = jnp.sum(jnp.where(off + tot <= tile_start, 1, 0), axis=0).astype(jnp.int32)
    thi_ref[0] = jnp.sum(jnp.where(off < tile_start + slot_tile, 1, 0), axis=0).astype(jnp.int32)


def _select_call(aff, cap, slot_tile):
    B, nb, E, _ = aff.shape
    blk = pl.BlockSpec((1, nb, E, TOKEN_BLOCK), lambda b: (b, 0, 0, 0))
    rng = pl.BlockSpec((1, E, TOKEN_BLOCK), lambda b: (b, 0, 0))
    return pl.pallas_call(
        functools.partial(_select_kernel, cap=cap, slot_tile=slot_tile),
        grid=(B,),
        in_specs=[blk],
        out_specs=[blk, blk, rng, rng],
        out_shape=[jax.ShapeDtypeStruct(aff.shape, F32), jax.ShapeDtypeStruct(aff.shape, jnp.int32),
                   jax.ShapeDtypeStruct((B, E, TOKEN_BLOCK), jnp.int32),
                   jax.ShapeDtypeStruct((B, E, TOKEN_BLOCK), jnp.int32)],
        compiler_params=_cparams("arbitrary"),
        name="expert_select",
    )(aff)


def _gather_kernel(tlo_ref, thi_ref, pos_ref, aff_ref, h_ref, xg_ref, g_ref, acc_ref, gacc_ref, *, slot_tile):
    b = pl.program_id(0)
    e = pl.program_id(1)
    n_tiles = xg_ref.shape[2] // slot_tile
    for t in range(n_tiles):
        acc_ref[...] = jnp.zeros(acc_ref.shape, F32)
        gacc_ref[...] = jnp.zeros(gacc_ref.shape, F32)
        slot = (t * slot_tile + lax.broadcasted_iota(jnp.int32, (slot_tile, TOKEN_BLOCK), 0)).astype(F32)

        def body(blk, carry):
            p = pos_ref[0, blk, pl.ds(e, 1), :]
            a = aff_ref[0, blk, pl.ds(e, 1), :]
            hit = p == slot
            tok = pl.multiple_of(blk * TOKEN_BLOCK, TOKEN_BLOCK)
            acc_ref[...] += _dot(jnp.where(hit, 1.0, 0.0).astype(BF16), h_ref[0, pl.ds(tok, TOKEN_BLOCK), :])
            gacc_ref[...] += jnp.where(hit, a, 0.0)
            return carry

        lax.fori_loop(tlo_ref[b, e, t], thi_ref[b, e, t], body, 0)
        xg_ref[0, 0, t * slot_tile:(t + 1) * slot_tile, :] = acc_ref[...].astype(BF16)
        g_ref[0, 0, t * slot_tile:(t + 1) * slot_tile, :] = jnp.sum(gacc_ref[...], axis=-1, keepdims=True)


def _gather_call(tlo, thi, pos, aff, h2, cap_pad, slot_tile):
    B, n, D = h2.shape
    nb = n // TOKEN_BLOCK
    blk = pl.BlockSpec((1, nb, N_EXPERTS, TOKEN_BLOCK), lambda b, e, *_: (b, 0, 0, 0))
    return pl.pallas_call(
        functools.partial(_gather_kernel, slot_tile=slot_tile),
        grid_spec=pltpu.PrefetchScalarGridSpec(
            num_scalar_prefetch=2,
            grid=(B, N_EXPERTS),
            in_specs=[blk, blk, pl.BlockSpec((1, n, D), lambda b, e, *_: (b, 0, 0))],
            out_specs=[pl.BlockSpec((1, 1, cap_pad, D), lambda b, e, *_: (b, e, 0, 0)),
                       pl.BlockSpec((1, 1, cap_pad, 1), lambda b, e, *_: (b, e, 0, 0))],
            scratch_shapes=[pltpu.VMEM((slot_tile, D), F32), pltpu.VMEM((slot_tile, TOKEN_BLOCK), F32)]),
        out_shape=[jax.ShapeDtypeStruct((B, N_EXPERTS, cap_pad, D), BF16),
                   jax.ShapeDtypeStruct((B, N_EXPERTS, cap_pad, 1), F32)],
        compiler_params=_cparams("arbitrary", "arbitrary"),
        name="expert_gather",
    )(tlo, thi, pos, aff, h2)


def _ffn_kernel(xg_ref, g_ref, wg_ref, wu_ref, wd_ref, y_ref):
    xg = xg_ref[0, 0]
    a = _dot(xg, wg_ref[0])
    u = _dot(xg, wu_ref[0])
    hmid = (a / (1.0 + jnp.exp(-a)) * u).astype(BF16)
    y_ref[0, 0] = (_dot(hmid, wd_ref[0]) * g_ref[0, 0]).astype(BF16)


def _ffn_call(xg, g, lw):
    B, E, cp, D = xg.shape
    F = lw["w_gate"].shape[-1]
    tok = lambda e, b: (b, e, 0, 0)
    wsp = lambda e, b: (e, 0, 0)
    return pl.pallas_call(
        _ffn_kernel,
        grid=(E, B),
        in_specs=[pl.BlockSpec((1, 1, cp, D), tok), pl.BlockSpec((1, 1, cp, 1), tok),
                  pl.BlockSpec((1, D, F), wsp), pl.BlockSpec((1, D, F), wsp), pl.BlockSpec((1, F, D), wsp)],
        out_specs=pl.BlockSpec((1, 1, cp, D), tok),
        out_shape=jax.ShapeDtypeStruct((B, E, cp, D), BF16),
        compiler_params=_cparams("arbitrary", "arbitrary"),
        name="expert_mlp",
    )(xg, g, lw["w_gate"], lw["w_up"], lw["w_down"])


def _combine_kernel(off_ref, pos_ref, yw_ref, y_ref, *, window, blocks_per_step):
    b = pl.program_id(0)
    t = pl.program_id(2)
    cap_pad = yw_ref.shape[2]
    nb_total = pl.num_programs(2) * blocks_per_step
    for k in range(blocks_per_step):
        blk = t * blocks_per_step + k
        acc = jnp.zeros((TOKEN_BLOCK, yw_ref.shape[-1]), F32)
        for e in range(N_EXPERTS):
            start = off_ref[(b * nb_total + blk) * N_EXPERTS + e]
            w0 = pl.multiple_of(jnp.minimum(start & -16, cap_pad - window), 16)
            p = pos_ref[0, k, e:e + 1, :]
            slot = (w0 + lax.broadcasted_iota(jnp.int32, (window, TOKEN_BLOCK), 0)).astype(F32)
            hit = jnp.where(p == slot, 1.0, 0.0).T.astype(BF16)
            acc = acc + _dot(hit, yw_ref[0, e, pl.ds(w0, window), :])
        y_ref[0, k * TOKEN_BLOCK:(k + 1) * TOKEN_BLOCK, :] = acc


def _combine_call(off_flat, pos, yw, n, D):
    B, E, cap_pad, _ = yw.shape
    nb = n // TOKEN_BLOCK
    window = min(256, cap_pad)
    dh = min(512, D)
    bps = min(4, nb)
    return pl.pallas_call(
        functools.partial(_combine_kernel, window=window, blocks_per_step=bps),
        grid_spec=pltpu.PrefetchScalarGridSpec(
            num_scalar_prefetch=1,
            grid=(B, D // dh, nb // bps),
            in_specs=[pl.BlockSpec((1, bps, E, TOKEN_BLOCK), lambda b, d, t, *_: (b, t, 0, 0)),
                      pl.BlockSpec((1, E, cap_pad, dh), lambda b, d, t, *_: (b, 0, 0, d))],
            out_specs=pl.BlockSpec((1, bps * TOKEN_BLOCK, dh), lambda b, d, t, *_: (b, t, d))),
        out_shape=jax.ShapeDtypeStruct((B, n, D), F32),
        compiler_params=_cparams("arbitrary", "arbitrary", "arbitrary"),
        name="expert_combine",
    )(off_flat, pos, yw)


def _ln2_kernel(x_ref, y_ref, g_ref, lg, lb, o_ref, *, alpha):
    o_ref[0] = _ln(alpha * x_ref[0] + g_ref[0] * y_ref[0]) * lg[...] + lb[...]


def _ln2_call(x1, y, gf, lg, lb, alpha):
    B, n, D = x1.shape
    tm = min(512, n)
    row = pl.BlockSpec((1, tm, D), lambda b, i: (b, i, 0))
    vec = pl.BlockSpec((1, D), lambda b, i: (0, 0))
    return pl.pallas_call(
        functools.partial(_ln2_kernel, alpha=alpha),
        grid=(B, n // tm),
        in_specs=[row, row, pl.BlockSpec((1, 1, D), lambda b, i: (b, 0, 0)), vec, vec],
        out_specs=row,
        out_shape=jax.ShapeDtypeStruct((B, n, D), F32),
        compiler_params=_cparams("arbitrary", "arbitrary"),
        name="ffn_residual_ln",
    )(x1, y, gf, lg, lb)


def _slot_pad(w, slot):
    z = jnp.zeros_like(w)
    return jnp.concatenate([w, z] if slot == 0 else [z, w], axis=-1)


def _prep_layer(l, p):
    w_in = p["w_in"][l]
    D = w_in.shape[0]
    hd = HEAD_DIM

    def gq_cols(base):
        q = [_slot_pad(w_in[:, base + h * hd: base + (h + 1) * hd], h // 2) for h in HEAD_PERM]
        return q + [w_in[:, base + 256: base + 512]]

    zc = lambda k: jnp.zeros((D, k), F32)
    cols = [w_in[:, 0:768]] + gq_cols(768) + gq_cols(1280)
    cols += [w_in[:, 1792:1984], zc(64), w_in[:, 1984:2112], w_in[:, 2112:2144], zc(96)]
    w_qkv = jnp.concatenate(cols, axis=-1).astype(BF16)
    assert w_qkv.shape[1] == _C_END

    gq = p["gqa_q_norm"][l]
    gk = p["gqa_k_norm"][l]
    zg = jnp.zeros_like(gq)
    gqb = jnp.stack([jnp.concatenate([gq, zg] if h // 2 == 0 else [zg, gq]) for h in HEAD_PERM])
    gk2 = jnp.concatenate([gk, gk])[None]
    gmq = jnp.concatenate([p["mla_q_norm"][l], jnp.zeros((64,), F32)])[None]
    gmkv = p["mla_kv_norm"][l][None]

    wq = p["mla_w_qb"][l]
    qcols = []
    for h in range(MLA_HEADS):
        nope = wq[:, h * 96: h * 96 + 64]
        rot = wq[:, h * 96 + 64: (h + 1) * 96]
        qcols += [_slot_pad(nope, h % 2), rot, jnp.zeros((MLA_Q_LORA, 96), F32)]
    wqb = jnp.concatenate(qcols, axis=-1)
    wqb = jnp.concatenate([wqb, jnp.zeros((64, wqb.shape[1]), F32)], axis=0).astype(BF16)
    wkv = p["mla_w_kvb"][l]
    wkvb = jnp.concatenate([wkv[:, h * 128: h * 128 + 64] for h in range(MLA_HEADS)]
                           + [wkv[:, h * 128 + 64: (h + 1) * 128] for h in range(MLA_HEADS)],
                           axis=-1).astype(BF16)

    wb = p["w_branch"][l]
    perm = np.concatenate([np.arange(h * hd, (h + 1) * hd) for h in HEAD_PERM])
    w_branch = jnp.stack([wb[0], wb[1][perm], wb[2][perm], wb[3]]).astype(BF16)

    wr = jnp.concatenate([p["w_router"][l], jnp.zeros((D, LANES - N_EXPERTS), F32)], axis=-1)
    wr_hi = wr.astype(BF16)
    wr_lo = (wr - wr_hi.astype(F32)).astype(BF16)

    w = jnp.arange(GRID_W)
    col_start = jnp.clip(w - NA_KW // 2, 0, GRID_W - NA_KW)
    col_ok = (w[None, :] >= col_start[:, None]) & (w[None, :] < col_start[:, None] + NA_KW)
    dc_idx = jnp.clip(w[None, :] - w[:, None], 1 - NA_KW, NA_KW - 1) + NA_KW - 1
    t = jnp.where(col_ok[None, None], p["na_rpb"][l][:, :, dc_idx], NEG_INF)
    t = jnp.concatenate([t, jnp.full((NA_HEADS, 1, GRID_W, GRID_W), NEG_INF, F32)], axis=1)
    t = t.reshape(NA_HEADS * 16, GRID_W, GRID_W)
    zt = jnp.zeros_like(t)
    return dict(
        w_qkv=w_qkv, gqb=gqb, gk2=gk2, gmq=gmq, gmkv=gmkv, wqb=wqb, wkvb=wkvb,
        w_gates=w_in[:, 2144:].astype(BF16), w_branch=w_branch, w_out=p["w_out"][l].astype(BF16),
        ln1_g=p["ln1_g"][l][None], ln1_b=p["ln1_b"][l][None],
        ln2_g=p["ln2_g"][l][None], ln2_b=p["ln2_b"][l][None],
        wr_hi=wr_hi, wr_lo=wr_lo,
        w_gate=p["w_gate"][l].astype(BF16), w_up=p["w_up"][l].astype(BF16),
        w_down=p["w_down"][l].astype(BF16),
        tl=jnp.concatenate([t, zt], axis=-1), tr=jnp.concatenate([zt, t], axis=-1),
        sink=p["swa_sink"][l][np.array(HEAD_PERM)],
    )


def _rope_tables(n, ctx_len):
    pos = jnp.arange(n, dtype=jnp.int32)
    row = (pos // GRID_W).astype(F32)
    col = (pos % GRID_W).astype(F32)

    def axial(dim):
        half = dim // 4
        freqs = ROPE_THETA ** (-jnp.arange(half, dtype=F32) / half)
        parts_c, parts_s = [], []
        for pvec in (row, col):
            ang = pvec[:, None] * freqs[None, :]
            c, s = jnp.cos(ang), jnp.sin(ang)
            parts_c += [c, c]
            parts_s += [-s, s]
        return jnp.concatenate(parts_c, axis=-1), jnp.concatenate(parts_s, axis=-1)

    c64, s64 = axial(HEAD_DIM)
    cos = jnp.concatenate([c64, c64], axis=-1)
    sin = jnp.concatenate([s64, s64], axis=-1)
    c32, s32 = axial(MLA_ROPE)
    cosd = jnp.concatenate([c32, jnp.ones((n, LANES - MLA_ROPE), F32)], axis=-1)
    sind = jnp.concatenate([s32, jnp.zeros((n, LANES - MLA_ROPE), F32)], axis=-1)
    one = jnp.ones((ctx_len, LANES), F32)
    zero = jnp.zeros((ctx_len, LANES), F32)
    return (cos, sin, cosd, sind), (one, zero, one, zero)


def _expert_ffn(h2, aff, lw):
    B, n, D = h2.shape
    cap = CAPACITY * n // N_EXPERTS
    cap_pad = max(cap, TOKEN_BLOCK)
    slot_tile = TOKEN_BLOCK
    pos, off, tlo, thi = _select_call(aff, cap, slot_tile)
    xg, g = _gather_call(tlo, thi, pos, aff, h2, cap_pad, slot_tile)
    yw = _ffn_call(xg, g, lw)
    return _combine_call(off[..., 0].reshape(-1), pos, yw, n, D)


def kernel(x, c, ctx, c_ctx, w_mod, b_mod, w_in, na_rpb, gqa_q_norm, gqa_k_norm, swa_sink, mla_q_norm, mla_kv_norm, mla_w_qb, mla_w_kvb, w_branch, w_out, ln1_g, ln1_b, ln2_g, ln2_b, w_router, w_gate, w_up, w_down):
    p = dict(w_in=w_in, na_rpb=na_rpb, gqa_q_norm=gqa_q_norm, gqa_k_norm=gqa_k_norm, swa_sink=swa_sink,
             mla_q_norm=mla_q_norm, mla_kv_norm=mla_kv_norm, mla_w_qb=mla_w_qb, mla_w_kvb=mla_w_kvb,
             w_branch=w_branch, w_out=w_out, ln1_g=ln1_g, ln1_b=ln1_b, ln2_g=ln2_g, ln2_b=ln2_b,
             w_router=w_router, w_gate=w_gate, w_up=w_up, w_down=w_down)
    B, n, D = x.shape
    depth = w_in.shape[0]
    C = ctx.shape[1]
    alpha = (2 * depth) ** 0.25
    assert B + 1 <= 8
    cc = jnp.concatenate([c, c_ctx[None], jnp.zeros((8 - B - 1, D), F32)], axis=0)
    mod_all = _mod_call(cc, w_mod, b_mod)
    tabs, tabs_ctx = _rope_tables(n, C)
    xc = ctx
    for l in range(depth):
        lw = _prep_layer(l, p)
        need_ctx = l < depth - 1
        mods = [mod_all[l, :B, k * D:(k + 1) * D][:, None, :] for k in range(6)]
        mods_c = [jnp.broadcast_to(mod_all[l, B, k * D:(k + 1) * D][None, None, :], (B, 1, D)) for k in range(6)]
        pl_ = _inproj_call(x, mods[0], mods[1], lw, tabs)
        pc = _inproj_call(xc, mods_c[0], mods_c[1], lw, tabs_ctx)
        (naq, nak, nav, bq, bk, bv, cq, ck, cv, dq, dk, dv) = pl_
        (_, nakc, navc, _, bkc, bvc, _, ckc, cvc, _, dkc, dvc) = pc
        o_a = _na_call(naq, nak, nav, nakc, navc, lw["tl"], lw["tr"])
        o_b = _flash_call(bq, bk[:, None], bv, bkc[:, None], bvc, per_pair=False)
        o_c = _win_call(lw["sink"], cq, ck, cv, ckc, cvc)
        o_d = _flash_call(dq, dk, dv, dkc, dvc, per_pair=True)
        x1, h2, aff = _merge_call(x, (o_a, o_b, o_c, o_d), mods[:5], lw, alpha)
        y = _expert_ffn(h2, aff, lw)
        x = _ln2_call(x1, y, mods[5], lw["ln2_g"], lw["ln2_b"], alpha)
        if need_ctx:
            oc = _ctx_attn_call(lw["sink"], pc)
            xc1, hc2, affc = _merge_call(xc, oc, mods_c[:5], lw, alpha)
            yc = _expert_ffn(hc2, affc, lw)
            xc = _ln2_call(xc1, yc, mods_c[5], lw["ln2_g"], lw["ln2_b"], alpha)
    return x
```

```python
import functools

import numpy as np
import jax
import jax.numpy as jnp
from jax import lax
from jax.experimental import pallas as pl
from jax.experimental.pallas import tpu as pltpu

F32 = jnp.float32
BF16 = jnp.bfloat16

GRID_W = 64
HEAD_DIM = 64
ROPE_THETA = 10000.0
EPS = 1e-6
NEG_INF = -1e30
NA_HEADS = 4
NA_KH = 8
NA_KW = 16
NA_WIN_ROWS = 10
SWA_WINDOW = 128
QBLOCK = 128
MLA_HEADS = 4
MLA_NOPE = 64
MLA_ROPE = 32
MLA_V = 64
MLA_Q_LORA = 192
MLA_KV_LORA = 128
N_BRANCH = 4
BRANCH_W = 256
N_EXPERTS = 16
CAPACITY = 2
LANES = 128
TOKEN_BLOCK = 128
VMEM_LIMIT = 56 * 1024 * 1024
HEAD_PERM = (0, 2, 1, 3)
VT_ROWS = 80


def _cparams(*sem):
    return pltpu.CompilerParams(dimension_semantics=sem, vmem_limit_bytes=VMEM_LIMIT)


def _dot(a, b):
    return jnp.dot(a, b, preferred_element_type=F32)


def _dot_nt(a, b):
    return lax.dot_general(a, b, (((1,), (1,)), ((), ())), preferred_element_type=F32)


def _ln(x):
    mu = jnp.mean(x, axis=-1, keepdims=True)
    xc = x - mu
    var = jnp.mean(xc * xc, axis=-1, keepdims=True)
    return xc * lax.rsqrt(var + EPS)


def _modulate(x, shift, scale):
    return _ln(x) * (1.0 + scale) + shift


def _lane(shape, dim=None):
    return lax.broadcasted_iota(jnp.int32, shape, len(shape) - 1 if dim is None else dim)


def _rope(y, cos, sin_signed, half):
    w = y.shape[-1]
    low = (_lane((1, w)) & (2 * half - 1)) < half
    partner = jnp.where(low, pltpu.roll(y, w - half, 1), pltpu.roll(y, half, 1))
    return y * cos + partner * sin_signed


def _mod_kernel(c_ref, w_ref, b_ref, o_ref):
    c = c_ref[...]
    s = c / (1.0 + jnp.exp(-c))
    o_ref[0] = _dot(s.astype(BF16), w_ref[0].astype(BF16)) + b_ref[0]


def _mod_call(cc, w_mod, b_mod):
    L, D, N = w_mod.shape
    tn = N // 4
    return pl.pallas_call(
        _mod_kernel,
        grid=(L, N // tn),
        in_specs=[pl.BlockSpec((8, D), lambda l, j: (0, 0)),
                  pl.BlockSpec((1, D, tn), lambda l, j: (l, 0, j)),
                  pl.BlockSpec((1, 1, tn), lambda l, j: (l, 0, j))],
        out_specs=pl.BlockSpec((1, 8, tn), lambda l, j: (l, 0, j)),
        out_shape=jax.ShapeDtypeStruct((L, 8, N), F32),
        compiler_params=_cparams("arbitrary", "arbitrary"),
        name="mod_vectors",
    )(cc, w_mod, b_mod.reshape(L, 1, N))


_C_NA = 0
_C_BQ = 768
_C_BK = 1280
_C_BV = 1408
_C_CQ = 1536
_C_CK = 2048
_C_CV = 2176
_C_DQ = 2304
_C_DKV = 2560
_C_DKR = 2688
_C_END = 2816


def _inproj_kernel(x_ref, sh_ref, sc_ref, w_ref, cos_ref, sin_ref, cosd_ref, sind_ref,
                   gqb_ref, gk2_ref, gmq_ref, gmkv_ref, wqb_ref, wkvb_ref,
                   naq, nak, nav, bq, bk, bv, cq, ck, cv, dq, dk, dv):
    hb = _modulate(x_ref[0], sh_ref[0], sc_ref[0]).astype(BF16)

    def seg(a, b):
        return _dot(hb, w_ref[:, a:b])

    cos = cos_ref[...]
    sin = sin_ref[...]
    cosd = cosd_ref[...]
    sind = sind_ref[...]
    qscale = HEAD_DIM ** -0.5
    log2e = 1.4426950408889634

    def put_values_t(ref, pair, v):
        vt = v.T
        ones = jnp.ones((VT_ROWS - HEAD_DIM, vt.shape[1]), F32)
        for s in range(2):
            ref[0, pair, 0, s] = jnp.concatenate([vt[s * HEAD_DIM:(s + 1) * HEAD_DIM], ones], axis=0).astype(BF16)

    z = seg(_C_NA, _C_NA + 768)
    naq[0] = (z[:, 0:256] * qscale).astype(BF16)
    nak[0] = z[:, 256:512].astype(BF16)
    nav[0] = z[:, 512:768].astype(BF16)

    z = seg(_C_BQ, _C_BK)
    for s in range(4):
        zs = z[:, s * LANES:(s + 1) * LANES]
        ss = jnp.sum(zs * zs, axis=-1, keepdims=True) * (1.0 / HEAD_DIM)
        y = zs * lax.rsqrt(ss + EPS) * gqb_ref[s:s + 1, :]
        bq[0, s] = (_rope(y, cos, sin, 16) * (qscale * log2e)).astype(BF16)
    z = seg(_C_BK, _C_BV)
    left = _lane((1, LANES)) < HEAD_DIM
    z2 = z * z
    ss0 = jnp.sum(jnp.where(left, z2, 0.0), axis=-1, keepdims=True)
    ss1 = jnp.sum(jnp.where(left, 0.0, z2), axis=-1, keepdims=True)
    inv = jnp.where(left, lax.rsqrt(ss0 * (1.0 / HEAD_DIM) + EPS), lax.rsqrt(ss1 * (1.0 / HEAD_DIM) + EPS))
    bk[0] = _rope(z * inv * gk2_ref[...], cos, sin, 16).astype(BF16)
    put_values_t(bv, 0, seg(_C_BV, _C_CQ))

    z = seg(_C_CQ, _C_CK)
    for s in range(4):
        cq[0, s] = (_rope(z[:, s * LANES:(s + 1) * LANES], cos, sin, 16) * qscale).astype(BF16)
    ck[0] = _rope(seg(_C_CK, _C_CV), cos, sin, 16).astype(BF16)
    cv[0] = seg(_C_CV, _C_DQ).astype(BF16)

    z = seg(_C_DQ, _C_DKV)
    ss = jnp.sum(z * z, axis=-1, keepdims=True) * (1.0 / MLA_Q_LORA)
    cqn = (z * lax.rsqrt(ss + EPS) * gmq_ref[...]).astype(BF16)
    qd = _dot(cqn, wqb_ref[...])
    dscale = (MLA_NOPE + MLA_ROPE) ** -0.5 * log2e
    for h in range(MLA_HEADS):
        nope = qd[:, h * 256:h * 256 + LANES] * dscale
        rot = _rope(qd[:, h * 256 + LANES:(h + 1) * 256], cosd, sind, 8) * dscale
        dq[0, h] = jnp.concatenate([nope, rot], axis=-1).astype(BF16)
    z = seg(_C_DKV, _C_DKR)
    ss = jnp.sum(z * z, axis=-1, keepdims=True) * (1.0 / MLA_KV_LORA)
    ckvn = (z * lax.rsqrt(ss + EPS) * gmkv_ref[...]).astype(BF16)
    kv = _dot(ckvn, wkvb_ref[...])
    kr = _rope(seg(_C_DKR, _C_END), cosd, sind, 8)
    for p in range(2):
        dk[0, p] = jnp.concatenate([kv[:, p * LANES:(p + 1) * LANES], kr], axis=-1).astype(BF16)
        put_values_t(dv, p, kv[:, 256 + p * LANES:256 + (p + 1) * LANES])


def _inproj_call(x, shift, scale, lw, tabs):
    B, n, D = x.shape
    tm = min(512, n)
    cos, sin, cosd, sind = tabs
    row = lambda b, i: (b, i, 0)
    hrow = lambda b, i: (b, 0, i, 0)
    const2 = lambda b, i: (0, 0)
    tab = pl.BlockSpec((tm, LANES), lambda b, i: (i, 0))
    mod = pl.BlockSpec((1, 1, D), lambda b, i: (b, 0, 0))
    nt = n // tm
    vt_spec = lambda pairs: ((B, pairs, nt, 2, VT_ROWS, tm),
                             pl.BlockSpec((1, pairs, 1, 2, VT_ROWS, tm), lambda b, i: (b, 0, i, 0, 0, 0)))
    slab = lambda kd: ((B, 4, n, kd), pl.BlockSpec((1, 4, tm, kd), hrow))
    tokm = lambda w: ((B, n, w), pl.BlockSpec((1, tm, w), row))
    outs = [tokm(256)] * 3
    outs += [slab(LANES), tokm(LANES), vt_spec(1)]
    outs += [slab(LANES), tokm(LANES), tokm(LANES)]
    outs += [slab(256), ((B, 2, n, 256), pl.BlockSpec((1, 2, tm, 256), hrow)), vt_spec(2)]
    return pl.pallas_call(
        _inproj_kernel,
        grid=(B, n // tm),
        in_specs=[pl.BlockSpec((1, tm, D), row), mod, mod,
                  pl.BlockSpec((D, _C_END), const2), tab, tab, tab, tab,
                  pl.BlockSpec((4, LANES), const2), pl.BlockSpec((1, LANES), const2),
                  pl.BlockSpec((1, 256), const2), pl.BlockSpec((1, LANES), const2),
                  pl.BlockSpec((256, 1024), const2), pl.BlockSpec((LANES, 512), const2)],
        out_specs=[o[1] for o in outs],
        out_shape=[jax.ShapeDtypeStruct(o[0], BF16) for o in outs],
        compiler_params=_cparams("arbitrary", "arbitrary"),
        name="in_proj",
    )(x, shift, scale, lw["w_qkv"], cos, sin, cosd, sind,
      lw["gqb"], lw["gk2"], lw["gmq"], lw["gmkv"], lw["wqb"], lw["wkvb"])


def _flash_kernel(q_ref, k_ref, vt_ref, o_ref, acc_ref, m_ref, s_ref, *, tq, nk, per_trip):
    q = q_ref[0].reshape(2 * tq, q_ref.shape[-1])
    m_ref[...] = jnp.full(m_ref.shape, NEG_INF, F32)
    acc_ref[...] = jnp.zeros(acc_ref.shape, F32)

    def update(s, vt):
        m_prev = m_ref[...]
        m_new = jnp.maximum(m_prev, jnp.max(s, axis=0, keepdims=True))
        a = jnp.exp2(m_prev - m_new)
        p = jnp.exp2(s - m_new).astype(BF16)
        for h in range(2):
            cols = slice(h * tq, (h + 1) * tq)
            acc_ref[h] = a[:, cols] * acc_ref[h] + _dot(vt[h], p[:, cols])
        m_ref[...] = m_new

    s_ref[0] = _dot_nt(k_ref[0, 0, 0], q)

    def body(jj, carry):
        for u in range(per_trip):
            j = per_trip * jj + u
            s_ref[(u + 1) % 2] = _dot_nt(k_ref[0, 0, j + 1], q)
            update(s_ref[u % 2], vt_ref[0, 0, j])
        return carry

    lax.fori_loop(0, (nk - 1) // per_trip, body, 0)
    update(s_ref[0], vt_ref[0, 0, nk - 1])
    o = [acc_ref[h][:HEAD_DIM] / acc_ref[h][HEAD_DIM:HEAD_DIM + 1] for h in range(2)]
    o_ref[0] = jnp.concatenate(o, axis=0).T.astype(BF16)


def _flash_call(q, k, vt, kc, vct, *, per_pair):
    B, _, n, kd = q.shape
    P = k.shape[1]
    C = kc.shape[2]
    nk, tk = vt.shape[2], vt.shape[5]
    assert tk % C == 0 and nk % 2 == 0
    k_all = jnp.concatenate([k] + [kc] * (tk // C), axis=2).reshape(B, P, nk + 1, tk, kd)
    vct_pad = jnp.concatenate([vct, jnp.zeros(vct.shape[:-1] + (tk - C,), vct.dtype)], axis=-1)
    vt_all = jnp.concatenate([vt, vct_pad], axis=2)
    tq = min(512, n)
    pidx =(lambda b, p, i: (b, p, 0, 0, 0)) if per_pair else (lambda b, p, i: (b, 0, 0, 0, 0))
    vidx = (lambda b, p, i: (b, p, 0, 0, 0, 0)) if per_pair else (lambda b, p, i: (b, 0, 0, 0, 0, 0))
    kern = functools.partial(_flash_kernel, tq=tq, nk=nk + 1, per_trip=4 if nk % 4 == 0 else 2)
    return pl.pallas_call(
        kern,
        grid=(B, 2, n // tq),
        in_specs=[pl.BlockSpec((1, 2, tq, kd), lambda b, p, i: (b, p, i, 0)),
                  pl.BlockSpec((1, 1, nk + 1, tk, kd), pidx),
                  pl.BlockSpec((1, 1, nk + 1, 2, VT_ROWS, tk), vidx)],
        out_specs=pl.BlockSpec((1, tq, LANES), lambda b, p, i: (b, i, p)),
        out_shape=jax.ShapeDtypeStruct((B, n, 256), BF16),
        scratch_shapes=[pltpu.VMEM((2, VT_ROWS, tq), F32), pltpu.VMEM((1, 2 * tq), F32),
                        pltpu.VMEM((2, tk, 2 * tq), F32)],
        compiler_params=_cparams("arbitrary", "arbitrary", "arbitrary"),
        name="dense_attn",
    )(q, k_all, vt_all)


def _na_kernel(q_ref, k_ref, v_ref, kc_ref, vc_ref, tl_ref, tr_ref, o_ref, *, rows):
    i = pl.program_id(1)
    r0 = 2 * i
    w0 = jnp.clip(r0 - NA_KH // 2, 0, rows - NA_WIN_ROWS)
    start = pl.multiple_of(w0 * GRID_W, GRID_W)
    kw = k_ref[0, pl.ds(start, NA_WIN_ROWS * GRID_W), :]
    vw = v_ref[0, pl.ds(start, NA_WIN_ROWS * GRID_W), :]
    kc = kc_ref[0]
    vc = vc_ref[0]
    q = q_ref[0]
    head_of_lane = _lane((1, 256)) >> 6

    def table_index(a, j):
        qr = r0 + a
        kr = w0 + j
        st = jnp.clip(qr - NA_KH // 2, 0, rows - NA_KH)
        ok = (kr >= st) & (kr < st + NA_KH)
        return jnp.where(ok, kr - qr + NA_KH - 1, 2 * NA_KH - 1)

    out = jnp.zeros((2 * GRID_W, 256), F32)
    for h in range(NA_HEADS):
        qm = jnp.where(head_of_lane == h, q, jnp.zeros_like(q))
        bias_rows = []
        for a in range(2):
            blocks = [tl_ref[h * 16 + table_index(a, 2 * jp)] + tr_ref[h * 16 + table_index(a, 2 * jp + 1)]
                      for jp in range(NA_WIN_ROWS // 2)]
            bias_rows.append(jnp.concatenate(blocks, axis=1))
        s_lat = _dot_nt(qm, kw) + jnp.concatenate(bias_rows, axis=0)
        s_ctx = _dot_nt(qm, kc)
        m = jnp.maximum(jnp.max(s_lat, axis=-1, keepdims=True), jnp.max(s_ctx, axis=-1, keepdims=True))
        p_lat = jnp.exp(s_lat - m)
        p_ctx = jnp.exp(s_ctx - m)
        l = jnp.sum(p_lat, axis=-1, keepdims=True) + jnp.sum(p_ctx, axis=-1, keepdims=True)
        pv = _dot(p_lat.astype(BF16), vw) + _dot(p_ctx.astype(BF16), vc)
        out = jnp.where(head_of_lane == h, pv / l, out)
    o_ref[0] = out.astype(BF16)


def _na_call(q, k, v, kc, vc, tl, tr):
    B, n, _ = q.shape
    C = kc.shape[1]
    rows = n // GRID_W
    assert rows >= NA_WIN_ROWS and rows % 2 == 0
    full = lambda b, i: (b, 0, 0)
    return pl.pallas_call(
        functools.partial(_na_kernel, rows=rows),
        grid=(B, rows // 2),
        in_specs=[pl.BlockSpec((1, 2 * GRID_W, 256), lambda b, i: (b, i, 0)),
                  pl.BlockSpec((1, n, 256), full), pl.BlockSpec((1, n, 256), full),
                  pl.BlockSpec((1, C, 256), full), pl.BlockSpec((1, C, 256), full),
                  pl.BlockSpec(tl.shape, lambda b, i: (0, 0, 0)),
                  pl.BlockSpec(tr.shape, lambda b, i: (0, 0, 0))],
        out_specs=pl.BlockSpec((1, 2 * GRID_W, 256), lambda b, i: (b, i, 0)),
        out_shape=jax.ShapeDtypeStruct((B, n, 256), BF16),
        compiler_params=_cparams("arbitrary", "arbitrary"),
        name="nbr_attn",
    )(q, k, v, kc, vc, tl, tr)


def _win_kernel(sink_ref, q_ref, k_ref, v_ref, kc_ref, vc_ref, o_ref, *, n):
    i = pl.program_id(1)
    band = 3 * QBLOCK
    ws = pl.multiple_of(jnp.clip((i - 1) * QBLOCK, 0, n - band), QBLOCK)
    kw = k_ref[0, pl.ds(ws, band), :]
    vw = v_ref[0, pl.ds(ws, band), :]
    kc = kc_ref[0]
    vc = vc_ref[0]
    qpos = i * QBLOCK + lax.broadcasted_iota(jnp.int32, (QBLOCK, band), 0)
    kpos = ws + lax.broadcasted_iota(jnp.int32, (QBLOCK, band), 1)
    ok = jnp.abs(qpos - kpos) <= SWA_WINDOW
    outs = []
    for s in range(4):
        q = q_ref[0, s]
        s_lat = jnp.where(ok, _dot_nt(q, kw), NEG_INF)
        s_ctx = _dot_nt(q, kc)
        snk = sink_ref[s]
        m = jnp.maximum(jnp.maximum(jnp.max(s_lat, axis=-1, keepdims=True),
                                    jnp.max(s_ctx, axis=-1, keepdims=True)), snk)
        p_lat = jnp.exp(s_lat - m)
        p_ctx = jnp.exp(s_ctx - m)
        l = (jnp.sum(p_lat, axis=-1, keepdims=True) + jnp.sum(p_ctx, axis=-1, keepdims=True)
             + jnp.exp(snk - m))
        outs.append((_dot(p_lat.astype(BF16), vw) + _dot(p_ctx.astype(BF16), vc)) / l)
    left = _lane((1, LANES)) < HEAD_DIM
    o_ref[0] = jnp.concatenate([jnp.where(left, outs[0], outs[1]),
                                jnp.where(left, outs[2], outs[3])], axis=-1).astype(BF16)


def _win_call(sink, q, k, v, kc, vc):
    B, _, n, _ = q.shape
    C = kc.shape[1]
    assert n >= 3 * QBLOCK
    full = lambda b, i, s: (b, 0, 0)
    return pl.pallas_call(
        functools.partial(_win_kernel, n=n),
        grid_spec=pltpu.PrefetchScalarGridSpec(
            num_scalar_prefetch=1,
            grid=(B, n // QBLOCK),
            in_specs=[pl.BlockSpec((1, 4, QBLOCK, LANES), lambda b, i, s: (b, 0, i, 0)),
                      pl.BlockSpec((1, n, LANES), full), pl.BlockSpec((1, n, LANES), full),
                      pl.BlockSpec((1, C, LANES), full), pl.BlockSpec((1, C, LANES), full)],
            out_specs=pl.BlockSpec((1, QBLOCK, 256), lambda b, i, s: (b, i, 0))),
        out_shape=jax.ShapeDtypeStruct((B, n, 256), BF16),
        compiler_params=_cparams("arbitrary", "arbitrary"),
        name="window_attn",
    )(sink, q, k, v, kc, vc)


def _attend(q, k, v, sink=None, v_transposed=False):
    s = _dot_nt(q, k)
    m = jnp.max(s, axis=-1, keepdims=True)
    if sink is not None:
        m = jnp.maximum(m, sink)
    p = jnp.exp2(s - m) if v_transposed else jnp.exp(s - m)
    l = jnp.sum(p, axis=-1, keepdims=True)
    if sink is not None:
        l = l + jnp.exp(sink - m)
    pv = _dot_nt(p.astype(BF16), v) if v_transposed else _dot(p.astype(BF16), v)
    return pv / l


def _ctx_attn_kernel(sink_ref, naq, nak, nav, bq, bk, bv, cq, ck, cv, dq, dk, dv, oa, ob, oc, od):
    left = _lane((1, LANES)) < HEAD_DIM
    head_of_lane = _lane((1, 256)) >> 6
    q = naq[0]
    out = jnp.zeros(q.shape, F32)
    for h in range(NA_HEADS):
        qm = jnp.where(head_of_lane == h, q, jnp.zeros_like(q))
        out = jnp.where(head_of_lane == h, _attend(qm, nak[0], nav[0]), out)
    oa[0] = out.astype(BF16)
    def both_heads(vt):
        return jnp.concatenate([vt[0, :HEAD_DIM], vt[1, :HEAD_DIM]], axis=0)

    o = [_attend(bq[0, s], bk[0], both_heads(bv[0, 0, 0]), v_transposed=True) for s in range(4)]
    ob[0] = jnp.concatenate([jnp.where(left, o[0], o[1]), jnp.where(left, o[2], o[3])], axis=-1).astype(BF16)
    o = [_attend(cq[0, s], ck[0], cv[0], sink_ref[s]) for s in range(4)]
    oc[0] = jnp.concatenate([jnp.where(left, o[0], o[1]), jnp.where(left, o[2], o[3])], axis=-1).astype(BF16)
    o = [_attend(dq[0, h], dk[0, h // 2], both_heads(dv[0, h // 2, 0]), v_transposed=True)
         for h in range(MLA_HEADS)]
    od[0] = jnp.concatenate([jnp.where(left, o[0], o[1]), jnp.where(left, o[2], o[3])], axis=-1).astype(BF16)


def _ctx_attn_call(sink, pc):
    B, C, _ = pc[0].shape
    names = pc
    specs = []
    for a in names:
        nd = a.ndim
        specs.append(pl.BlockSpec((1,) + a.shape[1:], (lambda b, s, nd=nd: (b,) + (0,) * (nd - 1))))
    out_spec = pl.BlockSpec((1, C, 256), lambda b, s: (b, 0, 0))
    return pl.pallas_call(
        _ctx_attn_kernel,
        grid_spec=pltpu.PrefetchScalarGridSpec(
            num_scalar_prefetch=1, grid=(B,), in_specs=specs, out_specs=[out_spec] * 4),
        out_shape=[jax.ShapeDtypeStruct((B, C, 256), BF16)] * 4,
        compiler_params=_cparams("arbitrary"),
        name="ctx_attn",
    )(sink, *pc)


def _merge_kernel(x_ref, oa, ob, oc, od, sha, sca, ga, shf, scf, wg_ref, wb_ref, wo_ref,
                  l1g, l1b, wrh_ref, wrl_ref, x1_ref, h2_ref, aff_ref, *, alpha):
    x = x_ref[0]
    D = x.shape[-1]
    hb = _modulate(x, sha[0], sca[0]).astype(BF16)
    merged = None
    for i, o in enumerate((oa, ob, oc, od)):
        g = 1.0 / (1.0 + jnp.exp(-_dot(hb, wg_ref[:, i * D:(i + 1) * D])))
        term = g * _dot(o[0], wb_ref[i])
        merged = term if merged is None else merged + term
    y = _dot(merged.astype(BF16), wo_ref[...])
    x1 = _ln(alpha * x + ga[0] * y) * l1g[...] + l1b[...]
    x1_ref[0] = x1
    h2 = _modulate(x1, shf[0], scf[0])
    h2_hi = h2.astype(BF16)
    h2_ref[0] = h2_hi
    h2_lo = (h2 - h2_hi.astype(F32)).astype(BF16)
    logits = _dot(h2_hi, wrh_ref[...]) + _dot(h2_hi, wrl_ref[...]) + _dot(h2_lo, wrh_ref[...])
    logits = jnp.where(_lane((1, LANES)) < N_EXPERTS, logits, NEG_INF)
    e = jnp.exp(logits - jnp.max(logits, axis=-1, keepdims=True))
    aff_t = (e / jnp.sum(e, axis=-1, keepdims=True)).T
    for k in range(aff_ref.shape[1]):
        aff_ref[0, k] = aff_t[:N_EXPERTS, k * TOKEN_BLOCK:(k + 1) * TOKEN_BLOCK]


def _merge_call(x, outs, mods, lw, alpha):
    B, n, D = x.shape
    tm = min(256, n)
    nb = tm // TOKEN_BLOCK
    row = lambda b, i: (b, i, 0)
    mod = pl.BlockSpec((1, 1, D), lambda b, i: (b, 0, 0))
    c2 = lambda b, i: (0, 0)
    obr = pl.BlockSpec((1, tm, 256), row)
    return pl.pallas_call(
        functools.partial(_merge_kernel, alpha=alpha),
        grid=(B, n // tm),
        in_specs=[pl.BlockSpec((1, tm, D), row), obr, obr, obr, obr, mod, mod, mod, mod, mod,
                  pl.BlockSpec((D, N_BRANCH * D), c2),
                  pl.BlockSpec((N_BRANCH, BRANCH_W, D), lambda b, i: (0, 0, 0)),
                  pl.BlockSpec((D, D), c2), pl.BlockSpec((1, D), c2), pl.BlockSpec((1, D), c2),
                  pl.BlockSpec((D, LANES), c2), pl.BlockSpec((D, LANES), c2)],
        out_specs=[pl.BlockSpec((1, tm, D), row), pl.BlockSpec((1, tm, D), row),
                   pl.BlockSpec((1, nb, N_EXPERTS, TOKEN_BLOCK), lambda b, i: (b, i, 0, 0))],
        out_shape=[jax.ShapeDtypeStruct((B, n, D), F32), jax.ShapeDtypeStruct((B, n, D), BF16),
                   jax.ShapeDtypeStruct((B, n // TOKEN_BLOCK, N_EXPERTS, TOKEN_BLOCK), F32)],
        compiler_params=_cparams("arbitrary", "arbitrary"),
        name="merge_router",
    )(x, *outs, *mods, lw["w_gates"], lw["w_branch"], lw["w_out"], lw["ln1_g"], lw["ln1_b"],
      lw["wr_hi"], lw["wr_lo"])


def _select_kernel(aff_ref, pos_ref, off_ref, tlo_ref, thi_ref, *, cap, slot_tile):
    a = aff_ref[0]
    nb = a.shape[0]
    rows = nb * N_EXPERTS
    bits = lax.bitcast_convert_type(a, jnp.int32)
    capf = jnp.float32(cap)

    def count(mask):
        c = jnp.sum(jnp.where(mask, 1.0, 0.0), axis=0)
        return jnp.broadcast_to(jnp.sum(c, axis=-1, keepdims=True), c.shape)

    def search(it, lo):
        cand = lo | lax.shift_left(jnp.int32(1), 30 - it)
        return jnp.where(count(bits >= cand[None]) >= capf, cand, lo)

    thr = lax.fori_loop(0, 31, search, jnp.zeros((N_EXPERTS, TOKEN_BLOCK), jnp.int32))

    r = lax.broadcasted_iota(jnp.int32, (rows, rows), 0)
    c = lax.broadcasted_iota(jnp.int32, (rows, rows), 1)
    earlier = jnp.where(((r & (N_EXPERTS - 1)) == (c & (N_EXPERTS - 1))) & ((c >> 4) < (r >> 4)), 1.0, 0.0).astype(BF16)
    ti = lax.broadcasted_iota(jnp.int32, (TOKEN_BLOCK, TOKEN_BLOCK), 0)
    tj = lax.broadcasted_iota(jnp.int32, (TOKEN_BLOCK, TOKEN_BLOCK), 1)
    tri = jnp.where(ti <= tj, 1.0, 0.0).astype(BF16)
    ones = jnp.ones((TOKEN_BLOCK, TOKEN_BLOCK), BF16)

    def prefix(mask):
        m2 = jnp.where(mask, 1.0, 0.0).reshape(rows, TOKEN_BLOCK)
        mb = m2.astype(BF16)
        within = _dot(mb, tri)
        tot = _dot(mb, ones)
        off = _dot(earlier, tot.astype(BF16))
        shp = (nb, N_EXPERTS, TOKEN_BLOCK)
        return (off + within - m2).reshape(shp), off.reshape(shp), tot.reshape(shp)

    gt = bits > thr[None]
    eq = bits == thr[None]
    need = capf - count(gt)
    eq_rank, _, _ = prefix(eq)
    sel = gt | (eq & (eq_rank < need[None]))
    excl, off, tot = prefix(sel)
    pos_ref[0] = jnp.where(sel, excl, -1.0)
    off_ref[0] = off.astype(jnp.int32)
    tile_start = (_lane((1, 1, TOKEN_BLOCK)) * slot_tile).astype(F32)
    tlo_ref[0] = jnp.sum(jnp.where(off + tot <= tile_start, 1, 0), axis=0).astype(jnp.int32)
    thi_ref[0] = jnp.sum(jnp.where(off < tile_start + slot_tile, 1, 0), axis=0).astype(jnp.int32)


def _select_call(aff, cap, slot_tile):
    B, nb, E, _ = aff.shape
    blk = pl.BlockSpec((1, nb, E, TOKEN_BLOCK), lambda b: (b, 0, 0, 0))
    rng = pl.BlockSpec((1, E, TOKEN_BLOCK), lambda b: (b, 0, 0))
    return pl.pallas_call(
        functools.partial(_select_kernel, cap=cap, slot_tile=slot_tile),
        grid=(B,),
        in_specs=[blk],
        out_specs=[blk, blk, rng, rng],
        out_shape=[jax.ShapeDtypeStruct(aff.shape, F32), jax.ShapeDtypeStruct(aff.shape, jnp.int32),
                   jax.ShapeDtypeStruct((B, E, TOKEN_BLOCK), jnp.int32),
                   jax.ShapeDtypeStruct((B, E, TOKEN_BLOCK), jnp.int32)],
        compiler_params=_cparams("arbitrary"),
        name="expert_select",
    )(aff)


def _gather_kernel(tlo_ref, thi_ref, pos_ref, aff_ref, h_ref, xg_ref, g_ref, acc_ref, gacc_ref, *, slot_tile):
    b = pl.program_id(0)
    e = pl.program_id(1)
    n_tiles = xg_ref.shape[2] // slot_tile
    for t in range(n_tiles):
        acc_ref[...] = jnp.zeros(acc_ref.shape, F32)
        gacc_ref[...] = jnp.zeros(gacc_ref.shape, F32)
        slot = (t * slot_tile + lax.broadcasted_iota(jnp.int32, (slot_tile, 2 * TOKEN_BLOCK), 0)).astype(F32)

        def body(pb, carry):
            p = jnp.concatenate([pos_ref[0, 2 * pb, pl.ds(e, 1), :], pos_ref[0, 2 * pb + 1, pl.ds(e, 1), :]], axis=-1)
            a = jnp.concatenate([aff_ref[0, 2 * pb, pl.ds(e, 1), :], aff_ref[0, 2 * pb + 1, pl.ds(e, 1), :]], axis=-1)
            hit = p == slot
            tok = pl.multiple_of(pb * (2 * TOKEN_BLOCK), 2 * TOKEN_BLOCK)
            acc_ref[...] += _dot(jnp.where(hit, 1.0, 0.0).astype(BF16), h_ref[0, pl.ds(tok, 2 * TOKEN_BLOCK), :])
            ga = jnp.where(hit, a, 0.0)
            gacc_ref[...] += ga[:, :TOKEN_BLOCK] + ga[:, TOKEN_BLOCK:]
            return carry

        lax.fori_loop(tlo_ref[b, e, t] >> 1, (thi_ref[b, e, t] + 1) >> 1, body, 0)
        xg_ref[0, 0, t * slot_tile:(t + 1) * slot_tile, :] = acc_ref[...].astype(BF16)
        g_ref[0, 0, t * slot_tile:(t + 1) * slot_tile, :] = jnp.sum(gacc_ref[...], axis=-1, keepdims=True)


def _gather_call(tlo, thi, pos, aff, h2, cap_pad, slot_tile):
    B, n, D = h2.shape
    nb = n // TOKEN_BLOCK
    blk = pl.BlockSpec((1, nb, N_EXPERTS, TOKEN_BLOCK), lambda b, e, *_: (b, 0, 0, 0))
    return pl.pallas_call(
        functools.partial(_gather_kernel, slot_tile=slot_tile),
        grid_spec=pltpu.PrefetchScalarGridSpec(
            num_scalar_prefetch=2,
            grid=(B, N_EXPERTS),
            in_specs=[blk, blk, pl.BlockSpec((1, n, D), lambda b, e, *_: (b, 0, 0))],
            out_specs=[pl.BlockSpec((1, 1, cap_pad, D), lambda b, e, *_: (b, e, 0, 0)),
                       pl.BlockSpec((1, 1, cap_pad, 1), lambda b, e, *_: (b, e, 0, 0))],
            scratch_shapes=[pltpu.VMEM((slot_tile, D), F32), pltpu.VMEM((slot_tile, TOKEN_BLOCK), F32)]),
        out_shape=[jax.ShapeDtypeStruct((B, N_EXPERTS, cap_pad, D), BF16),
                   jax.ShapeDtypeStruct((B, N_EXPERTS, cap_pad, 1), F32)],
        compiler_params=_cparams("arbitrary", "arbitrary"),
        name="expert_gather",
    )(tlo, thi, pos, aff, h2)


def _ffn_kernel(xg_ref, g_ref, wg_ref, wu_ref, wd_ref, y_ref):
    xg = xg_ref[0, 0]
    a = _dot(xg, wg_ref[0])
    u = _dot(xg, wu_ref[0])
    hmid = (a / (1.0 + jnp.exp(-a)) * u).astype(BF16)
    y_ref[0, 0] = (_dot(hmid, wd_ref[0]) * g_ref[0, 0]).astype(BF16)


def _ffn_call(xg, g, lw):
    B, E, cp, D = xg.shape
    F = lw["w_gate"].shape[-1]
    tok = lambda e, b: (b, e, 0, 0)
    wsp = lambda e, b: (e, 0, 0)
    return pl.pallas_call(
        _ffn_kernel,
        grid=(E, B),
        in_specs=[pl.BlockSpec((1, 1, cp, D), tok), pl.BlockSpec((1, 1, cp, 1), tok),
                  pl.BlockSpec((1, D, F), wsp), pl.BlockSpec((1, D, F), wsp), pl.BlockSpec((1, F, D), wsp)],
        out_specs=pl.BlockSpec((1, 1, cp, D), tok),
        out_shape=jax.ShapeDtypeStruct((B, E, cp, D), BF16),
        compiler_params=_cparams("arbitrary", "arbitrary"),
        name="expert_mlp",
    )(xg, g, lw["w_gate"], lw["w_up"], lw["w_down"])


def _combine_kernel(off_ref, pos_ref, yw_ref, y_ref, *, window, blocks_per_step):
    b = pl.program_id(0)
    t = pl.program_id(2)
    cap_pad = yw_ref.shape[2]
    nb_total = pl.num_programs(2) * blocks_per_step
    for k in range(blocks_per_step):
        blk = t * blocks_per_step + k
        acc = jnp.zeros((TOKEN_BLOCK, yw_ref.shape[-1]), F32)
        for e in range(N_EXPERTS):
            start = off_ref[(b * nb_total + blk) * N_EXPERTS + e]
            w0 = pl.multiple_of(jnp.minimum(start & -16, cap_pad - window), 16)
            p = pos_ref[0, k, e:e + 1, :]
            slot = (w0 + lax.broadcasted_iota(jnp.int32, (window, TOKEN_BLOCK), 0)).astype(F32)
            hit = jnp.where(p == slot, 1.0, 0.0).T.astype(BF16)
            acc = acc + _dot(hit, yw_ref[0, e, pl.ds(w0, window), :])
        y_ref[0, k * TOKEN_BLOCK:(k + 1) * TOKEN_BLOCK, :] = acc


def _combine_call(off_flat, pos, yw, n, D):
    B, E, cap_pad, _ = yw.shape
    nb = n // TOKEN_BLOCK
    window = min(256, cap_pad)
    dh = min(512, D)
    bps = min(4, nb)
    return pl.pallas_call(
        functools.partial(_combine_kernel, window=window, blocks_per_step=bps),
        grid_spec=pltpu.PrefetchScalarGridSpec(
            num_scalar_prefetch=1,
            grid=(B, D // dh, nb // bps),
            in_specs=[pl.BlockSpec((1, bps, E, TOKEN_BLOCK), lambda b, d, t, *_: (b, t, 0, 0)),
                      pl.BlockSpec((1, E, cap_pad, dh), lambda b, d, t, *_: (b, 0, 0, d))],
            out_specs=pl.BlockSpec((1, bps * TOKEN_BLOCK, dh), lambda b, d, t, *_: (b, t, d))),
        out_shape=jax.ShapeDtypeStruct((B, n, D), F32),
        compiler_params=_cparams("arbitrary", "arbitrary", "arbitrary"),
        name="expert_combine",
    )(off_flat, pos, yw)


def _ln2_kernel(x_ref, y_ref, g_ref, lg, lb, o_ref, *, alpha):
    o_ref[0] = _ln(alpha * x_ref[0] + g_ref[0] * y_ref[0]) * lg[...] + lb[...]


def _ln2_call(x1, y, gf, lg, lb, alpha):
    B, n, D = x1.shape
    tm = min(512, n)
    row = pl.BlockSpec((1, tm, D), lambda b, i: (b, i, 0))
    vec = pl.BlockSpec((1, D), lambda b, i: (0, 0))
    return pl.pallas_call(
        functools.partial(_ln2_kernel, alpha=alpha),
        grid=(B, n // tm),
        in_specs=[row, row, pl.BlockSpec((1, 1, D), lambda b, i: (b, 0, 0)), vec, vec],
        out_specs=row,
        out_shape=jax.ShapeDtypeStruct((B, n, D), F32),
        compiler_params=_cparams("arbitrary", "arbitrary"),
        name="ffn_residual_ln",
    )(x1, y, gf, lg, lb)


def _slot_pad(w, slot):
    z = jnp.zeros_like(w)
    return jnp.concatenate([w, z] if slot == 0 else [z, w], axis=-1)


def _prep_layer(l, p):
    w_in = p["w_in"][l]
    D = w_in.shape[0]
    hd = HEAD_DIM

    def gq_cols(base):
        q = [_slot_pad(w_in[:, base + h * hd: base + (h + 1) * hd], h // 2) for h in HEAD_PERM]
        return q + [w_in[:, base + 256: base + 512]]

    zc = lambda k: jnp.zeros((D, k), F32)
    cols = [w_in[:, 0:768]] + gq_cols(768) + gq_cols(1280)
    cols += [w_in[:, 1792:1984], zc(64), w_in[:, 1984:2112], w_in[:, 2112:2144], zc(96)]
    w_qkv = jnp.concatenate(cols, axis=-1).astype(BF16)
    assert w_qkv.shape[1] == _C_END

    gq = p["gqa_q_norm"][l]
    gk = p["gqa_k_norm"][l]
    zg = jnp.zeros_like(gq)
    gqb = jnp.stack([jnp.concatenate([gq, zg] if h // 2 == 0 else [zg, gq]) for h in HEAD_PERM])
    gk2 = jnp.concatenate([gk, gk])[None]
    gmq = jnp.concatenate([p["mla_q_norm"][l], jnp.zeros((64,), F32)])[None]
    gmkv = p["mla_kv_norm"][l][None]

    wq = p["mla_w_qb"][l]
    qcols = []
    for h in range(MLA_HEADS):
        nope = wq[:, h * 96: h * 96 + 64]
        rot = wq[:, h * 96 + 64: (h + 1) * 96]
        qcols += [_slot_pad(nope, h % 2), rot, jnp.zeros((MLA_Q_LORA, 96), F32)]
    wqb = jnp.concatenate(qcols, axis=-1)
    wqb = jnp.concatenate([wqb, jnp.zeros((64, wqb.shape[1]), F32)], axis=0).astype(BF16)
    wkv = p["mla_w_kvb"][l]
    wkvb = jnp.concatenate([wkv[:, h * 128: h * 128 + 64] for h in range(MLA_HEADS)]
                           + [wkv[:, h * 128 + 64: (h + 1) * 128] for h in range(MLA_HEADS)],
                           axis=-1).astype(BF16)

    wb = p["w_branch"][l]
    perm = np.concatenate([np.arange(h * hd, (h + 1) * hd) for h in HEAD_PERM])
    w_branch = jnp.stack([wb[0], wb[1][perm], wb[2][perm], wb[3]]).astype(BF16)

    wr = jnp.concatenate([p["w_router"][l], jnp.zeros((D, LANES - N_EXPERTS), F32)], axis=-1)
    wr_hi = wr.astype(BF16)
    wr_lo = (wr - wr_hi.astype(F32)).astype(BF16)

    w = jnp.arange(GRID_W)
    col_start = jnp.clip(w - NA_KW // 2, 0, GRID_W - NA_KW)
    col_ok = (w[None, :] >= col_start[:, None]) & (w[None, :] < col_start[:, None] + NA_KW)
    dc_idx = jnp.clip(w[None, :] - w[:, None], 1 - NA_KW, NA_KW - 1) + NA_KW - 1
    t = jnp.where(col_ok[None, None], p["na_rpb"][l][:, :, dc_idx], NEG_INF)
    t = jnp.concatenate([t, jnp.full((NA_HEADS, 1, GRID_W, GRID_W), NEG_INF, F32)], axis=1)
    t = t.reshape(NA_HEADS * 16, GRID_W, GRID_W)
    zt = jnp.zeros_like(t)
    return dict(
        w_qkv=w_qkv, gqb=gqb, gk2=gk2, gmq=gmq, gmkv=gmkv, wqb=wqb, wkvb=wkvb,
        w_gates=w_in[:, 2144:].astype(BF16), w_branch=w_branch, w_out=p["w_out"][l].astype(BF16),
        ln1_g=p["ln1_g"][l][None], ln1_b=p["ln1_b"][l][None],
        ln2_g=p["ln2_g"][l][None], ln2_b=p["ln2_b"][l][None],
        wr_hi=wr_hi, wr_lo=wr_lo,
        w_gate=p["w_gate"][l].astype(BF16), w_up=p["w_up"][l].astype(BF16),
        w_down=p["w_down"][l].astype(BF16),
        tl=jnp.concatenate([t, zt], axis=-1), tr=jnp.concatenate([zt, t], axis=-1),
        sink=p["swa_sink"][l][np.array(HEAD_PERM)],
    )


def _rope_tables(n, ctx_len):
    pos = jnp.arange(n, dtype=jnp.int32)
    row = (pos // GRID_W).astype(F32)
    col = (pos % GRID_W).astype(F32)

    def axial(dim):
        half = dim // 4
        freqs = ROPE_THETA ** (-jnp.arange(half, dtype=F32) / half)
        parts_c, parts_s = [], []
        for pvec in (row, col):
            ang = pvec[:, None] * freqs[None, :]
            c, s = jnp.cos(ang), jnp.sin(ang)
            parts_c += [c, c]
            parts_s += [-s, s]
        return jnp.concatenate(parts_c, axis=-1), jnp.concatenate(parts_s, axis=-1)

    c64, s64 = axial(HEAD_DIM)
    cos = jnp.concatenate([c64, c64], axis=-1)
    sin = jnp.concatenate([s64, s64], axis=-1)
    c32, s32 = axial(MLA_ROPE)
    cosd = jnp.concatenate([c32, jnp.ones((n, LANES - MLA_ROPE), F32)], axis=-1)
    sind = jnp.concatenate([s32, jnp.zeros((n, LANES - MLA_ROPE), F32)], axis=-1)
    one = jnp.ones((ctx_len, LANES), F32)
    zero = jnp.zeros((ctx_len, LANES), F32)
    return (cos, sin, cosd, sind), (one, zero, one, zero)


def _expert_ffn(h2, aff, lw):
    B, n, D = h2.shape
    cap = CAPACITY * n // N_EXPERTS
    cap_pad = max(cap, TOKEN_BLOCK)
    slot_tile = TOKEN_BLOCK
    pos, off, tlo, thi = _select_call(aff, cap, slot_tile)
    xg, g = _gather_call(tlo, thi, pos, aff, h2, cap_pad, slot_tile)
    yw = _ffn_call(xg, g, lw)
    return _combine_call(off[..., 0].reshape(-1), pos, yw, n, D)


def kernel(x, c, ctx, c_ctx, w_mod, b_mod, w_in, na_rpb, gqa_q_norm, gqa_k_norm, swa_sink, mla_q_norm, mla_kv_norm, mla_w_qb, mla_w_kvb, w_branch, w_out, ln1_g, ln1_b, ln2_g, ln2_b, w_router, w_gate, w_up, w_down):
    p = dict(w_in=w_in, na_rpb=na_rpb, gqa_q_norm=gqa_q_norm, gqa_k_norm=gqa_k_norm, swa_sink=swa_sink,
             mla_q_norm=mla_q_norm, mla_kv_norm=mla_kv_norm, mla_w_qb=mla_w_qb, mla_w_kvb=mla_w_kvb,
             w_branch=w_branch, w_out=w_out, ln1_g=ln1_g, ln1_b=ln1_b, ln2_g=ln2_g, ln2_b=ln2_b,
             w_router=w_router, w_gate=w_gate, w_up=w_up, w_down=w_down)
    B, n, D = x.shape
    depth = w_in.shape[0]
    C = ctx.shape[1]
    alpha = (2 * depth) ** 0.25
    assert B + 1 <= 8
    cc = jnp.concatenate([c, c_ctx[None], jnp.zeros((8 - B - 1, D), F32)], axis=0)
    mod_all = _mod_call(cc, w_mod, b_mod)
    tabs, tabs_ctx = _rope_tables(n, C)
    xc = ctx
    for l in range(depth):
        lw = _prep_layer(l, p)
        need_ctx = l < depth - 1
        mods = [mod_all[l, :B, k * D:(k + 1) * D][:, None, :] for k in range(6)]
        mods_c = [jnp.broadcast_to(mod_all[l, B, k * D:(k + 1) * D][None, None, :], (B, 1, D)) for k in range(6)]
        pl_ = _inproj_call(x, mods[0], mods[1], lw, tabs)
        pc = _inproj_call(xc, mods_c[0], mods_c[1], lw, tabs_ctx)
        (naq, nak, nav, bq, bk, bv, cq, ck, cv, dq, dk, dv) = pl_
        (_, nakc, navc, _, bkc, bvc, _, ckc, cvc, _, dkc, dvc) = pc
        o_a = _na_call(naq, nak, nav, nakc, navc, lw["tl"], lw["tr"])
        o_b = _flash_call(bq, bk[:, None], bv, bkc[:, None], bvc, per_pair=False)
        o_c = _win_call(lw["sink"], cq, ck, cv, ckc, cvc)
        o_d = _flash_call(dq, dk, dv, dkc, dvc, per_pair=True)
        x1, h2, aff = _merge_call(x, (o_a, o_b, o_c, o_d), mods[:5], lw, alpha)
        y = _expert_ffn(h2, aff, lw)
        x = _ln2_call(x1, y, mods[5], lw["ln2_g"], lw["ln2_b"], alpha)
        if need_ctx:
            oc = _ctx_attn_call(lw["sink"], pc)
            xc1, hc2, affc = _merge_call(xc, oc, mods_c[:5], lw, alpha)
            yc = _expert_ffn(hc2, affc, lw)
            xc = _ln2_call(xc1, yc, mods_c[5], lw["ln2_g"], lw["ln2_b"], alpha)
    return x
```

```python
import functools

import numpy as np
import jax
import jax.numpy as jnp
from jax import lax
from jax.experimental import pallas as pl
from jax.experimental.pallas import tpu as pltpu

F32 = jnp.float32
BF16 = jnp.bfloat16

GRID_W = 64
HEAD_DIM = 64
ROPE_THETA = 10000.0
EPS = 1e-6
NEG_INF = -1e30
NA_HEADS = 4
NA_KH = 8
NA_KW = 16
NA_WIN_ROWS = 10
SWA_WINDOW = 128
QBLOCK = 128
MLA_HEADS = 4
MLA_NOPE = 64
MLA_ROPE = 32
MLA_V = 64
MLA_Q_LORA = 192
MLA_KV_LORA = 128
N_BRANCH = 4
BRANCH_W = 256
N_EXPERTS = 16
CAPACITY = 2
LANES = 128
TOKEN_BLOCK = 128
VMEM_LIMIT = 56 * 1024 * 1024
HEAD_PERM = (0, 2, 1, 3)
VT_ROWS = 80
LOG2E = 1.4426950408889634


def _cparams(*sem):
    return pltpu.CompilerParams(dimension_semantics=sem, vmem_limit_bytes=VMEM_LIMIT)


def _dot(a, b):
    return jnp.dot(a, b, preferred_element_type=F32)


def _dot_nt(a, b):
    return lax.dot_general(a, b, (((1,), (1,)), ((), ())), preferred_element_type=F32)


def _ln(x):
    mu = jnp.mean(x, axis=-1, keepdims=True)
    xc = x - mu
    var = jnp.mean(xc * xc, axis=-1, keepdims=True)
    return xc * lax.rsqrt(var + EPS)


def _modulate(x, shift, scale):
    return _ln(x) * (1.0 + scale) + shift


def _lane(shape, dim=None):
    return lax.broadcasted_iota(jnp.int32, shape, len(shape) - 1 if dim is None else dim)


def _rope(y, cos, sin_signed, half):
    w = y.shape[-1]
    low = (_lane((1, w)) & (2 * half - 1)) < half
    partner = jnp.where(low, pltpu.roll(y, w - half, 1), pltpu.roll(y, half, 1))
    return y * cos + partner * sin_signed


def _mod_kernel(c_ref, w_ref, b_ref, o_ref):
    c = c_ref[...]
    s = c / (1.0 + jnp.exp(-c))
    o_ref[0] = _dot(s.astype(BF16), w_ref[0].astype(BF16)) + b_ref[0]


def _mod_call(cc, w_mod, b_mod):
    L, D, N = w_mod.shape
    tn = N // 4
    return pl.pallas_call(
        _mod_kernel,
        grid=(L, N // tn),
        in_specs=[pl.BlockSpec((8, D), lambda l, j: (0, 0)),
                  pl.BlockSpec((1, D, tn), lambda l, j: (l, 0, j)),
                  pl.BlockSpec((1, 1, tn), lambda l, j: (l, 0, j))],
        out_specs=pl.BlockSpec((1, 8, tn), lambda l, j: (l, 0, j)),
        out_shape=jax.ShapeDtypeStruct((L, 8, N), F32),
        compiler_params=_cparams("arbitrary", "arbitrary"),
        name="mod_vectors",
    )(cc, w_mod, b_mod.reshape(L, 1, N))


_C_NA = 0
_C_BQ = 768
_C_BK = 1280
_C_BV = 1408
_C_CQ = 1536
_C_CK = 2048
_C_CV = 2176
_C_DQ = 2304
_C_DKV = 2560
_C_DKR = 2688
_C_END = 2816


def _inproj_kernel(x_ref, sh_ref, sc_ref, w_ref, cos_ref, sin_ref, cosd_ref, sind_ref,
                   gqb_ref, gk2_ref, gmq_ref, gmkv_ref, wqb_ref, wkvb_ref,
                   naq, nak, nav, bq, bk, bv, cq, ck, cv, dq, dk, dv):
    hb = _modulate(x_ref[0], sh_ref[0], sc_ref[0]).astype(BF16)

    def seg(a, b):
        return _dot(hb, w_ref[:, a:b])

    cos = cos_ref[...]
    sin = sin_ref[...]
    cosd = cosd_ref[...]
    sind = sind_ref[...]
    qscale = HEAD_DIM ** -0.5 * LOG2E

    def put_values_blocks(ref, v, nheads):
        vt = v.T
        ones = jnp.ones((VT_ROWS - HEAD_DIM, TOKEN_BLOCK), F32)
        for g in range(vt.shape[1] // TOKEN_BLOCK):
            for h in range(nheads):
                blk = vt[h * HEAD_DIM:(h + 1) * HEAD_DIM, g * TOKEN_BLOCK:(g + 1) * TOKEN_BLOCK]
                ref[0, g, h] = jnp.concatenate([blk, ones], axis=0).astype(BF16)

    def put_values_t(ref, pair, v):
        vt = v.T
        ones = jnp.ones((VT_ROWS - HEAD_DIM, vt.shape[1]), F32)
        for s in range(2):
            ref[0, pair, 0, s] = jnp.concatenate([vt[s * HEAD_DIM:(s + 1) * HEAD_DIM], ones], axis=0).astype(BF16)

    z = seg(_C_NA, _C_NA + 768)
    naq[0] = (z[:, 0:256] * qscale).astype(BF16)
    nak[0] = z[:, 256:512].astype(BF16)
    put_values_blocks(nav, z[:, 512:768], NA_HEADS)

    z = seg(_C_BQ, _C_BK)
    for s in range(4):
        zs = z[:, s * LANES:(s + 1) * LANES]
        ss = jnp.sum(zs * zs, axis=-1, keepdims=True) * (1.0 / HEAD_DIM)
        y = zs * lax.rsqrt(ss + EPS) * gqb_ref[s:s + 1, :]
        bq[0, s] = (_rope(y, cos, sin, 16) * qscale).astype(BF16)
    z = seg(_C_BK, _C_BV)
    left = _lane((1, LANES)) < HEAD_DIM
    z2 = z * z
    ss0 = jnp.sum(jnp.where(left, z2, 0.0), axis=-1, keepdims=True)
    ss1 = jnp.sum(jnp.where(left, 0.0, z2), axis=-1, keepdims=True)
    inv = jnp.where(left, lax.rsqrt(ss0 * (1.0 / HEAD_DIM) + EPS), lax.rsqrt(ss1 * (1.0 / HEAD_DIM) + EPS))
    bk[0] = _rope(z * inv * gk2_ref[...], cos, sin, 16).astype(BF16)
    put_values_t(bv, 0, seg(_C_BV, _C_CQ))

    z = seg(_C_CQ, _C_CK)
    for s in range(4):
        cq[0, s] = (_rope(z[:, s * LANES:(s + 1) * LANES], cos, sin, 16) * qscale).astype(BF16)
    ck[0] = _rope(seg(_C_CK, _C_CV), cos, sin, 16).astype(BF16)
    put_values_blocks(cv, seg(_C_CV, _C_DQ), 2)

    z = seg(_C_DQ, _C_DKV)
    ss = jnp.sum(z * z, axis=-1, keepdims=True) * (1.0 / MLA_Q_LORA)
    cqn = (z * lax.rsqrt(ss + EPS) * gmq_ref[...]).astype(BF16)
    qd = _dot(cqn, wqb_ref[...])
    dscale = (MLA_NOPE + MLA_ROPE) ** -0.5 * LOG2E
    for h in range(MLA_HEADS):
        nope = qd[:, h * 256:h * 256 + LANES] * dscale
        rot = _rope(qd[:, h * 256 + LANES:(h + 1) * 256], cosd, sind, 8) * dscale
        dq[0, h] = jnp.concatenate([nope, rot], axis=-1).astype(BF16)
    z = seg(_C_DKV, _C_DKR)
    ss = jnp.sum(z * z, axis=-1, keepdims=True) * (1.0 / MLA_KV_LORA)
    ckvn = (z * lax.rsqrt(ss + EPS) * gmkv_ref[...]).astype(BF16)
    kv = _dot(ckvn, wkvb_ref[...])
    kr = _rope(seg(_C_DKR, _C_END), cosd, sind, 8)
    for p in range(2):
        dk[0, p] = jnp.concatenate([kv[:, p * LANES:(p + 1) * LANES], kr], axis=-1).astype(BF16)
        put_values_t(dv, p, kv[:, 256 + p * LANES:256 + (p + 1) * LANES])


def _inproj_call(x, shift, scale, lw, tabs):
    B, n, D = x.shape
    tm = min(512, n)
    cos, sin, cosd, sind = tabs
    row = lambda b, i: (b, i, 0)
    hrow = lambda b, i: (b, 0, i, 0)
    const2 = lambda b, i: (0, 0)
    tab = pl.BlockSpec((tm, LANES), lambda b, i: (i, 0))
    mod = pl.BlockSpec((1, 1, D), lambda b, i: (b, 0, 0))
    nt = n // tm
    vt_spec = lambda pairs: ((B, pairs, nt, 2, VT_ROWS, tm),
                             pl.BlockSpec((1, pairs, 1, 2, VT_ROWS, tm), lambda b, i: (b, 0, i, 0, 0, 0)))
    slab = lambda kd: ((B, 4, n, kd), pl.BlockSpec((1, 4, tm, kd), hrow))
    tokm = lambda w: ((B, n, w), pl.BlockSpec((1, tm, w), row))
    vblk = lambda heads: ((B, n // TOKEN_BLOCK, heads, VT_ROWS, TOKEN_BLOCK),
                          pl.BlockSpec((1, tm // TOKEN_BLOCK, heads, VT_ROWS, TOKEN_BLOCK),
                                       lambda b, i: (b, i, 0, 0, 0)))
    outs = [tokm(256), tokm(256), vblk(NA_HEADS)]
    outs += [slab(LANES), tokm(LANES), vt_spec(1)]
    outs += [slab(LANES), tokm(LANES), vblk(2)]
    outs += [slab(256), ((B, 2, n, 256), pl.BlockSpec((1, 2, tm, 256), hrow)), vt_spec(2)]
    return pl.pallas_call(
        _inproj_kernel,
        grid=(B, n // tm),
        in_specs=[pl.BlockSpec((1, tm, D), row), mod, mod,
                  pl.BlockSpec((D, _C_END), const2), tab, tab, tab, tab,
                  pl.BlockSpec((4, LANES), const2), pl.BlockSpec((1, LANES), const2),
                  pl.BlockSpec((1, 256), const2), pl.BlockSpec((1, LANES), const2),
                  pl.BlockSpec((256, 1024), const2), pl.BlockSpec((LANES, 512), const2)],
        out_specs=[o[1] for o in outs],
        out_shape=[jax.ShapeDtypeStruct(o[0], BF16) for o in outs],
        compiler_params=_cparams("arbitrary", "arbitrary"),
        name="in_proj",
    )(x, shift, scale, lw["w_qkv"], cos, sin, cosd, sind,
      lw["gqb"], lw["gk2"], lw["gmq"], lw["gmkv"], lw["wqb"], lw["wkvb"])


def _flash_kernel(q_ref, k_ref, vt_ref, o_ref, acc_ref, m_ref, s_ref, *, tq, nk, per_trip):
    q = q_ref[0].reshape(2 * tq, q_ref.shape[-1])
    m_ref[...] = jnp.full(m_ref.shape, NEG_INF, F32)
    acc_ref[...] = jnp.zeros(acc_ref.shape, F32)

    def update(s, vt):
        m_prev = m_ref[...]
        m_new = jnp.maximum(m_prev, jnp.max(s, axis=0, keepdims=True))
        a = jnp.exp2(m_prev - m_new)
        p = jnp.exp2(s - m_new).astype(BF16)
        for h in range(2):
            cols = slice(h * tq, (h + 1) * tq)
            acc_ref[h] = a[:, cols] * acc_ref[h] + _dot(vt[h], p[:, cols])
        m_ref[...] = m_new

    s_ref[0] = _dot_nt(k_ref[0, 0, 0], q)

    def body(jj, carry):
        for u in range(per_trip):
            j = per_trip * jj + u
            s_ref[(u + 1) % 2] = _dot_nt(k_ref[0, 0, j + 1], q)
            update(s_ref[u % 2], vt_ref[0, 0, j])
        return carry

    lax.fori_loop(0, (nk - 1) // per_trip, body, 0)
    update(s_ref[0], vt_ref[0, 0, nk - 1])
    o = [acc_ref[h][:HEAD_DIM] / acc_ref[h][HEAD_DIM:HEAD_DIM + 1] for h in range(2)]
    o_ref[0] = jnp.concatenate(o, axis=0).T.astype(BF16)


def _flash_call(q, k, vt, kc, vct, *, per_pair):
    B, _, n, kd = q.shape
    P = k.shape[1]
    C = kc.shape[2]
    nk, tk = vt.shape[2], vt.shape[5]
    assert tk % C == 0 and nk % 2 == 0
    k_all = jnp.concatenate([k] + [kc] * (tk // C), axis=2).reshape(B, P, nk + 1, tk, kd)
    vct_pad = jnp.concatenate([vct, jnp.zeros(vct.shape[:-1] + (tk - C,), vct.dtype)], axis=-1)
    vt_all = jnp.concatenate([vt, vct_pad], axis=2)
    tq = min(512, n)
    pidx =(lambda b, p, i: (b, p, 0, 0, 0)) if per_pair else (lambda b, p, i: (b, 0, 0, 0, 0))
    vidx = (lambda b, p, i: (b, p, 0, 0, 0, 0)) if per_pair else (lambda b, p, i: (b, 0, 0, 0, 0, 0))
    kern = functools.partial(_flash_kernel, tq=tq, nk=nk + 1, per_trip=4 if nk % 4 == 0 else 2)
    return pl.pallas_call(
        kern,
        grid=(B, 2, n // tq),
        in_specs=[pl.BlockSpec((1, 2, tq, kd), lambda b, p, i: (b, p, i, 0)),
                  pl.BlockSpec((1, 1, nk + 1, tk, kd), pidx),
                  pl.BlockSpec((1, 1, nk + 1, 2, VT_ROWS, tk), vidx)],
        out_specs=pl.BlockSpec((1, tq, LANES), lambda b, p, i: (b, i, p)),
        out_shape=jax.ShapeDtypeStruct((B, n, 256), BF16),
        scratch_shapes=[pltpu.VMEM((2, VT_ROWS, tq), F32), pltpu.VMEM((1, 2 * tq), F32),
                        pltpu.VMEM((2, tk, 2 * tq), F32)],
        compiler_params=_cparams("arbitrary", "arbitrary", "arbitrary"),
        name="dense_attn",
    )(q, k_all, vt_all)


def _local_values(vt_ref, g0, blocks, vct_ref, head):
    lat = jnp.concatenate([vt_ref[0, g0 + g, head] for g in range(blocks)], axis=1)
    ctx = jnp.concatenate([vct_ref[0, g, head] for g in range(vct_ref.shape[1])], axis=1)
    return lat, ctx


def _na_kernel(q_ref, k_ref, vt_ref, kc_ref, vct_ref, tl_ref, tr_ref, o_ref, *, rows):
    i = pl.program_id(1)
    r0 = 2 * i
    w0 = jnp.clip(r0 - NA_KH // 2, 0, rows - NA_WIN_ROWS)
    win = NA_WIN_ROWS * GRID_W
    start = pl.multiple_of(w0 * GRID_W, 2 * GRID_W)
    k_all = jnp.concatenate([k_ref[0, pl.ds(start, win), :], kc_ref[0]], axis=0)
    q = q_ref[0]
    head_of_lane = _lane((1, 256)) >> 6
    q_stack = jnp.concatenate([jnp.where(head_of_lane == h, q, jnp.zeros_like(q)) for h in range(NA_HEADS)], axis=0)
    s = _dot_nt(k_all, q_stack)

    def table_index(a, j):
        qr = r0 + a
        kr = w0 + j
        st = jnp.clip(qr - NA_KH // 2, 0, rows - NA_KH)
        ok = (kr >= st) & (kr < st + NA_KH)
        return jnp.where(ok, kr - qr + NA_KH - 1, 2 * NA_KH - 1)

    idx = [[table_index(a, j) for j in range(NA_WIN_ROWS)] for a in range(2)]
    bias = jnp.concatenate(
        [jnp.concatenate([tl_ref[h * 16 + idx[0][j]] + tr_ref[h * 16 + idx[1][j]] for j in range(NA_WIN_ROWS)], axis=0)
         for h in range(NA_HEADS)], axis=1)
    s_lat = s[:win] + bias
    s_ctx = s[win:]
    m = jnp.maximum(jnp.max(s_lat, axis=0, keepdims=True), jnp.max(s_ctx, axis=0, keepdims=True))
    p_lat = jnp.exp2(s_lat - m).astype(BF16)
    p_ctx = jnp.exp2(s_ctx - m).astype(BF16)
    outs = []
    for h in range(NA_HEADS):
        cols = slice(h * LANES, (h + 1) * LANES)
        v_lat, v_ctx = _local_values(vt_ref, w0 >> 1, win // TOKEN_BLOCK, vct_ref, h)
        o = _dot(v_lat, p_lat[:, cols]) + _dot(v_ctx, p_ctx[:, cols])
        outs.append(o[:HEAD_DIM] / o[HEAD_DIM:HEAD_DIM + 1])
    o_ref[0] = jnp.concatenate(outs, axis=0).T.astype(BF16)


def _na_call(q, k, vt, kc, vct, tl, tr):
    B, n, _ = q.shape
    C = kc.shape[1]
    rows = n // GRID_W
    assert rows >= NA_WIN_ROWS and rows % 2 == 0
    full = lambda b, i: (b, 0, 0)
    full5 = lambda b, i: (b, 0, 0, 0, 0)
    return pl.pallas_call(
        functools.partial(_na_kernel, rows=rows),
        grid=(B, rows // 2),
        in_specs=[pl.BlockSpec((1, 2 * GRID_W, 256), lambda b, i: (b, i, 0)),
                  pl.BlockSpec((1, n, 256), full), pl.BlockSpec((1,) + vt.shape[1:], full5),
                  pl.BlockSpec((1, C, 256), full), pl.BlockSpec((1,) + vct.shape[1:], full5),
                  pl.BlockSpec(tl.shape, lambda b, i: (0, 0, 0)),
                  pl.BlockSpec(tr.shape, lambda b, i: (0, 0, 0))],
        out_specs=pl.BlockSpec((1, 2 * GRID_W, 256), lambda b, i: (b, i, 0)),
        out_shape=jax.ShapeDtypeStruct((B, n, 256), BF16),
        compiler_params=_cparams("arbitrary", "arbitrary"),
        name="nbr_attn",
    )(q, k, vt, kc, vct, tl, tr)


def _win_kernel(sink_ref, q_ref, k_ref, vt_ref, kc_ref, vct_ref, o_ref, *, n):
    i = pl.program_id(1)
    band = 3 * QBLOCK
    ws = pl.multiple_of(jnp.clip((i - 1) * QBLOCK, 0, n - band), QBLOCK)
    k_all = jnp.concatenate([k_ref[0, pl.ds(ws, band), :], kc_ref[0]], axis=0)
    q_stack = q_ref[0].reshape(4 * QBLOCK, LANES)
    s = _dot_nt(k_all, q_stack)
    kpos = ws + lax.broadcasted_iota(jnp.int32, (band, QBLOCK), 0)
    qpos = i * QBLOCK + lax.broadcasted_iota(jnp.int32, (band, QBLOCK), 1)
    ok = jnp.abs(qpos - kpos) <= SWA_WINDOW
    s_lat = jnp.concatenate([jnp.where(ok, s[:band, j * QBLOCK:(j + 1) * QBLOCK], NEG_INF) for j in range(4)], axis=1)
    s_ctx = s[band:]
    snk = jnp.concatenate([jnp.full((1, QBLOCK), sink_ref[j], F32) for j in range(4)], axis=1)
    m = jnp.maximum(jnp.maximum(jnp.max(s_lat, axis=0, keepdims=True), jnp.max(s_ctx, axis=0, keepdims=True)), snk)
    p_lat = jnp.exp2(s_lat - m).astype(BF16)
    p_ctx = jnp.exp2(s_ctx - m).astype(BF16)
    p_snk = jnp.exp2(snk - m)
    g0 = ws >> 7
    outs = []
    for j in range(4):
        cols = slice(j * QBLOCK, (j + 1) * QBLOCK)
        v_lat, v_ctx = _local_values(vt_ref, g0, band // TOKEN_BLOCK, vct_ref, HEAD_PERM[j] // 2)
        o = _dot(v_lat, p_lat[:, cols]) + _dot(v_ctx, p_ctx[:, cols])
        outs.append(o[:HEAD_DIM] / (o[HEAD_DIM:HEAD_DIM + 1] + p_snk[:, cols]))
    o_ref[0] = jnp.concatenate(outs, axis=0).T.astype(BF16)


def _win_call(sink, q, k, vt, kc, vct):
    B, _, n, _ = q.shape
    C = kc.shape[1]
    assert n >= 3 * QBLOCK and QBLOCK == TOKEN_BLOCK
    full = lambda b, i, s: (b, 0, 0)
    full5 = lambda b, i, s: (b, 0, 0, 0, 0)
    return pl.pallas_call(
        functools.partial(_win_kernel, n=n),
        grid_spec=pltpu.PrefetchScalarGridSpec(
            num_scalar_prefetch=1,
            grid=(B, n // QBLOCK),
            in_specs=[pl.BlockSpec((1, 4, QBLOCK, LANES), lambda b, i, s: (b, 0, i, 0)),
                      pl.BlockSpec((1, n, LANES), full), pl.BlockSpec((1,) + vt.shape[1:], full5),
                      pl.BlockSpec((1, C, LANES), full), pl.BlockSpec((1,) + vct.shape[1:], full5)],
            out_specs=pl.BlockSpec((1, QBLOCK, 256), lambda b, i, s: (b, i, 0))),
        out_shape=jax.ShapeDtypeStruct((B, n, 256), BF16),
        compiler_params=_cparams("arbitrary", "arbitrary"),
        name="window_attn",
    )(sink, q, k, vt, kc, vct)


def _attend(q, k, vt, sink=None):
    s = _dot_nt(q, k)
    m = jnp.max(s, axis=-1, keepdims=True)
    if sink is not None:
        m = jnp.maximum(m, sink)
    p = jnp.exp2(s - m)
    l = jnp.sum(p, axis=-1, keepdims=True)
    if sink is not None:
        l = l + jnp.exp2(sink - m)
    return _dot_nt(p.astype(BF16), vt) / l


def _ctx_attn_kernel(sink_ref, naq, nak, nav, bq, bk, bv, cq, ck, cv, dq, dk, dv, oa, ob, oc, od):
    left = _lane((1, LANES)) < HEAD_DIM
    head_of_lane = _lane((1, 256)) >> 6

    def pair_out(o):
        return jnp.concatenate([jnp.where(left, o[0], o[1]), jnp.where(left, o[2], o[3])], axis=-1).astype(BF16)

    def blocks_t(ref, heads):
        return jnp.concatenate(
            [jnp.concatenate([ref[0, g, h][:HEAD_DIM] for g in range(ref.shape[1])], axis=1)
             for h in range(heads)], axis=0)

    def both_heads(vt):
        return jnp.concatenate([vt[0, :HEAD_DIM], vt[1, :HEAD_DIM]], axis=0)

    q = naq[0]
    v_na = blocks_t(nav, NA_HEADS)
    out = jnp.zeros(q.shape, F32)
    for h in range(NA_HEADS):
        qm = jnp.where(head_of_lane == h, q, jnp.zeros_like(q))
        out = jnp.where(head_of_lane == h, _attend(qm, nak[0], v_na), out)
    oa[0] = out.astype(BF16)
    ob[0] = pair_out([_attend(bq[0, s], bk[0], both_heads(bv[0, 0, 0])) for s in range(4)])
    v_c = blocks_t(cv, 2)
    oc[0] = pair_out([_attend(cq[0, s], ck[0], v_c, sink_ref[s]) for s in range(4)])
    od[0] = pair_out([_attend(dq[0, h], dk[0, h // 2], both_heads(dv[0, h // 2, 0])) for h in range(MLA_HEADS)])


def _ctx_attn_call(sink, pc):
    B, C, _ = pc[0].shape
    names = pc
    specs = []
    for a in names:
        nd = a.ndim
        specs.append(pl.BlockSpec((1,) + a.shape[1:], (lambda b, s, nd=nd: (b,) + (0,) * (nd - 1))))
    out_spec = pl.BlockSpec((1, C, 256), lambda b, s: (b, 0, 0))
    return pl.pallas_call(
        _ctx_attn_kernel,
        grid_spec=pltpu.PrefetchScalarGridSpec(
            num_scalar_prefetch=1, grid=(B,), in_specs=specs, out_specs=[out_spec] * 4),
        out_shape=[jax.ShapeDtypeStruct((B, C, 256), BF16)] * 4,
        compiler_params=_cparams("arbitrary"),
        name="ctx_attn",
    )(sink, *pc)


def _merge_kernel(x_ref, oa, ob, oc, od, sha, sca, ga, shf, scf, wg_ref, wb_ref, wo_ref,
                  l1g, l1b, wrh_ref, wrl_ref, x1_ref, h2_ref, aff_ref, *, alpha):
    x = x_ref[0]
    D = x.shape[-1]
    hb = _modulate(x, sha[0], sca[0]).astype(BF16)
    merged = None
    for i, o in enumerate((oa, ob, oc, od)):
        g = 1.0 / (1.0 + jnp.exp(-_dot(hb, wg_ref[:, i * D:(i + 1) * D])))
        term = g * _dot(o[0], wb_ref[i])
        merged = term if merged is None else merged + term
    y = _dot(merged.astype(BF16), wo_ref[...])
    x1 = _ln(alpha * x + ga[0] * y) * l1g[...] + l1b[...]
    x1_ref[0] = x1
    h2 = _modulate(x1, shf[0], scf[0])
    h2_hi = h2.astype(BF16)
    h2_ref[0] = h2_hi
    h2_lo = (h2 - h2_hi.astype(F32)).astype(BF16)
    logits = _dot(h2_hi, wrh_ref[...]) + _dot(h2_hi, wrl_ref[...]) + _dot(h2_lo, wrh_ref[...])
    logits = jnp.where(_lane((1, LANES)) < N_EXPERTS, logits, NEG_INF)
    e = jnp.exp(logits - jnp.max(logits, axis=-1, keepdims=True))
    aff_t = (e / jnp.sum(e, axis=-1, keepdims=True)).T
    for k in range(aff_ref.shape[1]):
        aff_ref[0, k] = aff_t[:N_EXPERTS, k * TOKEN_BLOCK:(k + 1) * TOKEN_BLOCK]


def _merge_call(x, outs, mods, lw, alpha):
    B, n, D = x.shape
    tm = min(256, n)
    nb = tm // TOKEN_BLOCK
    row = lambda b, i: (b, i, 0)
    mod = pl.BlockSpec((1, 1, D), lambda b, i: (b, 0, 0))
    c2 = lambda b, i: (0, 0)
    obr = pl.BlockSpec((1, tm, 256), row)
    return pl.pallas_call(
        functools.partial(_merge_kernel, alpha=alpha),
        grid=(B, n // tm),
        in_specs=[pl.BlockSpec((1, tm, D), row), obr, obr, obr, obr, mod, mod, mod, mod, mod,
                  pl.BlockSpec((D, N_BRANCH * D), c2),
                  pl.BlockSpec((N_BRANCH, BRANCH_W, D), lambda b, i: (0, 0, 0)),
                  pl.BlockSpec((D, D), c2), pl.BlockSpec((1, D), c2), pl.BlockSpec((1, D), c2),
                  pl.BlockSpec((D, LANES), c2), pl.BlockSpec((D, LANES), c2)],
        out_specs=[pl.BlockSpec((1, tm, D), row), pl.BlockSpec((1, tm, D), row),
                   pl.BlockSpec((1, nb, N_EXPERTS, TOKEN_BLOCK), lambda b, i: (b, i, 0, 0))],
        out_shape=[jax.ShapeDtypeStruct((B, n, D), F32), jax.ShapeDtypeStruct((B, n, D), BF16),
                   jax.ShapeDtypeStruct((B, n // TOKEN_BLOCK, N_EXPERTS, TOKEN_BLOCK), F32)],
        compiler_params=_cparams("arbitrary", "arbitrary"),
        name="merge_router",
    )(x, *outs, *mods, lw["w_gates"], lw["w_branch"], lw["w_out"], lw["ln1_g"], lw["ln1_b"],
      lw["wr_hi"], lw["wr_lo"])


def _select_kernel(aff_ref, pos_ref, off_ref, tlo_ref, thi_ref, *, cap, slot_tile):
    a = aff_ref[0]
    nb = a.shape[0]
    rows = nb * N_EXPERTS
    bits = lax.bitcast_convert_type(a, jnp.int32)
    capf = jnp.float32(cap)

    def count(mask):
        c = jnp.sum(jnp.where(mask, 1.0, 0.0), axis=0)
        return jnp.broadcast_to(jnp.sum(c, axis=-1, keepdims=True), c.shape)

    def search(it, lo):
        cand = lo | lax.shift_left(jnp.int32(1), 30 - it)
        return jnp.where(count(bits >= cand[None]) >= capf, cand, lo)

    thr = lax.fori_loop(0, 31, search, jnp.zeros((N_EXPERTS, TOKEN_BLOCK), jnp.int32))

    r = lax.broadcasted_iota(jnp.int32, (rows, rows), 0)
    c = lax.broadcasted_iota(jnp.int32, (rows, rows), 1)
    earlier = jnp.where(((r & (N_EXPERTS - 1)) == (c & (N_EXPERTS - 1))) & ((c >> 4) < (r >> 4)), 1.0, 0.0).astype(BF16)
    ti = lax.broadcasted_iota(jnp.int32, (TOKEN_BLOCK, TOKEN_BLOCK), 0)
    tj = lax.broadcasted_iota(jnp.int32, (TOKEN_BLOCK, TOKEN_BLOCK), 1)
    tri = jnp.where(ti <= tj, 1.0, 0.0).astype(BF16)
    ones = jnp.ones((TOKEN_BLOCK, TOKEN_BLOCK), BF16)

    def prefix(mask):
        m2 = jnp.where(mask, 1.0, 0.0).reshape(rows, TOKEN_BLOCK)
        mb = m2.astype(BF16)
        within = _dot(mb, tri)
        tot = _dot(mb, ones)
        off = _dot(earlier, tot.astype(BF16))
        shp = (nb, N_EXPERTS, TOKEN_BLOCK)
        return (off + within - m2).reshape(shp), off.reshape(shp), tot.reshape(shp)

    gt = bits > thr[None]
    eq = bits == thr[None]
    need = capf - count(gt)
    eq_rank, _, _ = prefix(eq)
    sel = gt | (eq & (eq_rank < need[None]))
    excl, off, tot = prefix(sel)
    pos_ref[0] = jnp.where(sel, excl, -1.0)
    off_ref[0] = off.astype(jnp.int32)
    tile_start = (_lane((1, 1, TOKEN_BLOCK)) * slot_tile).astype(F32)
    tlo_ref[0] = jnp.sum(jnp.where(off + tot <= tile_start, 1, 0), axis=0).astype(jnp.int32)
    thi_ref[0] = jnp.sum(jnp.where(off < tile_start + slot_tile, 1, 0), axis=0).astype(jnp.int32)


def _select_call(aff, cap, slot_tile):
    B, nb, E, _ = aff.shape
    blk = pl.BlockSpec((1, nb, E, TOKEN_BLOCK), lambda b: (b, 0, 0, 0))
    rng = pl.BlockSpec((1, E, TOKEN_BLOCK), lambda b: (b, 0, 0))
    return pl.pallas_call(
        functools.partial(_select_kernel, cap=cap, slot_tile=slot_tile),
        grid=(B,),
        in_specs=[blk],
        out_specs=[blk, blk, rng, rng],
        out_shape=[jax.ShapeDtypeStruct(aff.shape, F32), jax.ShapeDtypeStruct(aff.shape, jnp.int32),
                   jax.ShapeDtypeStruct((B, E, TOKEN_BLOCK), jnp.int32),
                   jax.ShapeDtypeStruct((B, E, TOKEN_BLOCK), jnp.int32)],
        compiler_params=_cparams("arbitrary"),
        name="expert_select",
    )(aff)


def _gather_kernel(tlo_ref, thi_ref, pos_ref, aff_ref, h_ref, xg_ref, g_ref, acc_ref, gacc_ref, *, slot_tile):
    b = pl.program_id(0)
    e = pl.program_id(1)
    n_tiles = xg_ref.shape[2] // slot_tile
    for t in range(n_tiles):
        acc_ref[...] = jnp.zeros(acc_ref.shape, F32)
        gacc_ref[...] = jnp.zeros(gacc_ref.shape, F32)
        slot = (t * slot_tile + lax.broadcasted_iota(jnp.int32, (slot_tile, 2 * TOKEN_BLOCK), 0)).astype(F32)

        def body(pb, carry):
            p = jnp.concatenate([pos_ref[0, 2 * pb, pl.ds(e, 1), :], pos_ref[0, 2 * pb + 1, pl.ds(e, 1), :]], axis=-1)
            a = jnp.concatenate([aff_ref[0, 2 * pb, pl.ds(e, 1), :], aff_ref[0, 2 * pb + 1, pl.ds(e, 1), :]], axis=-1)
            hit = p == slot
            tok = pl.multiple_of(pb * (2 * TOKEN_BLOCK), 2 * TOKEN_BLOCK)
            acc_ref[...] += _dot(jnp.where(hit, 1.0, 0.0).astype(BF16), h_ref[0, pl.ds(tok, 2 * TOKEN_BLOCK), :])
            ga = jnp.where(hit, a, 0.0)
            gacc_ref[...] += ga[:, :TOKEN_BLOCK] + ga[:, TOKEN_BLOCK:]
            return carry

        lax.fori_loop(tlo_ref[b, e, t] >> 1, (thi_ref[b, e, t] + 1) >> 1, body, 0)
        xg_ref[0, 0, t * slot_tile:(t + 1) * slot_tile, :] = acc_ref[...].astype(BF16)
        g_ref[0, 0, t * slot_tile:(t + 1) * slot_tile, :] = jnp.sum(gacc_ref[...], axis=-1, keepdims=True)


def _gather_call(tlo, thi, pos, aff, h2, cap_pad, slot_tile):
    B, n, D = h2.shape
    nb = n // TOKEN_BLOCK
    blk = pl.BlockSpec((1, nb, N_EXPERTS, TOKEN_BLOCK), lambda b, e, *_: (b, 0, 0, 0))
    return pl.pallas_call(
        functools.partial(_gather_kernel, slot_tile=slot_tile),
        grid_spec=pltpu.PrefetchScalarGridSpec(
            num_scalar_prefetch=2,
            grid=(B, N_EXPERTS),
            in_specs=[blk, blk, pl.BlockSpec((1, n, D), lambda b, e, *_: (b, 0, 0))],
            out_specs=[pl.BlockSpec((1, 1, cap_pad, D), lambda b, e, *_: (b, e, 0, 0)),
                       pl.BlockSpec((1, 1, cap_pad, 1), lambda b, e, *_: (b, e, 0, 0))],
            scratch_shapes=[pltpu.VMEM((slot_tile, D), F32), pltpu.VMEM((slot_tile, TOKEN_BLOCK), F32)]),
        out_shape=[jax.ShapeDtypeStruct((B, N_EXPERTS, cap_pad, D), BF16),
                   jax.ShapeDtypeStruct((B, N_EXPERTS, cap_pad, 1), F32)],
        compiler_params=_cparams("arbitrary", "arbitrary"),
        name="expert_gather",
    )(tlo, thi, pos, aff, h2)


def _ffn_kernel(xg_ref, g_ref, wg_ref, wu_ref, wd_ref, y_ref):
    xg = xg_ref[0, 0]
    a = _dot(xg, wg_ref[0])
    u = _dot(xg, wu_ref[0])
    hmid = (a / (1.0 + jnp.exp(-a)) * u).astype(BF16)
    y_ref[0, 0] = (_dot(hmid, wd_ref[0]) * g_ref[0, 0]).astype(BF16)


def _ffn_call(xg, g, lw):
    B, E, cp, D = xg.shape
    F = lw["w_gate"].shape[-1]
    tok = lambda e, b: (b, e, 0, 0)
    wsp = lambda e, b: (e, 0, 0)
    return pl.pallas_call(
        _ffn_kernel,
        grid=(E, B),
        in_specs=[pl.BlockSpec((1, 1, cp, D), tok), pl.BlockSpec((1, 1, cp, 1), tok),
                  pl.BlockSpec((1, D, F), wsp), pl.BlockSpec((1, D, F), wsp), pl.BlockSpec((1, F, D), wsp)],
        out_specs=pl.BlockSpec((1, 1, cp, D), tok),
        out_shape=jax.ShapeDtypeStruct((B, E, cp, D), BF16),
        compiler_params=_cparams("arbitrary", "arbitrary"),
        name="expert_mlp",
    )(xg, g, lw["w_gate"], lw["w_up"], lw["w_down"])


def _combine_kernel(off_ref, pos_ref, yw_ref, y_ref, *, window, blocks_per_step):
    b = pl.program_id(0)
    t = pl.program_id(2)
    cap_pad = yw_ref.shape[2]
    nb_total = pl.num_programs(2) * blocks_per_step
    for k in range(blocks_per_step):
        blk = t * blocks_per_step + k
        acc = jnp.zeros((TOKEN_BLOCK, yw_ref.shape[-1]), F32)
        for e in range(N_EXPERTS):
            start = off_ref[(b * nb_total + blk) * N_EXPERTS + e]
            w0 = pl.multiple_of(jnp.minimum(start & -16, cap_pad - window), 16)
            p = pos_ref[0, k, e:e + 1, :]
            slot = (w0 + lax.broadcasted_iota(jnp.int32, (window, TOKEN_BLOCK), 0)).astype(F32)
            hit = jnp.where(p == slot, 1.0, 0.0).T.astype(BF16)
            acc = acc + _dot(hit, yw_ref[0, e, pl.ds(w0, window), :])
        y_ref[0, k * TOKEN_BLOCK:(k + 1) * TOKEN_BLOCK, :] = acc


def _combine_call(off_flat, pos, yw, n, D):
    B, E, cap_pad, _ = yw.shape
    nb = n // TOKEN_BLOCK
    window = min(256, cap_pad)
    dh = min(512, D)
    bps = min(4, nb)
    return pl.pallas_call(
        functools.partial(_combine_kernel, window=window, blocks_per_step=bps),
        grid_spec=pltpu.PrefetchScalarGridSpec(
            num_scalar_prefetch=1,
            grid=(B, D // dh, nb // bps),
            in_specs=[pl.BlockSpec((1, bps, E, TOKEN_BLOCK), lambda b, d, t, *_: (b, t, 0, 0)),
                      pl.BlockSpec((1, E, cap_pad, dh), lambda b, d, t, *_: (b, 0, 0, d))],
            out_specs=pl.BlockSpec((1, bps * TOKEN_BLOCK, dh), lambda b, d, t, *_: (b, t, d))),
        out_shape=jax.ShapeDtypeStruct((B, n, D), F32),
        compiler_params=_cparams("arbitrary", "arbitrary", "arbitrary"),
        name="expert_combine",
    )(off_flat, pos, yw)


def _ln2_kernel(x_ref, y_ref, g_ref, lg, lb, o_ref, *, alpha):
    o_ref[0] = _ln(alpha * x_ref[0] + g_ref[0] * y_ref[0]) * lg[...] + lb[...]


def _ln2_call(x1, y, gf, lg, lb, alpha):
    B, n, D = x1.shape
    tm = min(512, n)
    row = pl.BlockSpec((1, tm, D), lambda b, i: (b, i, 0))
    vec = pl.BlockSpec((1, D), lambda b, i: (0, 0))
    return pl.pallas_call(
        functools.partial(_ln2_kernel, alpha=alpha),
        grid=(B, n // tm),
        in_specs=[row, row, pl.BlockSpec((1, 1, D), lambda b, i: (b, 0, 0)), vec, vec],
        out_specs=row,
        out_shape=jax.ShapeDtypeStruct((B, n, D), F32),
        compiler_params=_cparams("arbitrary", "arbitrary"),
        name="ffn_residual_ln",
    )(x1, y, gf, lg, lb)


def _slot_pad(w, slot):
    z = jnp.zeros_like(w)
    return jnp.concatenate([w, z] if slot == 0 else [z, w], axis=-1)


def _prep_layer(l, p):
    w_in = p["w_in"][l]
    D = w_in.shape[0]
    hd = HEAD_DIM

    def gq_cols(base):
        q = [_slot_pad(w_in[:, base + h * hd: base + (h + 1) * hd], h // 2) for h in HEAD_PERM]
        return q + [w_in[:, base + 256: base + 512]]

    zc = lambda k: jnp.zeros((D, k), F32)
    cols = [w_in[:, 0:768]] + gq_cols(768) + gq_cols(1280)
    cols += [w_in[:, 1792:1984], zc(64), w_in[:, 1984:2112], w_in[:, 2112:2144], zc(96)]
    w_qkv = jnp.concatenate(cols, axis=-1).astype(BF16)
    assert w_qkv.shape[1] == _C_END

    gq = p["gqa_q_norm"][l]
    gk = p["gqa_k_norm"][l]
    zg = jnp.zeros_like(gq)
    gqb = jnp.stack([jnp.concatenate([gq, zg] if h // 2 == 0 else [zg, gq]) for h in HEAD_PERM])
    gk2 = jnp.concatenate([gk, gk])[None]
    gmq = jnp.concatenate([p["mla_q_norm"][l], jnp.zeros((64,), F32)])[None]
    gmkv = p["mla_kv_norm"][l][None]

    wq = p["mla_w_qb"][l]
    qcols = []
    for h in range(MLA_HEADS):
        nope = wq[:, h * 96: h * 96 + 64]
        rot = wq[:, h * 96 + 64: (h + 1) * 96]
        qcols += [_slot_pad(nope, h % 2), rot, jnp.zeros((MLA_Q_LORA, 96), F32)]
    wqb = jnp.concatenate(qcols, axis=-1)
    wqb = jnp.concatenate([wqb, jnp.zeros((64, wqb.shape[1]), F32)], axis=0).astype(BF16)
    wkv = p["mla_w_kvb"][l]
    wkvb = jnp.concatenate([wkv[:, h * 128: h * 128 + 64] for h in range(MLA_HEADS)]
                           + [wkv[:, h * 128 + 64: (h + 1) * 128] for h in range(MLA_HEADS)],
                           axis=-1).astype(BF16)

    wb = p["w_branch"][l]
    perm = np.concatenate([np.arange(h * hd, (h + 1) * hd) for h in HEAD_PERM])
    w_branch = jnp.stack([wb[0], wb[1][perm], wb[2][perm], wb[3]]).astype(BF16)

    wr = jnp.concatenate([p["w_router"][l], jnp.zeros((D, LANES - N_EXPERTS), F32)], axis=-1)
    wr_hi = wr.astype(BF16)
    wr_lo = (wr - wr_hi.astype(F32)).astype(BF16)

    w = jnp.arange(GRID_W)
    col_start = jnp.clip(w - NA_KW // 2, 0, GRID_W - NA_KW)
    col_ok = (w[None, :] >= col_start[:, None]) & (w[None, :] < col_start[:, None] + NA_KW)
    dc_idx = jnp.clip(w[None, :] - w[:, None], 1 - NA_KW, NA_KW - 1) + NA_KW - 1
    t = jnp.where(col_ok[None, None], p["na_rpb"][l][:, :, dc_idx], NEG_INF)
    t = jnp.concatenate([t, jnp.full((NA_HEADS, 1, GRID_W, GRID_W), NEG_INF, F32)], axis=1)
    t = (t * LOG2E).transpose(0, 1, 3, 2).reshape(NA_HEADS * 16, GRID_W, GRID_W)
    zt = jnp.zeros_like(t)
    return dict(
        w_qkv=w_qkv, gqb=gqb, gk2=gk2, gmq=gmq, gmkv=gmkv, wqb=wqb, wkvb=wkvb,
        w_gates=w_in[:, 2144:].astype(BF16), w_branch=w_branch, w_out=p["w_out"][l].astype(BF16),
        ln1_g=p["ln1_g"][l][None], ln1_b=p["ln1_b"][l][None],
        ln2_g=p["ln2_g"][l][None], ln2_b=p["ln2_b"][l][None],
        wr_hi=wr_hi, wr_lo=wr_lo,
        w_gate=p["w_gate"][l].astype(BF16), w_up=p["w_up"][l].astype(BF16),
        w_down=p["w_down"][l].astype(BF16),
        tl=jnp.concatenate([t, zt], axis=-1), tr=jnp.concatenate([zt, t], axis=-1),
        sink=p["swa_sink"][l][np.array(HEAD_PERM)] * LOG2E,
    )


def _rope_tables(n, ctx_len):
    pos = jnp.arange(n, dtype=jnp.int32)
    row = (pos // GRID_W).astype(F32)
    col = (pos % GRID_W).astype(F32)

    def axial(dim):
        half = dim // 4
        freqs = ROPE_THETA ** (-jnp.arange(half, dtype=F32) / half)
        parts_c, parts_s = [], []
        for pvec in (row, col):
            ang = pvec[:, None] * freqs[None, :]
            c, s = jnp.cos(ang), jnp.sin(ang)
            parts_c += [c, c]
            parts_s += [-s, s]
        return jnp.concatenate(parts_c, axis=-1), jnp.concatenate(parts_s, axis=-1)

    c64, s64 = axial(HEAD_DIM)
    cos = jnp.concatenate([c64, c64], axis=-1)
    sin = jnp.concatenate([s64, s64], axis=-1)
    c32, s32 = axial(MLA_ROPE)
    cosd = jnp.concatenate([c32, jnp.ones((n, LANES - MLA_ROPE), F32)], axis=-1)
    sind = jnp.concatenate([s32, jnp.zeros((n, LANES - MLA_ROPE), F32)], axis=-1)
    one = jnp.ones((ctx_len, LANES), F32)
    zero = jnp.zeros((ctx_len, LANES), F32)
    return (cos, sin, cosd, sind), (one, zero, one, zero)


def _expert_ffn(h2, aff, lw):
    B, n, D = h2.shape
    cap = CAPACITY * n // N_EXPERTS
    cap_pad = max(cap, TOKEN_BLOCK)
    slot_tile = TOKEN_BLOCK
    pos, off, tlo, thi = _select_call(aff, cap, slot_tile)
    xg, g = _gather_call(tlo, thi, pos, aff, h2, cap_pad, slot_tile)
    yw = _ffn_call(xg, g, lw)
    return _combine_call(off[..., 0].reshape(-1), pos, yw, n, D)


def kernel(x, c, ctx, c_ctx, w_mod, b_mod, w_in, na_rpb, gqa_q_norm, gqa_k_norm, swa_sink, mla_q_norm, mla_kv_norm, mla_w_qb, mla_w_kvb, w_branch, w_out, ln1_g, ln1_b, ln2_g, ln2_b, w_router, w_gate, w_up, w_down):
    p = dict(w_in=w_in, na_rpb=na_rpb, gqa_q_norm=gqa_q_norm, gqa_k_norm=gqa_k_norm, swa_sink=swa_sink,
             mla_q_norm=mla_q_norm, mla_kv_norm=mla_kv_norm, mla_w_qb=mla_w_qb, mla_w_kvb=mla_w_kvb,
             w_branch=w_branch, w_out=w_out, ln1_g=ln1_g, ln1_b=ln1_b, ln2_g=ln2_g, ln2_b=ln2_b,
             w_router=w_router, w_gate=w_gate, w_up=w_up, w_down=w_down)
    B, n, D = x.shape
    depth = w_in.shape[0]
    C = ctx.shape[1]
    alpha = (2 * depth) ** 0.25
    assert B + 1 <= 8
    cc = jnp.concatenate([c, c_ctx[None], jnp.zeros((8 - B - 1, D), F32)], axis=0)
    mod_all = _mod_call(cc, w_mod, b_mod)
    tabs, tabs_ctx = _rope_tables(n, C)
    xc = ctx
    for l in range(depth):
        lw = _prep_layer(l, p)
        need_ctx = l < depth - 1
        mods = [mod_all[l, :B, k * D:(k + 1) * D][:, None, :] for k in range(6)]
        mods_c = [jnp.broadcast_to(mod_all[l, B, k * D:(k + 1) * D][None, None, :], (B, 1, D)) for k in range(6)]
        pl_ = _inproj_call(x, mods[0], mods[1], lw, tabs)
        pc = _inproj_call(xc, mods_c[0], mods_c[1], lw, tabs_ctx)
        (naq, nak, nav, bq, bk, bv, cq, ck, cv, dq, dk, dv) = pl_
        (_, nakc, navc, _, bkc, bvc, _, ckc, cvc, _, dkc, dvc) = pc
        o_a = _na_call(naq, nak, nav, nakc, navc, lw["tl"], lw["tr"])
        o_b = _flash_call(bq, bk[:, None], bv, bkc[:, None], bvc, per_pair=False)
        o_c = _win_call(lw["sink"], cq, ck, cv, ckc, cvc)
        o_d = _flash_call(dq, dk, dv, dkc, dvc, per_pair=True)
        x1, h2, aff = _merge_call(x, (o_a, o_b, o_c, o_d), mods[:5], lw, alpha)
        y = _expert_ffn(h2, aff, lw)
        x = _ln2_call(x1, y, mods[5], lw["ln2_g"], lw["ln2_b"], alpha)
        if need_ctx:
            oc = _ctx_attn_call(lw["sink"], pc)
            xc1, hc2, affc = _merge_call(xc, oc, mods_c[:5], lw, alpha)
            yc = _expert_ffn(hc2, affc, lw)
            xc = _ln2_call(xc1, yc, mods_c[5], lw["ln2_g"], lw["ln2_b"], alpha)
    return x
```

```python
import functools

import numpy as np
import jax
import jax.numpy as jnp
from jax import lax
from jax.experimental import pallas as pl
from jax.experimental.pallas import tpu as pltpu

F32 = jnp.float32
BF16 = jnp.bfloat16

GRID_W = 64
HEAD_DIM = 64
ROPE_THETA = 10000.0
EPS = 1e-6
NEG_INF = -1e30
NA_HEADS = 4
NA_KH = 8
NA_KW = 16
NA_WIN_ROWS = 10
SWA_WINDOW = 128
QBLOCK = 128
MLA_HEADS = 4
MLA_NOPE = 64
MLA_ROPE = 32
MLA_V = 64
MLA_Q_LORA = 192
MLA_KV_LORA = 128
N_BRANCH = 4
BRANCH_W = 256
N_EXPERTS = 16
CAPACITY = 2
LANES = 128
TOKEN_BLOCK = 128
VMEM_LIMIT = 56 * 1024 * 1024
HEAD_PERM = (0, 2, 1, 3)
VT_ROWS = 80
LOG2E = 1.4426950408889634


def _cparams(*sem):
    return pltpu.CompilerParams(dimension_semantics=sem, vmem_limit_bytes=VMEM_LIMIT)


def _dot(a, b):
    return jnp.dot(a, b, preferred_element_type=F32)


def _dot_nt(a, b):
    return lax.dot_general(a, b, (((1,), (1,)), ((), ())), preferred_element_type=F32)


def _ln(x):
    mu = jnp.mean(x, axis=-1, keepdims=True)
    xc = x - mu
    var = jnp.mean(xc * xc, axis=-1, keepdims=True)
    return xc * lax.rsqrt(var + EPS)


def _modulate(x, shift, scale):
    return _ln(x) * (1.0 + scale) + shift


def _lane(shape, dim=None):
    return lax.broadcasted_iota(jnp.int32, shape, len(shape) - 1 if dim is None else dim)


def _rope(y, cos, sin_signed):
    return y * cos + pltpu.roll(y, HEAD_DIM, 1) * sin_signed


def _mod_kernel(c_ref, w_ref, b_ref, o_ref):
    c = c_ref[...]
    s = c / (1.0 + jnp.exp(-c))
    o_ref[0] = _dot(s.astype(BF16), w_ref[0].astype(BF16)) + b_ref[0]


def _mod_call(cc, w_mod, b_mod):
    L, D, N = w_mod.shape
    tn = N // 4
    return pl.pallas_call(
        _mod_kernel,
        grid=(L, N // tn),
        in_specs=[pl.BlockSpec((8, D), lambda l, j: (0, 0)),
                  pl.BlockSpec((1, D, tn), lambda l, j: (l, 0, j)),
                  pl.BlockSpec((1, 1, tn), lambda l, j: (l, 0, j))],
        out_specs=pl.BlockSpec((1, 8, tn), lambda l, j: (l, 0, j)),
        out_shape=jax.ShapeDtypeStruct((L, 8, N), F32),
        compiler_params=_cparams("arbitrary", "arbitrary"),
        name="mod_vectors",
    )(cc, w_mod, b_mod.reshape(L, 1, N))


_C_NA = 0
_C_BQ = 768
_C_BKV = 1024
_C_CQ = 1280
_C_CKV = 1536
_C_DQ = 1792
_C_DKV = 2048
_C_END = 2304


def _inproj_kernel(x_ref, sh_ref, sc_ref, w_ref, cos_ref, sin_ref, cosd_ref, sind_ref,
                   gq_ref, gk_ref, gmq_ref, gmkv_ref, wqb_ref, wkvb_ref,
                   naq, nak, nav, bq, bk, bv, cq, ck, cv, dq, dk, dv):
    hb = _modulate(x_ref[0], sh_ref[0], sc_ref[0]).astype(BF16)

    def seg(a, b):
        return _dot(hb, w_ref[:, a:b])

    cos = cos_ref[...]
    sin = sin_ref[...]
    cosd = cosd_ref[...]
    sind = sind_ref[...]
    qscale = HEAD_DIM ** -0.5 * LOG2E

    def put_values_blocks(ref, v, nheads):
        vt = v.T
        ones = jnp.ones((VT_ROWS - HEAD_DIM, TOKEN_BLOCK), F32)
        for g in range(vt.shape[1] // TOKEN_BLOCK):
            for h in range(nheads):
                blk = vt[h * HEAD_DIM:(h + 1) * HEAD_DIM, g * TOKEN_BLOCK:(g + 1) * TOKEN_BLOCK]
                ref[0, g, h] = jnp.concatenate([blk, ones], axis=0).astype(BF16)

    def put_values_t(ref, pair, v):
        vt = v.T
        ones = jnp.ones((VT_ROWS - HEAD_DIM, vt.shape[1]), F32)
        for s in range(2):
            ref[0, pair, 0, s] = jnp.concatenate([vt[s * HEAD_DIM:(s + 1) * HEAD_DIM], ones], axis=0).astype(BF16)

    z = seg(_C_NA, _C_NA + 768)
    naq[0] = (z[:, 0:256] * qscale).astype(BF16)
    nak[0] = z[:, 256:512].astype(BF16)
    put_values_blocks(nav, z[:, 512:768], NA_HEADS)

    slot_a = (_lane((1, LANES)) & 32) == 0

    def slabs(y):
        return [jnp.where(slot_a, y, 0.0), jnp.where(slot_a, 0.0, y)]

    def head_rms(z, gain):
        z2 = z * z
        ss_a = jnp.sum(jnp.where(slot_a, z2, 0.0), axis=-1, keepdims=True)
        ss_b = jnp.sum(jnp.where(slot_a, 0.0, z2), axis=-1, keepdims=True)
        inv = jnp.where(slot_a, lax.rsqrt(ss_a * (1.0 / HEAD_DIM) + EPS), lax.rsqrt(ss_b * (1.0 / HEAD_DIM) + EPS))
        return z * inv * gain

    z = seg(_C_BQ, _C_BKV)
    for c in range(2):
        y = _rope(head_rms(z[:, c * LANES:(c + 1) * LANES], gq_ref[...]), cos, sin) * qscale
        for j, slab in enumerate(slabs(y)):
            bq[0, 2 * c + j] = slab.astype(BF16)
    kv = seg(_C_BKV, _C_CQ)
    bk[0] = _rope(head_rms(kv[:, :LANES], gk_ref[...]), cos, sin).astype(BF16)
    put_values_t(bv, 0, kv[:, LANES:])

    z = seg(_C_CQ, _C_CKV)
    for c in range(2):
        y = _rope(z[:, c * LANES:(c + 1) * LANES], cos, sin) * qscale
        for j, slab in enumerate(slabs(y)):
            cq[0, 2 * c + j] = slab.astype(BF16)
    kv = seg(_C_CKV, _C_DQ)
    ck[0] = _rope(kv[:, :LANES], cos, sin).astype(BF16)
    put_values_blocks(cv, kv[:, LANES:], 2)

    z = seg(_C_DQ, _C_DKV)
    ss = jnp.sum(z * z, axis=-1, keepdims=True) * (1.0 / MLA_Q_LORA)
    cqn = (z * lax.rsqrt(ss + EPS) * gmq_ref[...]).astype(BF16)
    qd = _dot(cqn, wqb_ref[...])
    dscale = (MLA_NOPE + MLA_ROPE) ** -0.5 * LOG2E
    for h in range(MLA_HEADS):
        nope = qd[:, h * 256:h * 256 + LANES] * dscale
        rot = _rope(qd[:, h * 256 + LANES:(h + 1) * 256], cosd, sind) * dscale
        dq[0, h] = jnp.concatenate([nope, rot], axis=-1).astype(BF16)
    zz = seg(_C_DKV, _C_END)
    z = zz[:, :LANES]
    ss = jnp.sum(z * z, axis=-1, keepdims=True) * (1.0 / MLA_KV_LORA)
    ckvn = (z * lax.rsqrt(ss + EPS) * gmkv_ref[...]).astype(BF16)
    kv = _dot(ckvn, wkvb_ref[...])
    kr = _rope(zz[:, LANES:], cosd, sind)
    for p in range(2):
        dk[0, p] = jnp.concatenate([kv[:, p * LANES:(p + 1) * LANES], kr], axis=-1).astype(BF16)
        put_values_t(dv, p, kv[:, 256 + p * LANES:256 + (p + 1) * LANES])


def _inproj_call(x, shift, scale, lw, tabs):
    B, n, D = x.shape
    tm = min(512, n)
    cos, sin, cosd, sind = tabs
    row = lambda b, i: (b, i, 0)
    hrow = lambda b, i: (b, 0, i, 0)
    const2 = lambda b, i: (0, 0)
    tab = pl.BlockSpec((tm, LANES), lambda b, i: (i, 0))
    mod = pl.BlockSpec((1, 1, D), lambda b, i: (b, 0, 0))
    nt = n // tm
    vt_spec = lambda pairs: ((B, pairs, nt, 2, VT_ROWS, tm),
                             pl.BlockSpec((1, pairs, 1, 2, VT_ROWS, tm), lambda b, i: (b, 0, i, 0, 0, 0)))
    slab = lambda kd: ((B, 4, n, kd), pl.BlockSpec((1, 4, tm, kd), hrow))
    tokm = lambda w: ((B, n, w), pl.BlockSpec((1, tm, w), row))
    vblk = lambda heads: ((B, n // TOKEN_BLOCK, heads, VT_ROWS, TOKEN_BLOCK),
                          pl.BlockSpec((1, tm // TOKEN_BLOCK, heads, VT_ROWS, TOKEN_BLOCK),
                                       lambda b, i: (b, i, 0, 0, 0)))
    outs = [tokm(256), tokm(256), vblk(NA_HEADS)]
    outs += [slab(LANES), tokm(LANES), vt_spec(1)]
    outs += [slab(LANES), tokm(LANES), vblk(2)]
    outs += [slab(256), ((B, 2, n, 256), pl.BlockSpec((1, 2, tm, 256), hrow)), vt_spec(2)]
    return pl.pallas_call(
        _inproj_kernel,
        grid=(B, n // tm),
        in_specs=[pl.BlockSpec((1, tm, D), row), mod, mod,
                  pl.BlockSpec((D, _C_END), const2), tab, tab, tab, tab,
                  pl.BlockSpec((1, LANES), const2), pl.BlockSpec((1, LANES), const2),
                  pl.BlockSpec((1, 256), const2), pl.BlockSpec((1, LANES), const2),
                  pl.BlockSpec((256, 1024), const2), pl.BlockSpec((LANES, 512), const2)],
        out_specs=[o[1] for o in outs],
        out_shape=[jax.ShapeDtypeStruct(o[0], BF16) for o in outs],
        compiler_params=_cparams("arbitrary", "arbitrary"),
        name="in_proj",
    )(x, shift, scale, lw["w_qkv"], cos, sin, cosd, sind,
      lw["gq"], lw["gk"], lw["gmq"], lw["gmkv"], lw["wqb"], lw["wkvb"])


def _flash_kernel(q_ref, k_ref, vt_ref, o_ref, acc_ref, m_ref, s_ref, *, tq, nk, per_trip):
    q = q_ref[0].reshape(2 * tq, q_ref.shape[-1])
    m_ref[...] = jnp.full(m_ref.shape, NEG_INF, F32)
    acc_ref[...] = jnp.zeros(acc_ref.shape, F32)

    def update(s, vt):
        m_prev = m_ref[...]
        m_new = jnp.maximum(m_prev, jnp.max(s, axis=0, keepdims=True))
        a = jnp.exp2(m_prev - m_new)
        p = jnp.exp2(s - m_new).astype(BF16)
        for h in range(2):
            cols = slice(h * tq, (h + 1) * tq)
            acc_ref[h] = a[:, cols] * acc_ref[h] + _dot(vt[h], p[:, cols])
        m_ref[...] = m_new

    s_ref[0] = _dot_nt(k_ref[0, 0, 0], q)

    def body(jj, carry):
        for u in range(per_trip):
            j = per_trip * jj + u
            s_ref[(u + 1) % 2] = _dot_nt(k_ref[0, 0, j + 1], q)
            update(s_ref[u % 2], vt_ref[0, 0, j])
        return carry

    lax.fori_loop(0, (nk - 1) // per_trip, body, 0)
    update(s_ref[0], vt_ref[0, 0, nk - 1])
    o = [acc_ref[h][:HEAD_DIM] / acc_ref[h][HEAD_DIM:HEAD_DIM + 1] for h in range(2)]
    o_ref[0] = jnp.concatenate(o, axis=0).T.astype(BF16)


def _flash_call(q, k, vt, kc, vct, *, per_pair):
    B, _, n, kd = q.shape
    P = k.shape[1]
    C = kc.shape[2]
    nk, tk = vt.shape[2], vt.shape[5]
    assert tk % C == 0 and nk % 2 == 0
    k_all = jnp.concatenate([k] + [kc] * (tk // C), axis=2).reshape(B, P, nk + 1, tk, kd)
    vct_pad = jnp.concatenate([vct, jnp.zeros(vct.shape[:-1] + (tk - C,), vct.dtype)], axis=-1)
    vt_all = jnp.concatenate([vt, vct_pad], axis=2)
    tq = min(512, n)
    pidx =(lambda b, p, i: (b, p, 0, 0, 0)) if per_pair else (lambda b, p, i: (b, 0, 0, 0, 0))
    vidx = (lambda b, p, i: (b, p, 0, 0, 0, 0)) if per_pair else (lambda b, p, i: (b, 0, 0, 0, 0, 0))
    kern = functools.partial(_flash_kernel, tq=tq, nk=nk + 1, per_trip=4 if nk % 4 == 0 else 2)
    return pl.pallas_call(
        kern,
        grid=(B, 2, n // tq),
        in_specs=[pl.BlockSpec((1, 2, tq, kd), lambda b, p, i: (b, p, i, 0)),
                  pl.BlockSpec((1, 1, nk + 1, tk, kd), pidx),
                  pl.BlockSpec((1, 1, nk + 1, 2, VT_ROWS, tk), vidx)],
        out_specs=pl.BlockSpec((1, tq, LANES), lambda b, p, i: (b, i, p)),
        out_shape=jax.ShapeDtypeStruct((B, n, 256), BF16),
        scratch_shapes=[pltpu.VMEM((2, VT_ROWS, tq), F32), pltpu.VMEM((1, 2 * tq), F32),
                        pltpu.VMEM((2, tk, 2 * tq), F32)],
        compiler_params=_cparams("arbitrary", "arbitrary", "arbitrary"),
        name="dense_attn",
    )(q, k_all, vt_all)


def _local_values(vt_ref, g0, blocks, vct_ref, head):
    lat = jnp.concatenate([vt_ref[0, g0 + g, head] for g in range(blocks)], axis=1)
    ctx = jnp.concatenate([vct_ref[0, g, head] for g in range(vct_ref.shape[1])], axis=1)
    return lat, ctx


def _na_kernel(q_ref, k_ref, vt_ref, kc_ref, vct_ref, tl_ref, tr_ref, o_ref, *, rows):
    i = pl.program_id(1)
    r0 = 2 * i
    w0 = jnp.clip(r0 - NA_KH // 2, 0, rows - NA_WIN_ROWS)
    win = NA_WIN_ROWS * GRID_W
    start = pl.multiple_of(w0 * GRID_W, 2 * GRID_W)
    k_all = jnp.concatenate([k_ref[0, pl.ds(start, win), :], kc_ref[0]], axis=0)
    q = q_ref[0]
    head_of_lane = _lane((1, 256)) >> 6
    q_stack = jnp.concatenate([jnp.where(head_of_lane == h, q, jnp.zeros_like(q)) for h in range(NA_HEADS)], axis=0)
    s = _dot_nt(k_all, q_stack)

    def table_index(a, j):
        qr = r0 + a
        kr = w0 + j
        st = jnp.clip(qr - NA_KH // 2, 0, rows - NA_KH)
        ok = (kr >= st) & (kr < st + NA_KH)
        return jnp.where(ok, kr - qr + NA_KH - 1, 2 * NA_KH - 1)

    idx = [[table_index(a, j) for j in range(NA_WIN_ROWS)] for a in range(2)]
    bias = jnp.concatenate(
        [jnp.concatenate([tl_ref[h * 16 + idx[0][j]] + tr_ref[h * 16 + idx[1][j]] for j in range(NA_WIN_ROWS)], axis=0)
         for h in range(NA_HEADS)], axis=1)
    s_lat = s[:win] + bias
    s_ctx = s[win:]
    m = jnp.maximum(jnp.max(s_lat, axis=0, keepdims=True), jnp.max(s_ctx, axis=0, keepdims=True))
    p_lat = jnp.exp2(s_lat - m).astype(BF16)
    p_ctx = jnp.exp2(s_ctx - m).astype(BF16)
    outs = []
    for h in range(NA_HEADS):
        cols = slice(h * LANES, (h + 1) * LANES)
        v_lat, v_ctx = _local_values(vt_ref, w0 >> 1, win // TOKEN_BLOCK, vct_ref, h)
        o = _dot(v_lat, p_lat[:, cols]) + _dot(v_ctx, p_ctx[:, cols])
        outs.append(o[:HEAD_DIM] / o[HEAD_DIM:HEAD_DIM + 1])
    o_ref[0] = jnp.concatenate(outs, axis=0).T.astype(BF16)


def _na_call(q, k, vt, kc, vct, tl, tr):
    B, n, _ = q.shape
    C = kc.shape[1]
    rows = n // GRID_W
    assert rows >= NA_WIN_ROWS and rows % 2 == 0
    full = lambda b, i: (b, 0, 0)
    full5 = lambda b, i: (b, 0, 0, 0, 0)
    return pl.pallas_call(
        functools.partial(_na_kernel, rows=rows),
        grid=(B, rows // 2),
        in_specs=[pl.BlockSpec((1, 2 * GRID_W, 256), lambda b, i: (b, i, 0)),
                  pl.BlockSpec((1, n, 256), full), pl.BlockSpec((1,) + vt.shape[1:], full5),
                  pl.BlockSpec((1, C, 256), full), pl.BlockSpec((1,) + vct.shape[1:], full5),
                  pl.BlockSpec(tl.shape, lambda b, i: (0, 0, 0)),
                  pl.BlockSpec(tr.shape, lambda b, i: (0, 0, 0))],
        out_specs=pl.BlockSpec((1, 2 * GRID_W, 256), lambda b, i: (b, i, 0)),
        out_shape=jax.ShapeDtypeStruct((B, n, 256), BF16),
        compiler_params=_cparams("arbitrary", "arbitrary"),
        name="nbr_attn",
    )(q, k, vt, kc, vct, tl, tr)


def _win_kernel(sink_ref, q_ref, k_ref, vt_ref, kc_ref, vct_ref, o_ref, *, n):
    i = pl.program_id(1)
    band = 3 * QBLOCK
    ws = pl.multiple_of(jnp.clip((i - 1) * QBLOCK, 0, n - band), QBLOCK)
    k_all = jnp.concatenate([k_ref[0, pl.ds(ws, band), :], kc_ref[0]], axis=0)
    q_stack = q_ref[0].reshape(4 * QBLOCK, LANES)
    s = _dot_nt(k_all, q_stack)
    kpos = ws + lax.broadcasted_iota(jnp.int32, (band, QBLOCK), 0)
    qpos = i * QBLOCK + lax.broadcasted_iota(jnp.int32, (band, QBLOCK), 1)
    ok = jnp.abs(qpos - kpos) <= SWA_WINDOW
    s_lat = jnp.concatenate([jnp.where(ok, s[:band, j * QBLOCK:(j + 1) * QBLOCK], NEG_INF) for j in range(4)], axis=1)
    s_ctx = s[band:]
    snk = jnp.concatenate([jnp.full((1, QBLOCK), sink_ref[j], F32) for j in range(4)], axis=1)
    m = jnp.maximum(jnp.maximum(jnp.max(s_lat, axis=0, keepdims=True), jnp.max(s_ctx, axis=0, keepdims=True)), snk)
    p_lat = jnp.exp2(s_lat - m).astype(BF16)
    p_ctx = jnp.exp2(s_ctx - m).astype(BF16)
    p_snk = jnp.exp2(snk - m)
    g0 = ws >> 7
    outs = []
    for j in range(4):
        cols = slice(j * QBLOCK, (j + 1) * QBLOCK)
        v_lat, v_ctx = _local_values(vt_ref, g0, band // TOKEN_BLOCK, vct_ref, HEAD_PERM[j] // 2)
        o = _dot(v_lat, p_lat[:, cols]) + _dot(v_ctx, p_ctx[:, cols])
        outs.append(o[:HEAD_DIM] / (o[HEAD_DIM:HEAD_DIM + 1] + p_snk[:, cols]))
    o_ref[0] = jnp.concatenate(outs, axis=0).T.astype(BF16)


def _win_call(sink, q, k, vt, kc, vct):
    B, _, n, _ = q.shape
    C = kc.shape[1]
    assert n >= 3 * QBLOCK and QBLOCK == TOKEN_BLOCK
    full = lambda b, i, s: (b, 0, 0)
    full5 = lambda b, i, s: (b, 0, 0, 0, 0)
    return pl.pallas_call(
        functools.partial(_win_kernel, n=n),
        grid_spec=pltpu.PrefetchScalarGridSpec(
            num_scalar_prefetch=1,
            grid=(B, n // QBLOCK),
            in_specs=[pl.BlockSpec((1, 4, QBLOCK, LANES), lambda b, i, s: (b, 0, i, 0)),
                      pl.BlockSpec((1, n, LANES), full), pl.BlockSpec((1,) + vt.shape[1:], full5),
                      pl.BlockSpec((1, C, LANES), full), pl.BlockSpec((1,) + vct.shape[1:], full5)],
            out_specs=pl.BlockSpec((1, QBLOCK, 256), lambda b, i, s: (b, i, 0))),
        out_shape=jax.ShapeDtypeStruct((B, n, 256), BF16),
        compiler_params=_cparams("arbitrary", "arbitrary"),
        name="window_attn",
    )(sink, q, k, vt, kc, vct)


def _attend(q, k, vt, sink=None):
    s = _dot_nt(q, k)
    m = jnp.max(s, axis=-1, keepdims=True)
    if sink is not None:
        m = jnp.maximum(m, sink)
    p = jnp.exp2(s - m)
    l = jnp.sum(p, axis=-1, keepdims=True)
    if sink is not None:
        l = l + jnp.exp2(sink - m)
    return _dot_nt(p.astype(BF16), vt) / l


def _ctx_attn_kernel(sink_ref, naq, nak, nav, bq, bk, bv, cq, ck, cv, dq, dk, dv, oa, ob, oc, od):
    left = _lane((1, LANES)) < HEAD_DIM
    head_of_lane = _lane((1, 256)) >> 6

    def pair_out(o):
        return jnp.concatenate([jnp.where(left, o[0], o[1]), jnp.where(left, o[2], o[3])], axis=-1).astype(BF16)

    def blocks_t(ref, heads):
        return jnp.concatenate(
            [jnp.concatenate([ref[0, g, h][:HEAD_DIM] for g in range(ref.shape[1])], axis=1)
             for h in range(heads)], axis=0)

    def both_heads(vt):
        return jnp.concatenate([vt[0, :HEAD_DIM], vt[1, :HEAD_DIM]], axis=0)

    q = naq[0]
    v_na = blocks_t(nav, NA_HEADS)
    out = jnp.zeros(q.shape, F32)
    for h in range(NA_HEADS):
        qm = jnp.where(head_of_lane == h, q, jnp.zeros_like(q))
        out = jnp.where(head_of_lane == h, _attend(qm, nak[0], v_na), out)
    oa[0] = out.astype(BF16)
    ob[0] = pair_out([_attend(bq[0, s], bk[0], both_heads(bv[0, 0, 0])) for s in range(4)])
    v_c = blocks_t(cv, 2)
    oc[0] = pair_out([_attend(cq[0, s], ck[0], v_c, sink_ref[s]) for s in range(4)])
    od[0] = pair_out([_attend(dq[0, h], dk[0, h // 2], both_heads(dv[0, h // 2, 0])) for h in range(MLA_HEADS)])


def _ctx_attn_call(sink, pc):
    B, C, _ = pc[0].shape
    names = pc
    specs = []
    for a in names:
        nd = a.ndim
        specs.append(pl.BlockSpec((1,) + a.shape[1:], (lambda b, s, nd=nd: (b,) + (0,) * (nd - 1))))
    out_spec = pl.BlockSpec((1, C, 256), lambda b, s: (b, 0, 0))
    return pl.pallas_call(
        _ctx_attn_kernel,
        grid_spec=pltpu.PrefetchScalarGridSpec(
            num_scalar_prefetch=1, grid=(B,), in_specs=specs, out_specs=[out_spec] * 4),
        out_shape=[jax.ShapeDtypeStruct((B, C, 256), BF16)] * 4,
        compiler_params=_cparams("arbitrary"),
        name="ctx_attn",
    )(sink, *pc)


def _merge_kernel(x_ref, oa, ob, oc, od, sha, sca, ga, shf, scf, wg_ref, wb_ref, wo_ref,
                  l1g, l1b, wrh_ref, wrl_ref, x1_ref, h2_ref, aff_ref, *, alpha):
    x = x_ref[0]
    D = x.shape[-1]
    hb = _modulate(x, sha[0], sca[0]).astype(BF16)
    merged = None
    for i, o in enumerate((oa, ob, oc, od)):
        g = 1.0 / (1.0 + jnp.exp(-_dot(hb, wg_ref[:, i * D:(i + 1) * D])))
        term = g * _dot(o[0], wb_ref[i])
        merged = term if merged is None else merged + term
    y = _dot(merged.astype(BF16), wo_ref[...])
    x1 = _ln(alpha * x + ga[0] * y) * l1g[...] + l1b[...]
    x1_ref[0] = x1
    h2 = _modulate(x1, shf[0], scf[0])
    h2_hi = h2.astype(BF16)
    h2_ref[0] = h2_hi
    h2_lo = (h2 - h2_hi.astype(F32)).astype(BF16)
    logits = _dot(h2_hi, wrh_ref[...]) + _dot(h2_hi, wrl_ref[...]) + _dot(h2_lo, wrh_ref[...])
    logits = jnp.where(_lane((1, LANES)) < N_EXPERTS, logits, NEG_INF)
    e = jnp.exp(logits - jnp.max(logits, axis=-1, keepdims=True))
    aff_t = (e / jnp.sum(e, axis=-1, keepdims=True)).T
    for k in range(aff_ref.shape[1]):
        aff_ref[0, k] = aff_t[:N_EXPERTS, k * TOKEN_BLOCK:(k + 1) * TOKEN_BLOCK]


def _merge_call(x, outs, mods, lw, alpha):
    B, n, D = x.shape
    tm = min(256, n)
    nb = tm // TOKEN_BLOCK
    row = lambda b, i: (b, i, 0)
    mod = pl.BlockSpec((1, 1, D), lambda b, i: (b, 0, 0))
    c2 = lambda b, i: (0, 0)
    obr = pl.BlockSpec((1, tm, 256), row)
    return pl.pallas_call(
        functools.partial(_merge_kernel, alpha=alpha),
        grid=(B, n // tm),
        in_specs=[pl.BlockSpec((1, tm, D), row), obr, obr, obr, obr, mod, mod, mod, mod, mod,
                  pl.BlockSpec((D, N_BRANCH * D), c2),
                  pl.BlockSpec((N_BRANCH, BRANCH_W, D), lambda b, i: (0, 0, 0)),
                  pl.BlockSpec((D, D), c2), pl.BlockSpec((1, D), c2), pl.BlockSpec((1, D), c2),
                  pl.BlockSpec((D, LANES), c2), pl.BlockSpec((D, LANES), c2)],
        out_specs=[pl.BlockSpec((1, tm, D), row), pl.BlockSpec((1, tm, D), row),
                   pl.BlockSpec((1, nb, N_EXPERTS, TOKEN_BLOCK), lambda b, i: (b, i, 0, 0))],
        out_shape=[jax.ShapeDtypeStruct((B, n, D), F32), jax.ShapeDtypeStruct((B, n, D), BF16),
                   jax.ShapeDtypeStruct((B, n // TOKEN_BLOCK, N_EXPERTS, TOKEN_BLOCK), F32)],
        compiler_params=_cparams("arbitrary", "arbitrary"),
        name="merge_router",
    )(x, *outs, *mods, lw["w_gates"], lw["w_branch"], lw["w_out"], lw["ln1_g"], lw["ln1_b"],
      lw["wr_hi"], lw["wr_lo"])


def _select_kernel(aff_ref, pos_ref, off_ref, tlo_ref, thi_ref, *, cap, slot_tile):
    a = aff_ref[0]
    nb = a.shape[0]
    rows = nb * N_EXPERTS
    bits = lax.bitcast_convert_type(a, jnp.int32)
    capf = jnp.float32(cap)

    def count(mask):
        c = jnp.sum(jnp.where(mask, 1.0, 0.0), axis=0)
        return jnp.broadcast_to(jnp.sum(c, axis=-1, keepdims=True), c.shape)

    def search(it, lo):
        cand = lo | lax.shift_left(jnp.int32(1), 30 - it)
        return jnp.where(count(bits >= cand[None]) >= capf, cand, lo)

    thr = lax.fori_loop(0, 31, search, jnp.zeros((N_EXPERTS, TOKEN_BLOCK), jnp.int32))

    r = lax.broadcasted_iota(jnp.int32, (rows, rows), 0)
    c = lax.broadcasted_iota(jnp.int32, (rows, rows), 1)
    earlier = jnp.where(((r & (N_EXPERTS - 1)) == (c & (N_EXPERTS - 1))) & ((c >> 4) < (r >> 4)), 1.0, 0.0).astype(BF16)
    ti = lax.broadcasted_iota(jnp.int32, (TOKEN_BLOCK, TOKEN_BLOCK), 0)
    tj = lax.broadcasted_iota(jnp.int32, (TOKEN_BLOCK, TOKEN_BLOCK), 1)
    tri = jnp.where(ti <= tj, 1.0, 0.0).astype(BF16)
    ones = jnp.ones((TOKEN_BLOCK, TOKEN_BLOCK), BF16)

    def prefix(mask):
        m2 = jnp.where(mask, 1.0, 0.0).reshape(rows, TOKEN_BLOCK)
        mb = m2.astype(BF16)
        within = _dot(mb, tri)
        tot = _dot(mb, ones)
        off = _dot(earlier, tot.astype(BF16))
        shp = (nb, N_EXPERTS, TOKEN_BLOCK)
        return (off + within - m2).reshape(shp), off.reshape(shp), tot.reshape(shp)

    gt = bits > thr[None]
    eq = bits == thr[None]
    need = capf - count(gt)
    eq_rank, _, _ = prefix(eq)
    sel = gt | (eq & (eq_rank < need[None]))
    excl, off, tot = prefix(sel)
    pos_ref[0] = jnp.where(sel, excl, -1.0)
    off_ref[0] = off.astype(jnp.int32)
    tile_start = (_lane((1, 1, TOKEN_BLOCK)) * slot_tile).astype(F32)
    tlo_ref[0] = jnp.sum(jnp.where(off + tot <= tile_start, 1, 0), axis=0).astype(jnp.int32)
    thi_ref[0] = jnp.sum(jnp.where(off < tile_start + slot_tile, 1, 0), axis=0).astype(jnp.int32)


def _select_call(aff, cap, slot_tile):
    B, nb, E, _ = aff.shape
    blk = pl.BlockSpec((1, nb, E, TOKEN_BLOCK), lambda b: (b, 0, 0, 0))
    rng = pl.BlockSpec((1, E, TOKEN_BLOCK), lambda b: (b, 0, 0))
    return pl.pallas_call(
        functools.partial(_select_kernel, cap=cap, slot_tile=slot_tile),
        grid=(B,),
        in_specs=[blk],
        out_specs=[blk, blk, rng, rng],
        out_shape=[jax.ShapeDtypeStruct(aff.shape, F32), jax.ShapeDtypeStruct(aff.shape, jnp.int32),
                   jax.ShapeDtypeStruct((B, E, TOKEN_BLOCK), jnp.int32),
                   jax.ShapeDtypeStruct((B, E, TOKEN_BLOCK), jnp.int32)],
        compiler_params=_cparams("arbitrary"),
        name="expert_select",
    )(aff)


def _gather_kernel(tlo_ref, thi_ref, pos_ref, aff_ref, h_ref, xg_ref, g_ref, acc_ref, gacc_ref, *, slot_tile):
    b = pl.program_id(0)
    e = pl.program_id(1)
    n_tiles = xg_ref.shape[2] // slot_tile
    for t in range(n_tiles):
        acc_ref[...] = jnp.zeros(acc_ref.shape, F32)
        gacc_ref[...] = jnp.zeros(gacc_ref.shape, F32)
        slot = (t * slot_tile + lax.broadcasted_iota(jnp.int32, (slot_tile, 2 * TOKEN_BLOCK), 0)).astype(F32)

        lo = tlo_ref[b, e, t] >> 1
        hi = (thi_ref[b, e, t] + 1) >> 1
        last = pos_ref.shape[1] // 2 - 1

        def one_pair(pb, valid):
            p = jnp.concatenate([pos_ref[0, 2 * pb, pl.ds(e, 1), :], pos_ref[0, 2 * pb + 1, pl.ds(e, 1), :]], axis=-1)
            a = jnp.concatenate([aff_ref[0, 2 * pb, pl.ds(e, 1), :], aff_ref[0, 2 * pb + 1, pl.ds(e, 1), :]], axis=-1)
            hit = p == slot
            tok = pl.multiple_of(pb * (2 * TOKEN_BLOCK), 2 * TOKEN_BLOCK)
            rows = _dot(jnp.where(hit, valid, 0.0).astype(BF16), h_ref[0, pl.ds(tok, 2 * TOKEN_BLOCK), :])
            ga = jnp.where(hit, a * valid, 0.0)
            return rows, ga[:, :TOKEN_BLOCK] + ga[:, TOKEN_BLOCK:]

        def body(it, carry):
            pa = lo + 2 * it
            pb = jnp.minimum(pa + 1, last)
            ra, ga = one_pair(pa, jnp.float32(1.0))
            rb, gb = one_pair(pb, jnp.where(pa + 1 < hi, 1.0, 0.0).astype(F32))
            acc_ref[...] += ra + rb
            gacc_ref[...] += ga + gb
            return carry

        lax.fori_loop(0, (hi - lo + 1) >> 1, body, 0)
        xg_ref[0, 0, t * slot_tile:(t + 1) * slot_tile, :] = acc_ref[...].astype(BF16)
        g_ref[0, 0, t * slot_tile:(t + 1) * slot_tile, :] = jnp.sum(gacc_ref[...], axis=-1, keepdims=True)


def _gather_call(tlo, thi, pos, aff, h2, cap_pad, slot_tile):
    B, n, D = h2.shape
    nb = n // TOKEN_BLOCK
    blk = pl.BlockSpec((1, nb, N_EXPERTS, TOKEN_BLOCK), lambda b, e, *_: (b, 0, 0, 0))
    return pl.pallas_call(
        functools.partial(_gather_kernel, slot_tile=slot_tile),
        grid_spec=pltpu.PrefetchScalarGridSpec(
            num_scalar_prefetch=2,
            grid=(B, N_EXPERTS),
            in_specs=[blk, blk, pl.BlockSpec((1, n, D), lambda b, e, *_: (b, 0, 0))],
            out_specs=[pl.BlockSpec((1, 1, cap_pad, D), lambda b, e, *_: (b, e, 0, 0)),
                       pl.BlockSpec((1, 1, cap_pad, 1), lambda b, e, *_: (b, e, 0, 0))],
            scratch_shapes=[pltpu.VMEM((slot_tile, D), F32), pltpu.VMEM((slot_tile, TOKEN_BLOCK), F32)]),
        out_shape=[jax.ShapeDtypeStruct((B, N_EXPERTS, cap_pad, D), BF16),
                   jax.ShapeDtypeStruct((B, N_EXPERTS, cap_pad, 1), F32)],
        compiler_params=_cparams("arbitrary", "arbitrary"),
        name="expert_gather",
    )(tlo, thi, pos, aff, h2)


def _ffn_kernel(xg_ref, g_ref, wg_ref, wu_ref, wd_ref, y_ref):
    xg = xg_ref[0, 0]
    a = _dot(xg, wg_ref[0])
    u = _dot(xg, wu_ref[0])
    hmid = (a / (1.0 + jnp.exp(-a)) * u).astype(BF16)
    y_ref[0, 0] = (_dot(hmid, wd_ref[0]) * g_ref[0, 0]).astype(BF16)


def _ffn_call(xg, g, lw):
    B, E, cp, D = xg.shape
    F = lw["w_gate"].shape[-1]
    tok = lambda e, b: (b, e, 0, 0)
    wsp = lambda e, b: (e, 0, 0)
    return pl.pallas_call(
        _ffn_kernel,
        grid=(E, B),
        in_specs=[pl.BlockSpec((1, 1, cp, D), tok), pl.BlockSpec((1, 1, cp, 1), tok),
                  pl.BlockSpec((1, D, F), wsp), pl.BlockSpec((1, D, F), wsp), pl.BlockSpec((1, F, D), wsp)],
        out_specs=pl.BlockSpec((1, 1, cp, D), tok),
        out_shape=jax.ShapeDtypeStruct((B, E, cp, D), BF16),
        compiler_params=_cparams("arbitrary", "arbitrary"),
        name="expert_mlp",
    )(xg, g, lw["w_gate"], lw["w_up"], lw["w_down"])


def _combine_kernel(off_ref, pos_ref, yw_ref, x_ref, g_ref, lg, lb, o_ref, *, window, blocks_per_step, alpha):
    b = pl.program_id(0)
    t = pl.program_id(1)
    cap_pad = yw_ref.shape[2]
    nb_total = pl.num_programs(1) * blocks_per_step
    for k in range(blocks_per_step):
        blk = t * blocks_per_step + k
        rows = slice(k * TOKEN_BLOCK, (k + 1) * TOKEN_BLOCK)
        acc = jnp.zeros((TOKEN_BLOCK, yw_ref.shape[-1]), F32)
        for e in range(N_EXPERTS):
            start = off_ref[(b * nb_total + blk) * N_EXPERTS + e]
            w0 = pl.multiple_of(jnp.minimum(start & -16, cap_pad - window), 16)
            p = pos_ref[0, k, e:e + 1, :]
            slot = (w0 + lax.broadcasted_iota(jnp.int32, (window, TOKEN_BLOCK), 0)).astype(F32)
            hit = jnp.where(p == slot, 1.0, 0.0).T.astype(BF16)
            acc = acc + _dot(hit, yw_ref[0, e, pl.ds(w0, window), :])
        o_ref[0, rows, :] = _ln(alpha * x_ref[0, rows, :] + g_ref[0] * acc) * lg[...] + lb[...]


def _combine_call(off_flat, pos, yw, x1, gf, lg, lb, alpha):
    B, E, cap_pad, D = yw.shape
    n = x1.shape[1]
    nb = n // TOKEN_BLOCK
    window = min(256, cap_pad)
    bps = min(4, nb)
    row = pl.BlockSpec((1, bps * TOKEN_BLOCK, D), lambda b, t, *_: (b, t, 0))
    vec = pl.BlockSpec((1, D), lambda b, t, *_: (0, 0))
    return pl.pallas_call(
        functools.partial(_combine_kernel, window=window, blocks_per_step=bps, alpha=alpha),
        grid_spec=pltpu.PrefetchScalarGridSpec(
            num_scalar_prefetch=1,
            grid=(B, nb // bps),
            in_specs=[pl.BlockSpec((1, bps, E, TOKEN_BLOCK), lambda b, t, *_: (b, t, 0, 0)),
                      pl.BlockSpec((1, E, cap_pad, D), lambda b, t, *_: (b, 0, 0, 0),
                                   pipeline_mode=pl.Buffered(1)),
                      row, pl.BlockSpec((1, 1, D), lambda b, t, *_: (b, 0, 0)), vec, vec],
            out_specs=row),
        out_shape=jax.ShapeDtypeStruct((B, n, D), F32),
        compiler_params=_cparams("arbitrary", "arbitrary"),
        name="expert_combine_ln",
    )(off_flat, pos, yw, x1, gf, lg, lb)


def _paired_lanes(dim):
    q = dim // 4
    return np.r_[0:q, 2 * q:3 * q], np.r_[q:2 * q, 3 * q:4 * q]


def _slot_pad(w, slot):
    z = jnp.zeros_like(w)
    return jnp.concatenate([w, z] if slot == 0 else [z, w], axis=-1)


def _prep_layer(l, p):
    w_in = p["w_in"][l]
    D = w_in.shape[0]
    hd = HEAD_DIM

    zc = lambda k: jnp.zeros((D, k), F32)

    def packed(w, a, b):
        ha, hb = w[:, a * hd:(a + 1) * hd], w[:, b * hd:(b + 1) * hd]
        first, second = _paired_lanes(hd)
        return jnp.concatenate([ha[:, first], hb[:, first], ha[:, second], hb[:, second]], axis=-1)

    def rot_slab(w):
        first, second = _paired_lanes(MLA_ROPE)
        z48 = jnp.zeros((w.shape[0], HEAD_DIM - MLA_ROPE // 2), F32)
        return jnp.concatenate([w[:, first], z48, w[:, second], z48], axis=-1)

    def gq_cols(base):
        q, k = w_in[:, base:base + 256], w_in[:, base + 256:base + 384]
        return [packed(q, 0, 2), packed(q, 1, 3), packed(k, 0, 1), w_in[:, base + 384:base + 512]]

    cols = [w_in[:, 0:768]] + gq_cols(768) + gq_cols(1280)
    cols += [w_in[:, 1792:1984], zc(64), w_in[:, 1984:2112], rot_slab(w_in[:, 2112:2144])]
    w_qkv = jnp.concatenate(cols, axis=-1).astype(BF16)
    assert w_qkv.shape[1] == _C_END

    gq = packed(p["gqa_q_norm"][l][None], 0, 0)
    gk = packed(p["gqa_k_norm"][l][None], 0, 0)
    gmq = jnp.concatenate([p["mla_q_norm"][l], jnp.zeros((64,), F32)])[None]
    gmkv = p["mla_kv_norm"][l][None]

    wq = p["mla_w_qb"][l]
    qcols = []
    for h in range(MLA_HEADS):
        nope = wq[:, h * 96: h * 96 + 64]
        rot = wq[:, h * 96 + 64: (h + 1) * 96]
        qcols += [_slot_pad(nope, h % 2), rot_slab(rot)]
    wqb = jnp.concatenate(qcols, axis=-1)
    wqb = jnp.concatenate([wqb, jnp.zeros((64, wqb.shape[1]), F32)], axis=0).astype(BF16)
    wkv = p["mla_w_kvb"][l]
    wkvb = jnp.concatenate([wkv[:, h * 128: h * 128 + 64] for h in range(MLA_HEADS)]
                           + [wkv[:, h * 128 + 64: (h + 1) * 128] for h in range(MLA_HEADS)],
                           axis=-1).astype(BF16)

    wb = p["w_branch"][l]
    perm = np.concatenate([np.arange(h * hd, (h + 1) * hd) for h in HEAD_PERM])
    w_branch = jnp.stack([wb[0], wb[1][perm], wb[2][perm], wb[3]]).astype(BF16)

    wr = jnp.concatenate([p["w_router"][l], jnp.zeros((D, LANES - N_EXPERTS), F32)], axis=-1)
    wr_hi = wr.astype(BF16)
    wr_lo = (wr - wr_hi.astype(F32)).astype(BF16)

    w = jnp.arange(GRID_W)
    col_start = jnp.clip(w - NA_KW // 2, 0, GRID_W - NA_KW)
    col_ok = (w[None, :] >= col_start[:, None]) & (w[None, :] < col_start[:, None] + NA_KW)
    dc_idx = jnp.clip(w[None, :] - w[:, None], 1 - NA_KW, NA_KW - 1) + NA_KW - 1
    t = jnp.where(col_ok[None, None], p["na_rpb"][l][:, :, dc_idx], NEG_INF)
    t = jnp.concatenate([t, jnp.full((NA_HEADS, 1, GRID_W, GRID_W), NEG_INF, F32)], axis=1)
    t = (t * LOG2E).transpose(0, 1, 3, 2).reshape(NA_HEADS * 16, GRID_W, GRID_W)
    zt = jnp.zeros_like(t)
    return dict(
        w_qkv=w_qkv, gq=gq, gk=gk, gmq=gmq, gmkv=gmkv, wqb=wqb, wkvb=wkvb,
        w_gates=w_in[:, 2144:].astype(BF16), w_branch=w_branch, w_out=p["w_out"][l].astype(BF16),
        ln1_g=p["ln1_g"][l][None], ln1_b=p["ln1_b"][l][None],
        ln2_g=p["ln2_g"][l][None], ln2_b=p["ln2_b"][l][None],
        wr_hi=wr_hi, wr_lo=wr_lo,
        w_gate=p["w_gate"][l].astype(BF16), w_up=p["w_up"][l].astype(BF16),
        w_down=p["w_down"][l].astype(BF16),
        tl=jnp.concatenate([t, zt], axis=-1), tr=jnp.concatenate([zt, t], axis=-1),
        sink=p["swa_sink"][l][np.array(HEAD_PERM)] * LOG2E,
    )


def _rope_tables(n, ctx_len):
    pos = jnp.arange(n, dtype=jnp.int32)
    row = (pos // GRID_W).astype(F32)
    col = (pos % GRID_W).astype(F32)

    def axial(dim):
        quarter = dim // 4
        freqs = ROPE_THETA ** (-jnp.arange(quarter, dtype=F32) / quarter)
        ang = jnp.concatenate([row[:, None] * freqs[None, :], col[:, None] * freqs[None, :]], axis=-1)
        return jnp.cos(ang), jnp.sin(ang)

    c, s = axial(HEAD_DIM)
    cos = jnp.concatenate([c, c, c, c], axis=-1)
    sin = jnp.concatenate([-s, -s, s, s], axis=-1)
    c, s = axial(MLA_ROPE)
    pad1 = jnp.ones((n, HEAD_DIM - MLA_ROPE // 2), F32)
    pad0 = jnp.zeros((n, HEAD_DIM - MLA_ROPE // 2), F32)
    cosd = jnp.concatenate([c, pad1, c, pad1], axis=-1)
    sind = jnp.concatenate([-s, pad0, s, pad0], axis=-1)
    one = jnp.ones((ctx_len, LANES), F32)
    zero = jnp.zeros((ctx_len, LANES), F32)
    return (cos, sin, cosd, sind), (one, zero, one, zero)


def _expert_ffn_ln(x1, h2, aff, gf, lw, alpha):
    B, n, D = h2.shape
    cap = CAPACITY * n // N_EXPERTS
    cap_pad = max(cap, TOKEN_BLOCK)
    slot_tile = TOKEN_BLOCK
    pos, off, tlo, thi = _select_call(aff, cap, slot_tile)
    xg, g = _gather_call(tlo, thi, pos, aff, h2, cap_pad, slot_tile)
    yw = _ffn_call(xg, g, lw)
    return _combine_call(off[..., 0].reshape(-1), pos, yw, x1, gf, lw["ln2_g"], lw["ln2_b"], alpha)


def kernel(x, c, ctx, c_ctx, w_mod, b_mod, w_in, na_rpb, gqa_q_norm, gqa_k_norm, swa_sink, mla_q_norm, mla_kv_norm, mla_w_qb, mla_w_kvb, w_branch, w_out, ln1_g, ln1_b, ln2_g, ln2_b, w_router, w_gate, w_up, w_down):
    p = dict(w_in=w_in, na_rpb=na_rpb, gqa_q_norm=gqa_q_norm, gqa_k_norm=gqa_k_norm, swa_sink=swa_sink,
             mla_q_norm=mla_q_norm, mla_kv_norm=mla_kv_norm, mla_w_qb=mla_w_qb, mla_w_kvb=mla_w_kvb,
             w_branch=w_branch, w_out=w_out, ln1_g=ln1_g, ln1_b=ln1_b, ln2_g=ln2_g, ln2_b=ln2_b,
             w_router=w_router, w_gate=w_gate, w_up=w_up, w_down=w_down)
    B, n, D = x.shape
    depth = w_in.shape[0]
    C = ctx.shape[1]
    alpha = (2 * depth) ** 0.25
    assert B + 1 <= 8
    cc = jnp.concatenate([c, c_ctx[None], jnp.zeros((8 - B - 1, D), F32)], axis=0)
    mod_all = _mod_call(cc, w_mod, b_mod)
    tabs, tabs_ctx = _rope_tables(n, C)
    xc = ctx
    for l in range(depth):
        lw = _prep_layer(l, p)
        need_ctx = l < depth - 1
        mods = [mod_all[l, :B, k * D:(k + 1) * D][:, None, :] for k in range(6)]
        mods_c = [jnp.broadcast_to(mod_all[l, B, k * D:(k + 1) * D][None, None, :], (B, 1, D)) for k in range(6)]
        pl_ = _inproj_call(x, mods[0], mods[1], lw, tabs)
        pc = _inproj_call(xc, mods_c[0], mods_c[1], lw, tabs_ctx)
        (naq, nak, nav, bq, bk, bv, cq, ck, cv, dq, dk, dv) = pl_
        (_, nakc, navc, _, bkc, bvc, _, ckc, cvc, _, dkc, dvc) = pc
        o_a = _na_call(naq, nak, nav, nakc, navc, lw["tl"], lw["tr"])
        o_b = _flash_call(bq, bk[:, None], bv, bkc[:, None], bvc, per_pair=False)
        o_c = _win_call(lw["sink"], cq, ck, cv, ckc, cvc)
        o_d = _flash_call(dq, dk, dv, dkc, dvc, per_pair=True)
        x1, h2, aff = _merge_call(x, (o_a, o_b, o_c, o_d), mods[:5], lw, alpha)
        x = _expert_ffn_ln(x1, h2, aff, mods[5], lw, alpha)
        if need_ctx:
            oc = _ctx_attn_call(lw["sink"], pc)
            xc1, hc2, affc = _merge_call(xc, oc, mods_c[:5], lw, alpha)
            xc = _expert_ffn_ln(xc1, hc2, affc, mods_c[5], lw, alpha)
    return x
```

```python
import functools

import numpy as np
import jax
import jax.numpy as jnp
from jax import lax
from jax.experimental import pallas as pl
from jax.experimental.pallas import tpu as pltpu

F32 = jnp.float32
BF16 = jnp.bfloat16

GRID_W = 64
HEAD_DIM = 64
ROPE_THETA = 10000.0
EPS = 1e-6
NEG_INF = -1e30
NA_HEADS = 4
NA_KH = 8
NA_KW = 16
NA_WIN_ROWS = 10
SWA_WINDOW = 128
QBLOCK = 128
MLA_HEADS = 4
MLA_NOPE = 64
MLA_ROPE = 32
MLA_V = 64
MLA_Q_LORA = 192
MLA_KV_LORA = 128
N_BRANCH = 4
BRANCH_W = 256
N_EXPERTS = 16
CAPACITY = 2
LANES = 128
TOKEN_BLOCK = 128
VMEM_LIMIT = 56 * 1024 * 1024
HEAD_PERM = (0, 2, 1, 3)
VT_ROWS = 80
LOG2E = 1.4426950408889634
LOCAL_BLOCKS_PER_STEP = 2


def _cparams(*sem):
    return pltpu.CompilerParams(dimension_semantics=sem, vmem_limit_bytes=VMEM_LIMIT)


def _dot(a, b):
    return jnp.dot(a, b, preferred_element_type=F32)


def _dot_nt(a, b):
    return lax.dot_general(a, b, (((1,), (1,)), ((), ())), preferred_element_type=F32)


def _ln(x):
    mu = jnp.mean(x, axis=-1, keepdims=True)
    xc = x - mu
    var = jnp.mean(xc * xc, axis=-1, keepdims=True)
    return xc * lax.rsqrt(var + EPS)


def _modulate(x, shift, scale):
    return _ln(x) * (1.0 + scale) + shift


def _lane(shape, dim=None):
    return lax.broadcasted_iota(jnp.int32, shape, len(shape) - 1 if dim is None else dim)


def _rope(y, cos, sin_signed):
    return y * cos + pltpu.roll(y, HEAD_DIM, 1) * sin_signed


def _mod_kernel(c_ref, w_ref, b_ref, o_ref):
    c = c_ref[...]
    s = c / (1.0 + jnp.exp(-c))
    o_ref[0] = _dot(s.astype(BF16), w_ref[0].astype(BF16)) + b_ref[0]


def _mod_call(cc, w_mod, b_mod):
    L, D, N = w_mod.shape
    tn = N // 4
    return pl.pallas_call(
        _mod_kernel,
        grid=(L, N // tn),
        in_specs=[pl.BlockSpec((8, D), lambda l, j: (0, 0)),
                  pl.BlockSpec((1, D, tn), lambda l, j: (l, 0, j)),
                  pl.BlockSpec((1, 1, tn), lambda l, j: (l, 0, j))],
        out_specs=pl.BlockSpec((1, 8, tn), lambda l, j: (l, 0, j)),
        out_shape=jax.ShapeDtypeStruct((L, 8, N), F32),
        compiler_params=_cparams("arbitrary", "arbitrary"),
        name="mod_vectors",
    )(cc, w_mod, b_mod.reshape(L, 1, N))


_C_NA = 0
_C_BQ = 768
_C_BKV = 1024
_C_CQ = 1280
_C_CKV = 1536
_C_DQ = 1792
_C_DKV = 2048
_C_END = 2304


def _inproj_kernel(x_ref, sh_ref, sc_ref, w_ref, cos_ref, sin_ref, cosd_ref, sind_ref,
                   gq_ref, gk_ref, gmq_ref, gmkv_ref, wqb_ref, wkvb_ref,
                   naq, nak, nav, bq, bk, bv, cq, ck, cv, dq, dk, dv):
    hb = _modulate(x_ref[0], sh_ref[0], sc_ref[0]).astype(BF16)

    def seg(a, b):
        return _dot(hb, w_ref[:, a:b])

    cos = cos_ref[...]
    sin = sin_ref[...]
    cosd = cosd_ref[...]
    sind = sind_ref[...]
    qscale = HEAD_DIM ** -0.5 * LOG2E

    def put_values_blocks(ref, v, nheads):
        vt = v.T
        ones = jnp.ones((VT_ROWS - HEAD_DIM, TOKEN_BLOCK), F32)
        for g in range(vt.shape[1] // TOKEN_BLOCK):
            for h in range(nheads):
                blk = vt[h * HEAD_DIM:(h + 1) * HEAD_DIM, g * TOKEN_BLOCK:(g + 1) * TOKEN_BLOCK]
                ref[0, g, h] = jnp.concatenate([blk, ones], axis=0).astype(BF16)

    def put_values_t(ref, pair, v):
        vt = v.T
        ones = jnp.ones((VT_ROWS - HEAD_DIM, vt.shape[1]), F32)
        for s in range(2):
            ref[0, pair, 0, s] = jnp.concatenate([vt[s * HEAD_DIM:(s + 1) * HEAD_DIM], ones], axis=0).astype(BF16)

    z = seg(_C_NA, _C_NA + 768)
    naq[0] = (z[:, 0:256] * qscale).astype(BF16)
    nak[0] = z[:, 256:512].astype(BF16)
    put_values_blocks(nav, z[:, 512:768], NA_HEADS)

    slot_a = (_lane((1, LANES)) & 32) == 0

    def slabs(y):
        return [jnp.where(slot_a, y, 0.0), jnp.where(slot_a, 0.0, y)]

    def head_rms(z, gain):
        z2 = z * z
        ss_a = jnp.sum(jnp.where(slot_a, z2, 0.0), axis=-1, keepdims=True)
        ss_b = jnp.sum(jnp.where(slot_a, 0.0, z2), axis=-1, keepdims=True)
        inv = jnp.where(slot_a, lax.rsqrt(ss_a * (1.0 / HEAD_DIM) + EPS), lax.rsqrt(ss_b * (1.0 / HEAD_DIM) + EPS))
        return z * inv * gain

    z = seg(_C_BQ, _C_BKV)
    for c in range(2):
        y = _rope(head_rms(z[:, c * LANES:(c + 1) * LANES], gq_ref[...]), cos, sin) * qscale
        for j, slab in enumerate(slabs(y)):
            bq[0, 2 * c + j] = slab.astype(BF16)
    kv = seg(_C_BKV, _C_CQ)
    bk[0] = _rope(head_rms(kv[:, :LANES], gk_ref[...]), cos, sin).astype(BF16)
    put_values_t(bv, 0, kv[:, LANES:])

    z = seg(_C_CQ, _C_CKV)
    for c in range(2):
        y = _rope(z[:, c * LANES:(c + 1) * LANES], cos, sin) * qscale
        for j, slab in enumerate(slabs(y)):
            cq[0, 2 * c + j] = slab.astype(BF16)
    kv = seg(_C_CKV, _C_DQ)
    ck[0] = _rope(kv[:, :LANES], cos, sin).astype(BF16)
    put_values_blocks(cv, kv[:, LANES:], 2)

    z = seg(_C_DQ, _C_DKV)
    ss = jnp.sum(z * z, axis=-1, keepdims=True) * (1.0 / MLA_Q_LORA)
    cqn = (z * lax.rsqrt(ss + EPS) * gmq_ref[...]).astype(BF16)
    qd = _dot(cqn, wqb_ref[...])
    dscale = (MLA_NOPE + MLA_ROPE) ** -0.5 * LOG2E
    for h in range(MLA_HEADS):
        nope = qd[:, h * 256:h * 256 + LANES] * dscale
        rot = _rope(qd[:, h * 256 + LANES:(h + 1) * 256], cosd, sind) * dscale
        dq[0, h] = jnp.concatenate([nope, rot], axis=-1).astype(BF16)
    zz = seg(_C_DKV, _C_END)
    z = zz[:, :LANES]
    ss = jnp.sum(z * z, axis=-1, keepdims=True) * (1.0 / MLA_KV_LORA)
    ckvn = (z * lax.rsqrt(ss + EPS) * gmkv_ref[...]).astype(BF16)
    kv = _dot(ckvn, wkvb_ref[...])
    kr = _rope(zz[:, LANES:], cosd, sind)
    for p in range(2):
        dk[0, p] = jnp.concatenate([kv[:, p * LANES:(p + 1) * LANES], kr], axis=-1).astype(BF16)
        put_values_t(dv, p, kv[:, 256 + p * LANES:256 + (p + 1) * LANES])


def _inproj_call(x, shift, scale, lw, tabs):
    B, n, D = x.shape
    tm = min(512, n)
    cos, sin, cosd, sind = tabs
    row = lambda b, i: (b, i, 0)
    hrow = lambda b, i: (b, 0, i, 0)
    const2 = lambda b, i: (0, 0)
    tab = pl.BlockSpec((tm, LANES), lambda b, i: (i, 0))
    mod = pl.BlockSpec((1, 1, D), lambda b, i: (b, 0, 0))
    nt = n // tm
    vt_spec = lambda pairs: ((B, pairs, nt, 2, VT_ROWS, tm),
                             pl.BlockSpec((1, pairs, 1, 2, VT_ROWS, tm), lambda b, i: (b, 0, i, 0, 0, 0)))
    slab = lambda kd: ((B, 4, n, kd), pl.BlockSpec((1, 4, tm, kd), hrow))
    tokm = lambda w: ((B, n, w), pl.BlockSpec((1, tm, w), row))
    vblk = lambda heads: ((B, n // TOKEN_BLOCK, heads, VT_ROWS, TOKEN_BLOCK),
                          pl.BlockSpec((1, tm // TOKEN_BLOCK, heads, VT_ROWS, TOKEN_BLOCK),
                                       lambda b, i: (b, i, 0, 0, 0)))
    outs = [tokm(256), tokm(256), vblk(NA_HEADS)]
    outs += [slab(LANES), tokm(LANES), vt_spec(1)]
    outs += [slab(LANES), tokm(LANES), vblk(2)]
    outs += [slab(256), ((B, 2, n, 256), pl.BlockSpec((1, 2, tm, 256), hrow)), vt_spec(2)]
    return pl.pallas_call(
        _inproj_kernel,
        grid=(B, n // tm),
        in_specs=[pl.BlockSpec((1, tm, D), row), mod, mod,
                  pl.BlockSpec((D, _C_END), const2), tab, tab, tab, tab,
                  pl.BlockSpec((1, LANES), const2), pl.BlockSpec((1, LANES), const2),
                  pl.BlockSpec((1, 256), const2), pl.BlockSpec((1, LANES), const2),
                  pl.BlockSpec((256, 1024), const2), pl.BlockSpec((LANES, 512), const2)],
        out_specs=[o[1] for o in outs],
        out_shape=[jax.ShapeDtypeStruct(o[0], BF16) for o in outs],
        compiler_params=_cparams("arbitrary", "arbitrary"),
        name="in_proj",
    )(x, shift, scale, lw["w_qkv"], cos, sin, cosd, sind,
      lw["gq"], lw["gk"], lw["gmq"], lw["gmkv"], lw["wqb"], lw["wkvb"])


def _flash_kernel(q_ref, k_ref, vt_ref, o_ref, acc_ref, m_ref, s_ref, *, tq, nk, per_trip, nq):
    kd = q_ref.shape[-1]

    def load_q(i):
        return q_ref[0, :, pl.ds(pl.multiple_of(i * tq, tq), tq), :].reshape(2 * tq, kd)

    def update(s, vt):
        m_prev = m_ref[...]
        m_new = jnp.maximum(m_prev, jnp.max(s, axis=0, keepdims=True))
        a = jnp.exp2(m_prev - m_new)
        p = jnp.exp2(s - m_new).astype(BF16)
        for h in range(2):
            cols = slice(h * tq, (h + 1) * tq)
            acc_ref[h] = a[:, cols] * acc_ref[h] + _dot(vt[h], p[:, cols])
        m_ref[...] = m_new

    def query_block(i, buf):
        q = load_q(i)
        m_ref[...] = jnp.full(m_ref.shape, NEG_INF, F32)
        acc_ref[...] = jnp.zeros(acc_ref.shape, F32)

        def body(jj, carry):
            for u in range(per_trip):
                j = per_trip * jj + u
                s_ref[(buf + u + 1) % 2] = _dot_nt(k_ref[0, 0, j + 1], q)
                update(s_ref[(buf + u) % 2], vt_ref[0, 0, j])
            return carry

        lax.fori_loop(0, (nk - 1) // per_trip, body, 0)
        s_ref[1 - buf] = _dot_nt(k_ref[0, 0, 0], load_q(jnp.minimum(i + 1, nq - 1)))
        update(s_ref[buf], vt_ref[0, 0, nk - 1])
        o = [acc_ref[h][:HEAD_DIM] / acc_ref[h][HEAD_DIM:HEAD_DIM + 1] for h in range(2)]
        o_ref[0, pl.ds(pl.multiple_of(i * tq, tq), tq), :] = jnp.concatenate(o, axis=0).T.astype(BF16)

    s_ref[0] = _dot_nt(k_ref[0, 0, 0], load_q(0))

    def two_blocks(ii, carry):
        query_block(2 * ii, 0)
        query_block(2 * ii + 1, 1)
        return carry

    lax.fori_loop(0, nq // 2, two_blocks, 0)


def _flash_call(q, k, vt, kc, vct, *, per_pair):
    B, _, n, kd = q.shape
    P = k.shape[1]
    C = kc.shape[2]
    nk, tk = vt.shape[2], vt.shape[5]
    assert tk % C == 0 and nk % 2 == 0
    k_all = jnp.concatenate([k] + [kc] * (tk // C), axis=2).reshape(B, P, nk + 1, tk, kd)
    vct_pad = jnp.concatenate([vct, jnp.zeros(vct.shape[:-1] + (tk - C,), vct.dtype)], axis=-1)
    vt_all = jnp.concatenate([vt, vct_pad], axis=2)
    tq = min(512, n)
    nq = n // tq
    assert nq % 2 == 0
    pidx = (lambda b, p: (b, p, 0, 0, 0)) if per_pair else (lambda b, p: (b, 0, 0, 0, 0))
    vidx = (lambda b, p: (b, p, 0, 0, 0, 0)) if per_pair else (lambda b, p: (b, 0, 0, 0, 0, 0))
    kern = functools.partial(_flash_kernel, tq=tq, nk=nk + 1, per_trip=4 if nk % 4 == 0 else 2, nq=nq)
    return pl.pallas_call(
        kern,
        grid=(B, 2),
        in_specs=[pl.BlockSpec((1, 2, n, kd), lambda b, p: (b, p, 0, 0)),
                  pl.BlockSpec((1, 1, nk + 1, tk, kd), pidx),
                  pl.BlockSpec((1, 1, nk + 1, 2, VT_ROWS, tk), vidx)],
        out_specs=pl.BlockSpec((1, n, LANES), lambda b, p: (b, 0, p)),
        out_shape=jax.ShapeDtypeStruct((B, n, 256), BF16),
        scratch_shapes=[pltpu.VMEM((2, VT_ROWS, tq), F32), pltpu.VMEM((1, 2 * tq), F32),
                        pltpu.VMEM((2, tk, 2 * tq), F32)],
        compiler_params=_cparams("arbitrary", "arbitrary"),
        name="dense_attn",
    )(q, k_all, vt_all)


def _local_values(vt_ref, g0, blocks, vct_ref, head):
    lat = jnp.concatenate([vt_ref[0, g0 + g, head] for g in range(blocks)], axis=1)
    ctx = jnp.concatenate([vct_ref[0, g, head] for g in range(vct_ref.shape[1])], axis=1)
    return lat, ctx


def _na_kernel(q_ref, k_ref, vt_ref, kc_ref, vct_ref, tl_ref, tr_ref, o_ref, *, rows):
    for sub in range(LOCAL_BLOCKS_PER_STEP):
        rows_q = slice(sub * 2 * GRID_W, (sub + 1) * 2 * GRID_W)
        o_ref[0, rows_q, :] = _na_block(LOCAL_BLOCKS_PER_STEP * pl.program_id(1) + sub, q_ref[0, rows_q, :],
                                        k_ref, vt_ref, kc_ref, vct_ref, tl_ref, tr_ref, rows)


def _na_block(i, q, k_ref, vt_ref, kc_ref, vct_ref, tl_ref, tr_ref, rows):
    r0 = 2 * i
    w0 = jnp.clip(r0 - NA_KH // 2, 0, rows - NA_WIN_ROWS)
    win = NA_WIN_ROWS * GRID_W
    start = pl.multiple_of(w0 * GRID_W, 2 * GRID_W)
    k_all = jnp.concatenate([k_ref[0, pl.ds(start, win), :], kc_ref[0]], axis=0)
    head_of_lane = _lane((1, 256)) >> 6
    q_stack = jnp.concatenate([jnp.where(head_of_lane == h, q, jnp.zeros_like(q)) for h in range(NA_HEADS)], axis=0)
    s = _dot_nt(k_all, q_stack)

    def table_index(a, j):
        qr = r0 + a
        kr = w0 + j
        st = jnp.clip(qr - NA_KH // 2, 0, rows - NA_KH)
        ok = (kr >= st) & (kr < st + NA_KH)
        return jnp.where(ok, kr - qr + NA_KH - 1, 2 * NA_KH - 1)

    idx = [[table_index(a, j) for j in range(NA_WIN_ROWS)] for a in range(2)]
    bias = jnp.concatenate(
        [jnp.concatenate([tl_ref[h * 16 + idx[0][j]] + tr_ref[h * 16 + idx[1][j]] for j in range(NA_WIN_ROWS)], axis=0)
         for h in range(NA_HEADS)], axis=1)
    s_lat = s[:win] + bias
    s_ctx = s[win:]
    m = jnp.maximum(jnp.max(s_lat, axis=0, keepdims=True), jnp.max(s_ctx, axis=0, keepdims=True))
    p_lat = jnp.exp2(s_lat - m).astype(BF16)
    p_ctx = jnp.exp2(s_ctx - m).astype(BF16)
    outs = []
    for h in range(NA_HEADS):
        cols = slice(h * LANES, (h + 1) * LANES)
        v_lat, v_ctx = _local_values(vt_ref, w0 >> 1, win // TOKEN_BLOCK, vct_ref, h)
        o = _dot(v_lat, p_lat[:, cols]) + _dot(v_ctx, p_ctx[:, cols])
        outs.append(o[:HEAD_DIM] / o[HEAD_DIM:HEAD_DIM + 1])
    return jnp.concatenate(outs, axis=0).T.astype(BF16)


def _na_call(q, k, vt, kc, vct, tl, tr):
    B, n, _ = q.shape
    C = kc.shape[1]
    rows = n // GRID_W
    step_rows = 2 * LOCAL_BLOCKS_PER_STEP
    assert rows >= NA_WIN_ROWS and rows % step_rows == 0
    full = lambda b, i: (b, 0, 0)
    full5 = lambda b, i: (b, 0, 0, 0, 0)
    return pl.pallas_call(
        functools.partial(_na_kernel, rows=rows),
        grid=(B, rows // step_rows),
        in_specs=[pl.BlockSpec((1, step_rows * GRID_W, 256), lambda b, i: (b, i, 0)),
                  pl.BlockSpec((1, n, 256), full), pl.BlockSpec((1,) + vt.shape[1:], full5),
                  pl.BlockSpec((1, C, 256), full), pl.BlockSpec((1,) + vct.shape[1:], full5),
                  pl.BlockSpec(tl.shape, lambda b, i: (0, 0, 0)),
                  pl.BlockSpec(tr.shape, lambda b, i: (0, 0, 0))],
        out_specs=pl.BlockSpec((1, step_rows * GRID_W, 256), lambda b, i: (b, i, 0)),
        out_shape=jax.ShapeDtypeStruct((B, n, 256), BF16),
        compiler_params=_cparams("arbitrary", "arbitrary"),
        name="nbr_attn",
    )(q, k, vt, kc, vct, tl, tr)


def _win_kernel(sink_ref, q_ref, k_ref, vt_ref, kc_ref, vct_ref, o_ref, *, n):
    for sub in range(LOCAL_BLOCKS_PER_STEP):
        rows_q = slice(sub * QBLOCK, (sub + 1) * QBLOCK)
        o_ref[0, rows_q, :] = _win_block(LOCAL_BLOCKS_PER_STEP * pl.program_id(1) + sub, q_ref[0, :, rows_q, :],
                                         sink_ref, k_ref, vt_ref, kc_ref, vct_ref, n)


def _win_block(i, q, sink_ref, k_ref, vt_ref, kc_ref, vct_ref, n):
    band = 3 * QBLOCK
    ws = pl.multiple_of(jnp.clip((i - 1) * QBLOCK, 0, n - band), QBLOCK)
    k_all = jnp.concatenate([k_ref[0, pl.ds(ws, band), :], kc_ref[0]], axis=0)
    q_stack = q.reshape(4 * QBLOCK, LANES)
    s = _dot_nt(k_all, q_stack)
    kpos = ws + lax.broadcasted_iota(jnp.int32, (band, QBLOCK), 0)
    qpos = i * QBLOCK + lax.broadcasted_iota(jnp.int32, (band, QBLOCK), 1)
    ok = jnp.abs(qpos - kpos) <= SWA_WINDOW
    s_lat = jnp.concatenate([jnp.where(ok, s[:band, j * QBLOCK:(j + 1) * QBLOCK], NEG_INF) for j in range(4)], axis=1)
    s_ctx = s[band:]
    snk = jnp.concatenate([jnp.full((1, QBLOCK), sink_ref[j], F32) for j in range(4)], axis=1)
    m = jnp.maximum(jnp.maximum(jnp.max(s_lat, axis=0, keepdims=True), jnp.max(s_ctx, axis=0, keepdims=True)), snk)
    p_lat = jnp.exp2(s_lat - m).astype(BF16)
    p_ctx = jnp.exp2(s_ctx - m).astype(BF16)
    p_snk = jnp.exp2(snk - m)
    g0 = ws >> 7
    outs = []
    for j in range(4):
        cols = slice(j * QBLOCK, (j + 1) * QBLOCK)
        v_lat, v_ctx = _local_values(vt_ref, g0, band // TOKEN_BLOCK, vct_ref, HEAD_PERM[j] // 2)
        o = _dot(v_lat, p_lat[:, cols]) + _dot(v_ctx, p_ctx[:, cols])
        outs.append(o[:HEAD_DIM] / (o[HEAD_DIM:HEAD_DIM + 1] + p_snk[:, cols]))
    return jnp.concatenate(outs, axis=0).T.astype(BF16)


def _win_call(sink, q, k, vt, kc, vct):
    B, _, n, _ = q.shape
    C = kc.shape[1]
    step = LOCAL_BLOCKS_PER_STEP * QBLOCK
    assert n >= 3 * QBLOCK and QBLOCK == TOKEN_BLOCK and n % step == 0
    full = lambda b, i, s: (b, 0, 0)
    full5 = lambda b, i, s: (b, 0, 0, 0, 0)
    return pl.pallas_call(
        functools.partial(_win_kernel, n=n),
        grid_spec=pltpu.PrefetchScalarGridSpec(
            num_scalar_prefetch=1,
            grid=(B, n // step),
            in_specs=[pl.BlockSpec((1, 4, step, LANES), lambda b, i, s: (b, 0, i, 0)),
                      pl.BlockSpec((1, n, LANES), full), pl.BlockSpec((1,) + vt.shape[1:], full5),
                      pl.BlockSpec((1, C, LANES), full), pl.BlockSpec((1,) + vct.shape[1:], full5)],
            out_specs=pl.BlockSpec((1, step, 256), lambda b, i, s: (b, i, 0))),
        out_shape=jax.ShapeDtypeStruct((B, n, 256), BF16),
        compiler_params=_cparams("arbitrary", "arbitrary"),
        name="window_attn",
    )(sink, q, k, vt, kc, vct)


def _attend(q, k, vt, sink=None):
    s = _dot_nt(q, k)
    m = jnp.max(s, axis=-1, keepdims=True)
    if sink is not None:
        m = jnp.maximum(m, sink)
    p = jnp.exp2(s - m)
    l = jnp.sum(p, axis=-1, keepdims=True)
    if sink is not None:
        l = l + jnp.exp2(sink - m)
    return _dot_nt(p.astype(BF16), vt) / l


def _ctx_attn_kernel(sink_ref, naq, nak, nav, bq, bk, bv, cq, ck, cv, dq, dk, dv, oa, ob, oc, od):
    left = _lane((1, LANES)) < HEAD_DIM
    head_of_lane = _lane((1, 256)) >> 6

    def pair_out(o):
        return jnp.concatenate([jnp.where(left, o[0], o[1]), jnp.where(left, o[2], o[3])], axis=-1).astype(BF16)

    def blocks_t(ref, heads):
        return jnp.concatenate(
            [jnp.concatenate([ref[0, g, h][:HEAD_DIM] for g in range(ref.shape[1])], axis=1)
             for h in range(heads)], axis=0)

    def both_heads(vt):
        return jnp.concatenate([vt[0, :HEAD_DIM], vt[1, :HEAD_DIM]], axis=0)

    q = naq[0]
    v_na = blocks_t(nav, NA_HEADS)
    out = jnp.zeros(q.shape, F32)
    for h in range(NA_HEADS):
        qm = jnp.where(head_of_lane == h, q, jnp.zeros_like(q))
        out = jnp.where(head_of_lane == h, _attend(qm, nak[0], v_na), out)
    oa[0] = out.astype(BF16)
    ob[0] = pair_out([_attend(bq[0, s], bk[0], both_heads(bv[0, 0, 0])) for s in range(4)])
    v_c = blocks_t(cv, 2)
    oc[0] = pair_out([_attend(cq[0, s], ck[0], v_c, sink_ref[s]) for s in range(4)])
    od[0] = pair_out([_attend(dq[0, h], dk[0, h // 2], both_heads(dv[0, h // 2, 0])) for h in range(MLA_HEADS)])


def _ctx_attn_call(sink, pc):
    B, C, _ = pc[0].shape
    names = pc
    specs = []
    for a in names:
        nd = a.ndim
        specs.append(pl.BlockSpec((1,) + a.shape[1:], (lambda b, s, nd=nd: (b,) + (0,) * (nd - 1))))
    out_spec = pl.BlockSpec((1, C, 256), lambda b, s: (b, 0, 0))
    return pl.pallas_call(
        _ctx_attn_kernel,
        grid_spec=pltpu.PrefetchScalarGridSpec(
            num_scalar_prefetch=1, grid=(B,), in_specs=specs, out_specs=[out_spec] * 4),
        out_shape=[jax.ShapeDtypeStruct((B, C, 256), BF16)] * 4,
        compiler_params=_cparams("arbitrary"),
        name="ctx_attn",
    )(sink, *pc)


def _merge_kernel(x_ref, oa, ob, oc, od, sha, sca, ga, shf, scf, wg_ref, wb_ref, wo_ref,
                  l1g, l1b, wrh_ref, wrl_ref, x1_ref, h2_ref, aff_ref, *, alpha):
    x = x_ref[0]
    D = x.shape[-1]
    hb = _modulate(x, sha[0], sca[0]).astype(BF16)
    merged = None
    for i, o in enumerate((oa, ob, oc, od)):
        g = 1.0 / (1.0 + jnp.exp(-_dot(hb, wg_ref[:, i * D:(i + 1) * D])))
        term = g * _dot(o[0], wb_ref[i])
        merged = term if merged is None else merged + term
    y = _dot(merged.astype(BF16), wo_ref[...])
    x1 = _ln(alpha * x + ga[0] * y) * l1g[...] + l1b[...]
    x1_ref[0] = x1
    h2 = _modulate(x1, shf[0], scf[0])
    h2_hi = h2.astype(BF16)
    h2_ref[0] = h2_hi
    h2_lo = (h2 - h2_hi.astype(F32)).astype(BF16)
    logits = _dot(h2_hi, wrh_ref[...]) + _dot(h2_hi, wrl_ref[...]) + _dot(h2_lo, wrh_ref[...])
    logits = jnp.where(_lane((1, LANES)) < N_EXPERTS, logits, NEG_INF)
    e = jnp.exp(logits - jnp.max(logits, axis=-1, keepdims=True))
    aff_t = (e / jnp.sum(e, axis=-1, keepdims=True)).T
    for k in range(aff_ref.shape[1]):
        aff_ref[0, k] = aff_t[:N_EXPERTS, k * TOKEN_BLOCK:(k + 1) * TOKEN_BLOCK]


def _merge_call(x, outs, mods, lw, alpha):
    B, n, D = x.shape
    tm = min(256, n)
    nb = tm // TOKEN_BLOCK
    row = lambda b, i: (b, i, 0)
    mod = pl.BlockSpec((1, 1, D), lambda b, i: (b, 0, 0))
    c2 = lambda b, i: (0, 0)
    obr = pl.BlockSpec((1, tm, 256), row)
    return pl.pallas_call(
        functools.partial(_merge_kernel, alpha=alpha),
        grid=(B, n // tm),
        in_specs=[pl.BlockSpec((1, tm, D), row), obr, obr, obr, obr, mod, mod, mod, mod, mod,
                  pl.BlockSpec((D, N_BRANCH * D), c2),
                  pl.BlockSpec((N_BRANCH, BRANCH_W, D), lambda b, i: (0, 0, 0)),
                  pl.BlockSpec((D, D), c2), pl.BlockSpec((1, D), c2), pl.BlockSpec((1, D), c2),
                  pl.BlockSpec((D, LANES), c2), pl.BlockSpec((D, LANES), c2)],
        out_specs=[pl.BlockSpec((1, tm, D), row), pl.BlockSpec((1, tm, D), row),
                   pl.BlockSpec((1, nb, N_EXPERTS, TOKEN_BLOCK), lambda b, i: (b, i, 0, 0))],
        out_shape=[jax.ShapeDtypeStruct((B, n, D), F32), jax.ShapeDtypeStruct((B, n, D), BF16),
                   jax.ShapeDtypeStruct((B, n // TOKEN_BLOCK, N_EXPERTS, TOKEN_BLOCK), F32)],
        compiler_params=_cparams("arbitrary", "arbitrary"),
        name="merge_router",
    )(x, *outs, *mods, lw["w_gates"], lw["w_branch"], lw["w_out"], lw["ln1_g"], lw["ln1_b"],
      lw["wr_hi"], lw["wr_lo"])


def _select_kernel(aff_ref, pos_ref, off_ref, tlo_ref, thi_ref, *, cap, slot_tile):
    a = aff_ref[0]
    nb = a.shape[0]
    rows = nb * N_EXPERTS
    bits = lax.bitcast_convert_type(a, jnp.int32)
    capf = jnp.float32(cap)

    def count(mask):
        c = jnp.sum(jnp.where(mask, 1.0, 0.0), axis=0)
        return jnp.broadcast_to(jnp.sum(c, axis=-1, keepdims=True), c.shape)

    def search(it, lo):
        cand = lo | lax.shift_left(jnp.int32(1), 30 - it)
        return jnp.where(count(bits >= cand[None]) >= capf, cand, lo)

    thr = lax.fori_loop(0, 31, search, jnp.zeros((N_EXPERTS, TOKEN_BLOCK), jnp.int32))

    r = lax.broadcasted_iota(jnp.int32, (rows, rows), 0)
    c = lax.broadcasted_iota(jnp.int32, (rows, rows), 1)
    earlier = jnp.where(((r & (N_EXPERTS - 1)) == (c & (N_EXPERTS - 1))) & ((c >> 4) < (r >> 4)), 1.0, 0.0).astype(BF16)
    ti = lax.broadcasted_iota(jnp.int32, (TOKEN_BLOCK, TOKEN_BLOCK), 0)
    tj = lax.broadcasted_iota(jnp.int32, (TOKEN_BLOCK, TOKEN_BLOCK), 1)
    tri = jnp.where(ti <= tj, 1.0, 0.0).astype(BF16)
    ones = jnp.ones((TOKEN_BLOCK, TOKEN_BLOCK), BF16)

    def prefix(mask):
        m2 = jnp.where(mask, 1.0, 0.0).reshape(rows, TOKEN_BLOCK)
        mb = m2.astype(BF16)
        within = _dot(mb, tri)
        tot = _dot(mb, ones)
        off = _dot(earlier, tot.astype(BF16))
        shp = (nb, N_EXPERTS, TOKEN_BLOCK)
        return (off + within - m2).reshape(shp), off.reshape(shp), tot.reshape(shp)

    gt = bits > thr[None]
    eq = bits == thr[None]
    need = capf - count(gt)
    eq_rank, _, _ = prefix(eq)
    sel = gt | (eq & (eq_rank < need[None]))
    excl, off, tot = prefix(sel)
    pos_ref[0] = jnp.where(sel, excl, -1.0)
    off_ref[0] = off.astype(jnp.int32)
    tile_start = (_lane((1, 1, TOKEN_BLOCK)) * slot_tile).astype(F32)
    tlo_ref[0] = jnp.sum(jnp.where(off + tot <= tile_start, 1, 0), axis=0).astype(jnp.int32)
    thi_ref[0] = jnp.sum(jnp.where(off < tile_start + slot_tile, 1, 0), axis=0).astype(jnp.int32)


def _select_call(aff, cap, slot_tile):
    B, nb, E, _ = aff.shape
    blk = pl.BlockSpec((1, nb, E, TOKEN_BLOCK), lambda b: (b, 0, 0, 0))
    rng = pl.BlockSpec((1, E, TOKEN_BLOCK), lambda b: (b, 0, 0))
    return pl.pallas_call(
        functools.partial(_select_kernel, cap=cap, slot_tile=slot_tile),
        grid=(B,),
        in_specs=[blk],
        out_specs=[blk, blk, rng, rng],
        out_shape=[jax.ShapeDtypeStruct(aff.shape, F32), jax.ShapeDtypeStruct(aff.shape, jnp.int32),
                   jax.ShapeDtypeStruct((B, E, TOKEN_BLOCK), jnp.int32),
                   jax.ShapeDtypeStruct((B, E, TOKEN_BLOCK), jnp.int32)],
        compiler_params=_cparams("arbitrary"),
        name="expert_select",
    )(aff)


def _gather_kernel(tlo_ref, thi_ref, pos_ref, aff_ref, h_ref, xg_ref, g_ref, acc_ref, gacc_ref, *,
                   slot_tile, group, in_flight):
    b = pl.program_id(0)
    e = pl.program_id(1)
    n_tiles = xg_ref.shape[2] // slot_tile
    for t in range(n_tiles):
        acc_ref[...] = jnp.zeros(acc_ref.shape, F32)
        gacc_ref[...] = jnp.zeros(gacc_ref.shape, F32)
        width = group * TOKEN_BLOCK
        slot = (t * slot_tile + lax.broadcasted_iota(jnp.int32, (slot_tile, width), 0)).astype(F32)
        shift = group.bit_length() - 1
        lo = tlo_ref[b, e, t] >> shift
        hi = (thi_ref[b, e, t] + group - 1) >> shift
        last = pos_ref.shape[1] // group - 1

        def one_product(pg, valid):
            p = jnp.concatenate([pos_ref[0, group * pg + u, pl.ds(e, 1), :] for u in range(group)], axis=-1)
            a = jnp.concatenate([aff_ref[0, group * pg + u, pl.ds(e, 1), :] for u in range(group)], axis=-1)
            hit = p == slot
            tok = pl.multiple_of(pg * width, width)
            rows = _dot(jnp.where(hit, valid, 0.0).astype(BF16), h_ref[0, pl.ds(tok, width), :])
            ga = jnp.where(hit, a * valid, 0.0)
            return rows, sum(ga[:, u * TOKEN_BLOCK:(u + 1) * TOKEN_BLOCK] for u in range(group))

        def body(it, carry):
            rows, gates = None, None
            for u in range(in_flight):
                pg = lo + in_flight * it + u
                r, g = one_product(jnp.minimum(pg, last), jnp.where(pg < hi, 1.0, 0.0).astype(F32))
                rows = r if rows is None else rows + r
                gates = g if gates is None else gates + g
            acc_ref[...] += rows
            gacc_ref[...] += gates
            return carry

        lax.fori_loop(0, lax.div(hi - lo + in_flight - 1, in_flight), body, 0)
        xg_ref[0, 0, t * slot_tile:(t + 1) * slot_tile, :] = acc_ref[...].astype(BF16)
        g_ref[0, 0, t * slot_tile:(t + 1) * slot_tile, :] = jnp.sum(gacc_ref[...], axis=-1, keepdims=True)


def _gather_call(tlo, thi, pos, aff, h2, cap_pad, slot_tile):
    B, n, D = h2.shape
    nb = n // TOKEN_BLOCK
    blk = pl.BlockSpec((1, nb, N_EXPERTS, TOKEN_BLOCK), lambda b, e, *_: (b, 0, 0, 0))
    return pl.pallas_call(
        functools.partial(_gather_kernel, slot_tile=slot_tile, group=2, in_flight=min(5, nb // 2)),
        grid_spec=pltpu.PrefetchScalarGridSpec(
            num_scalar_prefetch=2,
            grid=(B, N_EXPERTS),
            in_specs=[blk, blk, pl.BlockSpec((1, n, D), lambda b, e, *_: (b, 0, 0))],
            out_specs=[pl.BlockSpec((1, 1, cap_pad, D), lambda b, e, *_: (b, e, 0, 0)),
                       pl.BlockSpec((1, 1, cap_pad, 1), lambda b, e, *_: (b, e, 0, 0))],
            scratch_shapes=[pltpu.VMEM((slot_tile, D), F32), pltpu.VMEM((slot_tile, TOKEN_BLOCK), F32)]),
        out_shape=[jax.ShapeDtypeStruct((B, N_EXPERTS, cap_pad, D), BF16),
                   jax.ShapeDtypeStruct((B, N_EXPERTS, cap_pad, 1), F32)],
        compiler_params=_cparams("arbitrary", "arbitrary"),
        name="expert_gather",
    )(tlo, thi, pos, aff, h2)


def _ffn_kernel(xg_ref, g_ref, wg_ref, wu_ref, wd_ref, y_ref):
    xg = xg_ref[0, 0]
    a = _dot(xg, wg_ref[0])
    u = _dot(xg, wu_ref[0])
    hmid = (a / (1.0 + jnp.exp(-a)) * u).astype(BF16)
    y_ref[0, 0] = (_dot(hmid, wd_ref[0]) * g_ref[0, 0]).astype(BF16)


def _ffn_call(xg, g, lw):
    B, E, cp, D = xg.shape
    F = lw["w_gate"].shape[-1]
    tok = lambda e, b: (b, e, 0, 0)
    wsp = lambda e, b: (e, 0, 0)
    return pl.pallas_call(
        _ffn_kernel,
        grid=(E, B),
        in_specs=[pl.BlockSpec((1, 1, cp, D), tok), pl.BlockSpec((1, 1, cp, 1), tok),
                  pl.BlockSpec((1, D, F), wsp), pl.BlockSpec((1, D, F), wsp), pl.BlockSpec((1, F, D), wsp)],
        out_specs=pl.BlockSpec((1, 1, cp, D), tok),
        out_shape=jax.ShapeDtypeStruct((B, E, cp, D), BF16),
        compiler_params=_cparams("arbitrary", "arbitrary"),
        name="expert_mlp",
    )(xg, g, lw["w_gate"], lw["w_up"], lw["w_down"])


def _combine_kernel(off_ref, pos_ref, yw_ref, x_ref, g_ref, lg, lb, o_ref, *, window, blocks_per_step, alpha):
    b = pl.program_id(0)
    t = pl.program_id(1)
    cap_pad = yw_ref.shape[2]
    nb_total = pl.num_programs(1) * blocks_per_step
    for k in range(blocks_per_step):
        blk = t * blocks_per_step + k
        rows = slice(k * TOKEN_BLOCK, (k + 1) * TOKEN_BLOCK)
        acc = jnp.zeros((TOKEN_BLOCK, yw_ref.shape[-1]), F32)
        for e in range(N_EXPERTS):
            start = off_ref[(b * nb_total + blk) * N_EXPERTS + e]
            w0 = pl.multiple_of(jnp.minimum(start & -16, cap_pad - window), 16)
            p = pos_ref[0, k, e:e + 1, :]
            slot = (w0 + lax.broadcasted_iota(jnp.int32, (window, TOKEN_BLOCK), 0)).astype(F32)
            hit = jnp.where(p == slot, 1.0, 0.0).T.astype(BF16)
            acc = acc + _dot(hit, yw_ref[0, e, pl.ds(w0, window), :])
        o_ref[0, rows, :] = _ln(alpha * x_ref[0, rows, :] + g_ref[0] * acc) * lg[...] + lb[...]


def _combine_call(off_flat, pos, yw, x1, gf, lg, lb, alpha):
    B, E, cap_pad, D = yw.shape
    n = x1.shape[1]
    nb = n // TOKEN_BLOCK
    window = min(256, cap_pad)
    bps = min(4, nb)
    row = pl.BlockSpec((1, bps * TOKEN_BLOCK, D), lambda b, t, *_: (b, t, 0))
    vec = pl.BlockSpec((1, D), lambda b, t, *_: (0, 0))
    return pl.pallas_call(
        functools.partial(_combine_kernel, window=window, blocks_per_step=bps, alpha=alpha),
        grid_spec=pltpu.PrefetchScalarGridSpec(
            num_scalar_prefetch=1,
            grid=(B, nb // bps),
            in_specs=[pl.BlockSpec((1, bps, E, TOKEN_BLOCK), lambda b, t, *_: (b, t, 0, 0)),
                      pl.BlockSpec((1, E, cap_pad, D), lambda b, t, *_: (b, 0, 0, 0),
                                   pipeline_mode=pl.Buffered(1)),
                      row, pl.BlockSpec((1, 1, D), lambda b, t, *_: (b, 0, 0)), vec, vec],
            out_specs=row),
        out_shape=jax.ShapeDtypeStruct((B, n, D), F32),
        compiler_params=_cparams("arbitrary", "arbitrary"),
        name="expert_combine_ln",
    )(off_flat, pos, yw, x1, gf, lg, lb)


def _paired_lanes(dim):
    q = dim // 4
    return np.r_[0:q, 2 * q:3 * q], np.r_[q:2 * q, 3 * q:4 * q]


def _slot_pad(w, slot):
    z = jnp.zeros_like(w)
    return jnp.concatenate([w, z] if slot == 0 else [z, w], axis=-1)


def _prep_layer(l, p):
    w_in = p["w_in"][l]
    D = w_in.shape[0]
    hd = HEAD_DIM

    zc = lambda k: jnp.zeros((D, k), F32)

    def packed(w, a, b):
        ha, hb = w[:, a * hd:(a + 1) * hd], w[:, b * hd:(b + 1) * hd]
        first, second = _paired_lanes(hd)
        return jnp.concatenate([ha[:, first], hb[:, first], ha[:, second], hb[:, second]], axis=-1)

    def rot_slab(w):
        first, second = _paired_lanes(MLA_ROPE)
        z48 = jnp.zeros((w.shape[0], HEAD_DIM - MLA_ROPE // 2), F32)
        return jnp.concatenate([w[:, first], z48, w[:, second], z48], axis=-1)

    def gq_cols(base):
        q, k = w_in[:, base:base + 256], w_in[:, base + 256:base + 384]
        return [packed(q, 0, 2), packed(q, 1, 3), packed(k, 0, 1), w_in[:, base + 384:base + 512]]

    cols = [w_in[:, 0:768]] + gq_cols(768) + gq_cols(1280)
    cols += [w_in[:, 1792:1984], zc(64), w_in[:, 1984:2112], rot_slab(w_in[:, 2112:2144])]
    w_qkv = jnp.concatenate(cols, axis=-1).astype(BF16)
    assert w_qkv.shape[1] == _C_END

    gq = packed(p["gqa_q_norm"][l][None], 0, 0)
    gk = packed(p["gqa_k_norm"][l][None], 0, 0)
    gmq = jnp.concatenate([p["mla_q_norm"][l], jnp.zeros((64,), F32)])[None]
    gmkv = p["mla_kv_norm"][l][None]

    wq = p["mla_w_qb"][l]
    qcols = []
    for h in range(MLA_HEADS):
        nope = wq[:, h * 96: h * 96 + 64]
        rot = wq[:, h * 96 + 64: (h + 1) * 96]
        qcols += [_slot_pad(nope, h % 2), rot_slab(rot)]
    wqb = jnp.concatenate(qcols, axis=-1)
    wqb = jnp.concatenate([wqb, jnp.zeros((64, wqb.shape[1]), F32)], axis=0).astype(BF16)
    wkv = p["mla_w_kvb"][l]
    wkvb = jnp.concatenate([wkv[:, h * 128: h * 128 + 64] for h in range(MLA_HEADS)]
                           + [wkv[:, h * 128 + 64: (h + 1) * 128] for h in range(MLA_HEADS)],
                           axis=-1).astype(BF16)

    wb = p["w_branch"][l]
    perm = np.concatenate([np.arange(h * hd, (h + 1) * hd) for h in HEAD_PERM])
    w_branch = jnp.stack([wb[0], wb[1][perm], wb[2][perm], wb[3]]).astype(BF16)

    wr = jnp.concatenate([p["w_router"][l], jnp.zeros((D, LANES - N_EXPERTS), F32)], axis=-1)
    wr_hi = wr.astype(BF16)
    wr_lo = (wr - wr_hi.astype(F32)).astype(BF16)

    w = jnp.arange(GRID_W)
    col_start = jnp.clip(w - NA_KW // 2, 0, GRID_W - NA_KW)
    col_ok = (w[None, :] >= col_start[:, None]) & (w[None, :] < col_start[:, None] + NA_KW)
    dc_idx = jnp.clip(w[None, :] - w[:, None], 1 - NA_KW, NA_KW - 1) + NA_KW - 1
    t = jnp.where(col_ok[None, None], p["na_rpb"][l][:, :, dc_idx], NEG_INF)
    t = jnp.concatenate([t, jnp.full((NA_HEADS, 1, GRID_W, GRID_W), NEG_INF, F32)], axis=1)
    t = (t * LOG2E).transpose(0, 1, 3, 2).reshape(NA_HEADS * 16, GRID_W, GRID_W)
    zt = jnp.zeros_like(t)
    return dict(
        w_qkv=w_qkv, gq=gq, gk=gk, gmq=gmq, gmkv=gmkv, wqb=wqb, wkvb=wkvb,
        w_gates=w_in[:, 2144:].astype(BF16), w_branch=w_branch, w_out=p["w_out"][l].astype(BF16),
        ln1_g=p["ln1_g"][l][None], ln1_b=p["ln1_b"][l][None],
        ln2_g=p["ln2_g"][l][None], ln2_b=p["ln2_b"][l][None],
        wr_hi=wr_hi, wr_lo=wr_lo,
        w_gate=p["w_gate"][l].astype(BF16), w_up=p["w_up"][l].astype(BF16),
        w_down=p["w_down"][l].astype(BF16),
        tl=jnp.concatenate([t, zt], axis=-1), tr=jnp.concatenate([zt, t], axis=-1),
        sink=p["swa_sink"][l][np.array(HEAD_PERM)] * LOG2E,
    )


def _rope_tables(n, ctx_len):
    pos = jnp.arange(n, dtype=jnp.int32)
    row = (pos // GRID_W).astype(F32)
    col = (pos % GRID_W).astype(F32)

    def axial(dim):
        quarter = dim // 4
        freqs = ROPE_THETA ** (-jnp.arange(quarter, dtype=F32) / quarter)
        ang = jnp.concatenate([row[:, None] * freqs[None, :], col[:, None] * freqs[None, :]], axis=-1)
        return jnp.cos(ang), jnp.sin(ang)

    c, s = axial(HEAD_DIM)
    cos = jnp.concatenate([c, c, c, c], axis=-1)
    sin = jnp.concatenate([-s, -s, s, s], axis=-1)
    c, s = axial(MLA_ROPE)
    pad1 = jnp.ones((n, HEAD_DIM - MLA_ROPE // 2), F32)
    pad0 = jnp.zeros((n, HEAD_DIM - MLA_ROPE // 2), F32)
    cosd = jnp.concatenate([c, pad1, c, pad1], axis=-1)
    sind = jnp.concatenate([-s, pad0, s, pad0], axis=-1)
    one = jnp.ones((ctx_len, LANES), F32)
    zero = jnp.zeros((ctx_len, LANES), F32)
    return (cos, sin, cosd, sind), (one, zero, one, zero)


def _expert_ffn_ln(x1, h2, aff, gf, lw, alpha):
    B, n, D = h2.shape
    cap = CAPACITY * n // N_EXPERTS
    cap_pad = max(cap, TOKEN_BLOCK)
    slot_tile = TOKEN_BLOCK
    pos, off, tlo, thi = _select_call(aff, cap, slot_tile)
    xg, g = _gather_call(tlo, thi, pos, aff, h2, cap_pad, slot_tile)
    yw = _ffn_call(xg, g, lw)
    return _combine_call(off[..., 0].reshape(-1), pos, yw, x1, gf, lw["ln2_g"], lw["ln2_b"], alpha)


def kernel(x, c, ctx, c_ctx, w_mod, b_mod, w_in, na_rpb, gqa_q_norm, gqa_k_norm, swa_sink, mla_q_norm, mla_kv_norm, mla_w_qb, mla_w_kvb, w_branch, w_out, ln1_g, ln1_b, ln2_g, ln2_b, w_router, w_gate, w_up, w_down):
    p = dict(w_in=w_in, na_rpb=na_rpb, gqa_q_norm=gqa_q_norm, gqa_k_norm=gqa_k_norm, swa_sink=swa_sink,
             mla_q_norm=mla_q_norm, mla_kv_norm=mla_kv_norm, mla_w_qb=mla_w_qb, mla_w_kvb=mla_w_kvb,
             w_branch=w_branch, w_out=w_out, ln1_g=ln1_g, ln1_b=ln1_b, ln2_g=ln2_g, ln2_b=ln2_b,
             w_router=w_router, w_gate=w_gate, w_up=w_up, w_down=w_down)
    B, n, D = x.shape
    depth = w_in.shape[0]
    C = ctx.shape[1]
    alpha = (2 * depth) ** 0.25
    assert B + 1 <= 8
    cc = jnp.concatenate([c, c_ctx[None], jnp.zeros((8 - B - 1, D), F32)], axis=0)
    mod_all = _mod_call(cc, w_mod, b_mod)
    tabs, tabs_ctx = _rope_tables(n, C)
    xc = ctx
    for l in range(depth):
        lw = _prep_layer(l, p)
        need_ctx = l < depth - 1
        mods = [mod_all[l, :B, k * D:(k + 1) * D][:, None, :] for k in range(6)]
        mods_c = [jnp.broadcast_to(mod_all[l, B, k * D:(k + 1) * D][None, None, :], (B, 1, D)) for k in range(6)]
        pl_ = _inproj_call(x, mods[0], mods[1], lw, tabs)
        pc = _inproj_call(xc, mods_c[0], mods_c[1], lw, tabs_ctx)
        (naq, nak, nav, bq, bk, bv, cq, ck, cv, dq, dk, dv) = pl_
        (_, nakc, navc, _, bkc, bvc, _, ckc, cvc, _, dkc, dvc) = pc
        o_a = _na_call(naq, nak, nav, nakc, navc, lw["tl"], lw["tr"])
        o_b = _flash_call(bq, bk[:, None], bv, bkc[:, None], bvc, per_pair=False)
        o_c = _win_call(lw["sink"], cq, ck, cv, ckc, cvc)
        o_d = _flash_call(dq, dk, dv, dkc, dvc, per_pair=True)
        x1, h2, aff = _merge_call(x, (o_a, o_b, o_c, o_d), mods[:5], lw, alpha)
        x = _expert_ffn_ln(x1, h2, aff, mods[5], lw, alpha)
        if need_ctx:
            oc = _ctx_attn_call(lw["sink"], pc)
            xc1, hc2, affc = _merge_call(xc, oc, mods_c[:5], lw, alpha)
            xc = _expert_ffn_ln(xc1, hc2, affc, mods_c[5], lw, alpha)
    return x
```

```python
import functools

import numpy as np
import jax
import jax.numpy as jnp
from jax import lax
from jax.experimental import pallas as pl
from jax.experimental.pallas import tpu as pltpu

F32 = jnp.float32
BF16 = jnp.bfloat16

GRID_W = 64
HEAD_DIM = 64
ROPE_THETA = 10000.0
EPS = 1e-6
NEG_INF = -1e30
NA_HEADS = 4
NA_KH = 8
NA_KW = 16
NA_WIN_ROWS = 10
SWA_WINDOW = 128
QBLOCK = 128
MLA_HEADS = 4
MLA_NOPE = 64
MLA_ROPE = 32
MLA_V = 64
MLA_Q_LORA = 192
MLA_KV_LORA = 128
N_BRANCH = 4
BRANCH_W = 256
N_EXPERTS = 16
CAPACITY = 2
LANES = 128
TOKEN_BLOCK = 128
VMEM_LIMIT = 56 * 1024 * 1024
HEAD_PERM = (0, 2, 1, 3)
VT_ROWS = 80
LOG2E = 1.4426950408889634
DENSE_KEY_CHUNK = 1024
LOCAL_BLOCKS_PER_STEP = 2


def _cparams(*sem):
    return pltpu.CompilerParams(dimension_semantics=sem, vmem_limit_bytes=VMEM_LIMIT)


def _dot(a, b):
    return jnp.dot(a, b, preferred_element_type=F32)


def _dot_nt(a, b):
    return lax.dot_general(a, b, (((1,), (1,)), ((), ())), preferred_element_type=F32)


def _ln(x):
    mu = jnp.mean(x, axis=-1, keepdims=True)
    xc = x - mu
    var = jnp.mean(xc * xc, axis=-1, keepdims=True)
    return xc * lax.rsqrt(var + EPS)


def _modulate(x, shift, scale):
    return _ln(x) * (1.0 + scale) + shift


def _lane(shape, dim=None):
    return lax.broadcasted_iota(jnp.int32, shape, len(shape) - 1 if dim is None else dim)


def _rope(y, cos, sin_signed):
    return y * cos + pltpu.roll(y, HEAD_DIM, 1) * sin_signed


def _mod_kernel(c_ref, w_ref, b_ref, o_ref):
    c = c_ref[...]
    s = c / (1.0 + jnp.exp(-c))
    o_ref[0] = _dot(s.astype(BF16), w_ref[0].astype(BF16)) + b_ref[0]


def _mod_call(cc, w_mod, b_mod):
    L, D, N = w_mod.shape
    tn = N // 4
    return pl.pallas_call(
        _mod_kernel,
        grid=(L, N // tn),
        in_specs=[pl.BlockSpec((8, D), lambda l, j: (0, 0)),
                  pl.BlockSpec((1, D, tn), lambda l, j: (l, 0, j)),
                  pl.BlockSpec((1, 1, tn), lambda l, j: (l, 0, j))],
        out_specs=pl.BlockSpec((1, 8, tn), lambda l, j: (l, 0, j)),
        out_shape=jax.ShapeDtypeStruct((L, 8, N), F32),
        compiler_params=_cparams("arbitrary", "arbitrary"),
        name="mod_vectors",
    )(cc, w_mod, b_mod.reshape(L, 1, N))


_C_NA = 0
_C_BQ = 768
_C_BKV = 1024
_C_CQ = 1280
_C_CKV = 1536
_C_DQ = 1792
_C_DKV = 2048
_C_END = 2304


def _inproj_kernel(x_ref, sh_ref, sc_ref, w_ref, cos_ref, sin_ref, cosd_ref, sind_ref,
                   gq_ref, gk_ref, gmq_ref, gmkv_ref, wqb_ref, wkvb_ref,
                   naq, nak, nav, bq, bk, bv, cq, ck, cv, dq, dk, dv):
    hb = _modulate(x_ref[0], sh_ref[0], sc_ref[0]).astype(BF16)

    def seg(a, b):
        return _dot(hb, w_ref[:, a:b])

    cos = cos_ref[...]
    sin = sin_ref[...]
    cosd = cosd_ref[...]
    sind = sind_ref[...]
    qscale = HEAD_DIM ** -0.5 * LOG2E

    def put_values_blocks(ref, v, nheads):
        vt = v.T
        ones = jnp.ones((VT_ROWS - HEAD_DIM, TOKEN_BLOCK), F32)
        for g in range(vt.shape[1] // TOKEN_BLOCK):
            for h in range(nheads):
                blk = vt[h * HEAD_DIM:(h + 1) * HEAD_DIM, g * TOKEN_BLOCK:(g + 1) * TOKEN_BLOCK]
                ref[0, g, h] = jnp.concatenate([blk, ones], axis=0).astype(BF16)

    def put_values_t(ref, pair, v):
        vt = v.T
        ones = jnp.ones((VT_ROWS - HEAD_DIM, vt.shape[1]), F32)
        for s in range(2):
            ref[0, pair, 0, s] = jnp.concatenate([vt[s * HEAD_DIM:(s + 1) * HEAD_DIM], ones], axis=0).astype(BF16)

    z = seg(_C_NA, _C_NA + 768)
    naq[0] = (z[:, 0:256] * qscale).astype(BF16)
    nak[0] = z[:, 256:512].astype(BF16)
    put_values_blocks(nav, z[:, 512:768], NA_HEADS)

    slot_a = (_lane((1, LANES)) & 32) == 0

    def slabs(y):
        return [jnp.where(slot_a, y, 0.0), jnp.where(slot_a, 0.0, y)]

    def head_rms(z, gain):
        z2 = z * z
        ss_a = jnp.sum(jnp.where(slot_a, z2, 0.0), axis=-1, keepdims=True)
        ss_b = jnp.sum(jnp.where(slot_a, 0.0, z2), axis=-1, keepdims=True)
        inv = jnp.where(slot_a, lax.rsqrt(ss_a * (1.0 / HEAD_DIM) + EPS), lax.rsqrt(ss_b * (1.0 / HEAD_DIM) + EPS))
        return z * inv * gain

    z = seg(_C_BQ, _C_BKV)
    for c in range(2):
        y = _rope(head_rms(z[:, c * LANES:(c + 1) * LANES], gq_ref[...]), cos, sin) * qscale
        for j, slab in enumerate(slabs(y)):
            bq[0, 2 * c + j] = slab.astype(BF16)
    kv = seg(_C_BKV, _C_CQ)
    bk[0] = _rope(head_rms(kv[:, :LANES], gk_ref[...]), cos, sin).astype(BF16)
    put_values_t(bv, 0, kv[:, LANES:])

    z = seg(_C_CQ, _C_CKV)
    for c in range(2):
        y = _rope(z[:, c * LANES:(c + 1) * LANES], cos, sin) * qscale
        for j, slab in enumerate(slabs(y)):
            cq[0, 2 * c + j] = slab.astype(BF16)
    kv = seg(_C_CKV, _C_DQ)
    ck[0] = _rope(kv[:, :LANES], cos, sin).astype(BF16)
    put_values_blocks(cv, kv[:, LANES:], 2)

    z = seg(_C_DQ, _C_DKV)
    ss = jnp.sum(z * z, axis=-1, keepdims=True) * (1.0 / MLA_Q_LORA)
    cqn = (z * lax.rsqrt(ss + EPS) * gmq_ref[...]).astype(BF16)
    qd = _dot(cqn, wqb_ref[...])
    dscale = (MLA_NOPE + MLA_ROPE) ** -0.5 * LOG2E
    for h in range(MLA_HEADS):
        nope = qd[:, h * 256:h * 256 + LANES] * dscale
        rot = _rope(qd[:, h * 256 + LANES:(h + 1) * 256], cosd, sind) * dscale
        dq[0, h] = jnp.concatenate([nope, rot], axis=-1).astype(BF16)
    zz = seg(_C_DKV, _C_END)
    z = zz[:, :LANES]
    ss = jnp.sum(z * z, axis=-1, keepdims=True) * (1.0 / MLA_KV_LORA)
    ckvn = (z * lax.rsqrt(ss + EPS) * gmkv_ref[...]).astype(BF16)
    kv = _dot(ckvn, wkvb_ref[...])
    kr = _rope(zz[:, LANES:], cosd, sind)
    for p in range(2):
        dk[0, p] = jnp.concatenate([kv[:, p * LANES:(p + 1) * LANES], kr], axis=-1).astype(BF16)
        put_values_t(dv, p, kv[:, 256 + p * LANES:256 + (p + 1) * LANES])


def _inproj_call(x, shift, scale, lw, tabs):
    B, n, D = x.shape
    tm = min(512, n)
    cos, sin, cosd, sind = tabs
    row = lambda b, i: (b, i, 0)
    hrow = lambda b, i: (b, 0, i, 0)
    const2 = lambda b, i: (0, 0)
    tab = pl.BlockSpec((tm, LANES), lambda b, i: (i, 0))
    mod = pl.BlockSpec((1, 1, D), lambda b, i: (b, 0, 0))
    nt = n // tm
    vt_spec = lambda pairs: ((B, pairs, nt, 2, VT_ROWS, tm),
                             pl.BlockSpec((1, pairs, 1, 2, VT_ROWS, tm), lambda b, i: (b, 0, i, 0, 0, 0)))
    slab = lambda kd: ((B, 4, n, kd), pl.BlockSpec((1, 4, tm, kd), hrow))
    tokm = lambda w: ((B, n, w), pl.BlockSpec((1, tm, w), row))
    vblk = lambda heads: ((B, n // TOKEN_BLOCK, heads, VT_ROWS, TOKEN_BLOCK),
                          pl.BlockSpec((1, tm // TOKEN_BLOCK, heads, VT_ROWS, TOKEN_BLOCK),
                                       lambda b, i: (b, i, 0, 0, 0)))
    outs = [tokm(256), tokm(256), vblk(NA_HEADS)]
    outs += [slab(LANES), tokm(LANES), vt_spec(1)]
    outs += [slab(LANES), tokm(LANES), vblk(2)]
    outs += [slab(256), ((B, 2, n, 256), pl.BlockSpec((1, 2, tm, 256), hrow)), vt_spec(2)]
    return pl.pallas_call(
        _inproj_kernel,
        grid=(B, n // tm),
        in_specs=[pl.BlockSpec((1, tm, D), row), mod, mod,
                  pl.BlockSpec((D, _C_END), const2), tab, tab, tab, tab,
                  pl.BlockSpec((1, LANES), const2), pl.BlockSpec((1, LANES), const2),
                  pl.BlockSpec((1, 256), const2), pl.BlockSpec((1, LANES), const2),
                  pl.BlockSpec((256, 1024), const2), pl.BlockSpec((LANES, 512), const2)],
        out_specs=[o[1] for o in outs],
        out_shape=[jax.ShapeDtypeStruct(o[0], BF16) for o in outs],
        compiler_params=_cparams("arbitrary", "arbitrary"),
        name="in_proj",
    )(x, shift, scale, lw["w_qkv"], cos, sin, cosd, sind,
      lw["gq"], lw["gk"], lw["gmq"], lw["gmkv"], lw["wqb"], lw["wkvb"])


def _flash_kernel(q_ref, k_ref, vt_ref, kc_ref, vct_ref, o_ref, acc_ref, m_ref, s_ref, *, tq, nkk, vpc, nq):
    kd = q_ref.shape[-1]
    ctx_len = kc_ref.shape[3]

    def load_q(i):
        return q_ref[0, :, pl.ds(pl.multiple_of(i * tq, tq), tq), :].reshape(2 * tq, kd)

    def update(s, values):
        m_prev = m_ref[...]
        m_new = jnp.maximum(m_prev, jnp.max(s, axis=0, keepdims=True))
        a = jnp.exp2(m_prev - m_new)
        p = jnp.exp2(s - m_new).astype(BF16)
        for h in range(2):
            cols = slice(h * tq, (h + 1) * tq)
            pv, row = None, 0
            for vt in values:
                term = _dot(vt[h], p[row:row + vt.shape[-1], cols])
                pv = term if pv is None else pv + term
                row += vt.shape[-1]
            acc_ref[h] = a[:, cols] * acc_ref[h] + pv
        m_ref[...] = m_new

    def query_block(i, buf):
        q = load_q(i)
        m_ref[...] = jnp.full(m_ref.shape, NEG_INF, F32)
        acc_ref[...] = jnp.zeros(acc_ref.shape, F32)
        for j in range(nkk):
            cur, nxt = (buf + j) % 2, (buf + j + 1) % 2
            if j + 1 < nkk:
                s_ref[nxt] = _dot_nt(k_ref[0, 0, j + 1], q)
            else:
                s_ref[nxt, :ctx_len] = _dot_nt(kc_ref[0, 0, 0], q)
            update(s_ref[cur], [vt_ref[0, 0, vpc * j + u] for u in range(vpc)])
        s_ref[1 - buf] = _dot_nt(k_ref[0, 0, 0], load_q(jnp.minimum(i + 1, nq - 1)))
        update(s_ref[buf, :ctx_len], [vct_ref[0, 0, 0]])
        o = [acc_ref[h][:HEAD_DIM] / acc_ref[h][HEAD_DIM:HEAD_DIM + 1] for h in range(2)]
        o_ref[0, pl.ds(pl.multiple_of(i * tq, tq), tq), :] = jnp.concatenate(o, axis=0).T.astype(BF16)

    s_ref[0] = _dot_nt(k_ref[0, 0, 0], load_q(0))

    def two_blocks(ii, carry):
        query_block(2 * ii, 0)
        query_block(2 * ii + 1, 1)
        return carry

    lax.fori_loop(0, nq // 2, two_blocks, 0)


def _flash_call(q, k, vt, kc, vct, *, per_pair):
    B, _, n, kd = q.shape
    P = k.shape[1]
    C = kc.shape[2]
    nv, tv = vt.shape[2], vt.shape[5]
    tkk = max(tv, min(DENSE_KEY_CHUNK, n // 2))
    nkk, vpc = n // tkk, tkk // tv
    tq = min(512, n)
    nq = n // tq
    assert nq % 2 == 0 and nkk % 2 == 0 and nkk * vpc == nv and C <= tkk
    pidx = (lambda b, p: (b, p, 0, 0, 0)) if per_pair else (lambda b, p: (b, 0, 0, 0, 0))
    vidx = (lambda b, p: (b, p, 0, 0, 0, 0)) if per_pair else (lambda b, p: (b, 0, 0, 0, 0, 0))
    kern = functools.partial(_flash_kernel, tq=tq, nkk=nkk, vpc=vpc, nq=nq)
    return pl.pallas_call(
        kern,
        grid=(B, 2),
        in_specs=[pl.BlockSpec((1, 2, n, kd), lambda b, p: (b, p, 0, 0)),
                  pl.BlockSpec((1, 1, nkk, tkk, kd), pidx),
                  pl.BlockSpec((1, 1, nv, 2, VT_ROWS, tv), vidx),
                  pl.BlockSpec((1, 1, 1, C, kd), pidx),
                  pl.BlockSpec((1, 1, 1, 2, VT_ROWS, C), vidx)],
        out_specs=pl.BlockSpec((1, n, LANES), lambda b, p: (b, 0, p)),
        out_shape=jax.ShapeDtypeStruct((B, n, 256), BF16),
        scratch_shapes=[pltpu.VMEM((2, VT_ROWS, tq), F32), pltpu.VMEM((1, 2 * tq), F32),
                        pltpu.VMEM((2, tkk, 2 * tq), F32)],
        compiler_params=_cparams("arbitrary", "arbitrary"),
        name="dense_attn",
    )(q, k.reshape(B, P, nkk, tkk, kd), vt, kc[:, :, None], vct)


def _local_values(vt_ref, g0, blocks, vct_ref, head):
    lat = jnp.concatenate([vt_ref[0, g0 + g, head] for g in range(blocks)], axis=1)
    ctx = jnp.concatenate([vct_ref[0, g, head] for g in range(vct_ref.shape[1])], axis=1)
    return lat, ctx


def _na_kernel(q_ref, k_ref, vt_ref, kc_ref, vct_ref, tl_ref, tr_ref, o_ref, *, rows):
    for sub in range(LOCAL_BLOCKS_PER_STEP):
        rows_q = slice(sub * 2 * GRID_W, (sub + 1) * 2 * GRID_W)
        o_ref[0, rows_q, :] = _na_block(LOCAL_BLOCKS_PER_STEP * pl.program_id(1) + sub, q_ref[0, rows_q, :],
                                        k_ref, vt_ref, kc_ref, vct_ref, tl_ref, tr_ref, rows)


def _na_block(i, q, k_ref, vt_ref, kc_ref, vct_ref, tl_ref, tr_ref, rows):
    r0 = 2 * i
    w0 = jnp.clip(r0 - NA_KH // 2, 0, rows - NA_WIN_ROWS)
    win = NA_WIN_ROWS * GRID_W
    start = pl.multiple_of(w0 * GRID_W, 2 * GRID_W)
    k_all = jnp.concatenate([k_ref[0, pl.ds(start, win), :], kc_ref[0]], axis=0)
    head_of_lane = _lane((1, 256)) >> 6
    q_stack = jnp.concatenate([jnp.where(head_of_lane == h, q, jnp.zeros_like(q)) for h in range(NA_HEADS)], axis=0)
    s = _dot_nt(k_all, q_stack)

    def table_index(a, j):
        qr = r0 + a
        kr = w0 + j
        st = jnp.clip(qr - NA_KH // 2, 0, rows - NA_KH)
        ok = (kr >= st) & (kr < st + NA_KH)
        return jnp.where(ok, kr - qr + NA_KH - 1, 2 * NA_KH - 1)

    idx = [[table_index(a, j) for j in range(NA_WIN_ROWS)] for a in range(2)]
    bias = jnp.concatenate(
        [jnp.concatenate([tl_ref[h * 16 + idx[0][j]] + tr_ref[h * 16 + idx[1][j]] for j in range(NA_WIN_ROWS)], axis=0)
         for h in range(NA_HEADS)], axis=1)
    s_lat = s[:win] + bias
    s_ctx = s[win:]
    m = jnp.maximum(jnp.max(s_lat, axis=0, keepdims=True), jnp.max(s_ctx, axis=0, keepdims=True))
    p_lat = jnp.exp2(s_lat - m).astype(BF16)
    p_ctx = jnp.exp2(s_ctx - m).astype(BF16)
    outs = []
    for h in range(NA_HEADS):
        cols = slice(h * LANES, (h + 1) * LANES)
        v_lat, v_ctx = _local_values(vt_ref, w0 >> 1, win // TOKEN_BLOCK, vct_ref, h)
        o = _dot(v_lat, p_lat[:, cols]) + _dot(v_ctx, p_ctx[:, cols])
        outs.append(o[:HEAD_DIM] / o[HEAD_DIM:HEAD_DIM + 1])
    return jnp.concatenate(outs, axis=0).T.astype(BF16)


def _na_call(q, k, vt, kc, vct, tl, tr):
    B, n, _ = q.shape
    C = kc.shape[1]
    rows = n // GRID_W
    step_rows = 2 * LOCAL_BLOCKS_PER_STEP
    assert rows >= NA_WIN_ROWS and rows % step_rows == 0
    full = lambda b, i: (b, 0, 0)
    full5 = lambda b, i: (b, 0, 0, 0, 0)
    return pl.pallas_call(
        functools.partial(_na_kernel, rows=rows),
        grid=(B, rows // step_rows),
        in_specs=[pl.BlockSpec((1, step_rows * GRID_W, 256), lambda b, i: (b, i, 0)),
                  pl.BlockSpec((1, n, 256), full), pl.BlockSpec((1,) + vt.shape[1:], full5),
                  pl.BlockSpec((1, C, 256), full), pl.BlockSpec((1,) + vct.shape[1:], full5),
                  pl.BlockSpec(tl.shape, lambda b, i: (0, 0, 0)),
                  pl.BlockSpec(tr.shape, lambda b, i: (0, 0, 0))],
        out_specs=pl.BlockSpec((1, step_rows * GRID_W, 256), lambda b, i: (b, i, 0)),
        out_shape=jax.ShapeDtypeStruct((B, n, 256), BF16),
        compiler_params=_cparams("arbitrary", "arbitrary"),
        name="nbr_attn",
    )(q, k, vt, kc, vct, tl, tr)


def _win_kernel(sink_ref, q_ref, k_ref, vt_ref, kc_ref, vct_ref, o_ref, *, n):
    for sub in range(LOCAL_BLOCKS_PER_STEP):
        rows_q = slice(sub * QBLOCK, (sub + 1) * QBLOCK)
        o_ref[0, rows_q, :] = _win_block(LOCAL_BLOCKS_PER_STEP * pl.program_id(1) + sub, q_ref[0, :, rows_q, :],
                                         sink_ref, k_ref, vt_ref, kc_ref, vct_ref, n)


def _win_block(i, q, sink_ref, k_ref, vt_ref, kc_ref, vct_ref, n):
    band = 3 * QBLOCK
    ws = pl.multiple_of(jnp.clip((i - 1) * QBLOCK, 0, n - band), QBLOCK)
    k_all = jnp.concatenate([k_ref[0, pl.ds(ws, band), :], kc_ref[0]], axis=0)
    q_stack = q.reshape(4 * QBLOCK, LANES)
    s = _dot_nt(k_all, q_stack)
    kpos = ws + lax.broadcasted_iota(jnp.int32, (band, QBLOCK), 0)
    qpos = i * QBLOCK + lax.broadcasted_iota(jnp.int32, (band, QBLOCK), 1)
    ok = jnp.abs(qpos - kpos) <= SWA_WINDOW
    s_lat = jnp.concatenate([jnp.where(ok, s[:band, j * QBLOCK:(j + 1) * QBLOCK], NEG_INF) for j in range(4)], axis=1)
    s_ctx = s[band:]
    snk = jnp.concatenate([jnp.full((1, QBLOCK), sink_ref[j], F32) for j in range(4)], axis=1)
    m = jnp.maximum(jnp.maximum(jnp.max(s_lat, axis=0, keepdims=True), jnp.max(s_ctx, axis=0, keepdims=True)), snk)
    p_lat = jnp.exp2(s_lat - m).astype(BF16)
    p_ctx = jnp.exp2(s_ctx - m).astype(BF16)
    p_snk = jnp.exp2(snk - m)
    g0 = ws >> 7
    outs = []
    for j in range(4):
        cols = slice(j * QBLOCK, (j + 1) * QBLOCK)
        v_lat, v_ctx = _local_values(vt_ref, g0, band // TOKEN_BLOCK, vct_ref, HEAD_PERM[j] // 2)
        o = _dot(v_lat, p_lat[:, cols]) + _dot(v_ctx, p_ctx[:, cols])
        outs.append(o[:HEAD_DIM] / (o[HEAD_DIM:HEAD_DIM + 1] + p_snk[:, cols]))
    return jnp.concatenate(outs, axis=0).T.astype(BF16)


def _win_call(sink, q, k, vt, kc, vct):
    B, _, n, _ = q.shape
    C = kc.shape[1]
    step = LOCAL_BLOCKS_PER_STEP * QBLOCK
    assert n >= 3 * QBLOCK and QBLOCK == TOKEN_BLOCK and n % step == 0
    full = lambda b, i, s: (b, 0, 0)
    full5 = lambda b, i, s: (b, 0, 0, 0, 0)
    return pl.pallas_call(
        functools.partial(_win_kernel, n=n),
        grid_spec=pltpu.PrefetchScalarGridSpec(
            num_scalar_prefetch=1,
            grid=(B, n // step),
            in_specs=[pl.BlockSpec((1, 4, step, LANES), lambda b, i, s: (b, 0, i, 0)),
                      pl.BlockSpec((1, n, LANES), full), pl.BlockSpec((1,) + vt.shape[1:], full5),
                      pl.BlockSpec((1, C, LANES), full), pl.BlockSpec((1,) + vct.shape[1:], full5)],
            out_specs=pl.BlockSpec((1, step, 256), lambda b, i, s: (b, i, 0))),
        out_shape=jax.ShapeDtypeStruct((B, n, 256), BF16),
        compiler_params=_cparams("arbitrary", "arbitrary"),
        name="window_attn",
    )(sink, q, k, vt, kc, vct)


def _attend(q, k, vt, sink=None):
    s = _dot_nt(q, k)
    m = jnp.max(s, axis=-1, keepdims=True)
    if sink is not None:
        m = jnp.maximum(m, sink)
    p = jnp.exp2(s - m)
    l = jnp.sum(p, axis=-1, keepdims=True)
    if sink is not None:
        l = l + jnp.exp2(sink - m)
    return _dot_nt(p.astype(BF16), vt) / l


def _ctx_attn_kernel(sink_ref, naq, nak, nav, bq, bk, bv, cq, ck, cv, dq, dk, dv, oa, ob, oc, od):
    left = _lane((1, LANES)) < HEAD_DIM
    head_of_lane = _lane((1, 256)) >> 6

    def pair_out(o):
        return jnp.concatenate([jnp.where(left, o[0], o[1]), jnp.where(left, o[2], o[3])], axis=-1).astype(BF16)

    def blocks_t(ref, heads):
        return jnp.concatenate(
            [jnp.concatenate([ref[0, g, h][:HEAD_DIM] for g in range(ref.shape[1])], axis=1)
             for h in range(heads)], axis=0)

    def both_heads(vt):
        return jnp.concatenate([vt[0, :HEAD_DIM], vt[1, :HEAD_DIM]], axis=0)

    q = naq[0]
    v_na = blocks_t(nav, NA_HEADS)
    out = jnp.zeros(q.shape, F32)
    for h in range(NA_HEADS):
        qm = jnp.where(head_of_lane == h, q, jnp.zeros_like(q))
        out = jnp.where(head_of_lane == h, _attend(qm, nak[0], v_na), out)
    oa[0] = out.astype(BF16)
    ob[0] = pair_out([_attend(bq[0, s], bk[0], both_heads(bv[0, 0, 0])) for s in range(4)])
    v_c = blocks_t(cv, 2)
    oc[0] = pair_out([_attend(cq[0, s], ck[0], v_c, sink_ref[s]) for s in range(4)])
    od[0] = pair_out([_attend(dq[0, h], dk[0, h // 2], both_heads(dv[0, h // 2, 0])) for h in range(MLA_HEADS)])


def _ctx_attn_call(sink, pc):
    B, C, _ = pc[0].shape
    names = pc
    specs = []
    for a in names:
        nd = a.ndim
        specs.append(pl.BlockSpec((1,) + a.shape[1:], (lambda b, s, nd=nd: (b,) + (0,) * (nd - 1))))
    out_spec = pl.BlockSpec((1, C, 256), lambda b, s: (b, 0, 0))
    return pl.pallas_call(
        _ctx_attn_kernel,
        grid_spec=pltpu.PrefetchScalarGridSpec(
            num_scalar_prefetch=1, grid=(B,), in_specs=specs, out_specs=[out_spec] * 4),
        out_shape=[jax.ShapeDtypeStruct((B, C, 256), BF16)] * 4,
        compiler_params=_cparams("arbitrary"),
        name="ctx_attn",
    )(sink, *pc)


def _merge_kernel(x_ref, oa, ob, oc, od, sha, sca, ga, shf, scf, wg_ref, wb_ref, wo_ref,
                  l1g, l1b, wrh_ref, wrl_ref, x1_ref, h2_ref, aff_ref, *, alpha):
    x = x_ref[0]
    D = x.shape[-1]
    hb = _modulate(x, sha[0], sca[0]).astype(BF16)
    merged = None
    for i, o in enumerate((oa, ob, oc, od)):
        g = 1.0 / (1.0 + jnp.exp(-_dot(hb, wg_ref[:, i * D:(i + 1) * D])))
        term = g * _dot(o[0], wb_ref[i])
        merged = term if merged is None else merged + term
    y = _dot(merged.astype(BF16), wo_ref[...])
    x1 = _ln(alpha * x + ga[0] * y) * l1g[...] + l1b[...]
    x1_ref[0] = x1
    h2 = _modulate(x1, shf[0], scf[0])
    h2_hi = h2.astype(BF16)
    h2_ref[0] = h2_hi
    h2_lo = (h2 - h2_hi.astype(F32)).astype(BF16)
    logits = _dot(h2_hi, wrh_ref[...]) + _dot(h2_hi, wrl_ref[...]) + _dot(h2_lo, wrh_ref[...])
    logits = jnp.where(_lane((1, LANES)) < N_EXPERTS, logits, NEG_INF)
    e = jnp.exp(logits - jnp.max(logits, axis=-1, keepdims=True))
    aff_t = (e / jnp.sum(e, axis=-1, keepdims=True)).T
    for k in range(aff_ref.shape[1]):
        aff_ref[0, k] = aff_t[:N_EXPERTS, k * TOKEN_BLOCK:(k + 1) * TOKEN_BLOCK]


def _merge_call(x, outs, mods, lw, alpha):
    B, n, D = x.shape
    tm = min(256, n)
    nb = tm // TOKEN_BLOCK
    row = lambda b, i: (b, i, 0)
    mod = pl.BlockSpec((1, 1, D), lambda b, i: (b, 0, 0))
    c2 = lambda b, i: (0, 0)
    obr = pl.BlockSpec((1, tm, 256), row)
    return pl.pallas_call(
        functools.partial(_merge_kernel, alpha=alpha),
        grid=(B, n // tm),
        in_specs=[pl.BlockSpec((1, tm, D), row), obr, obr, obr, obr, mod, mod, mod, mod, mod,
                  pl.BlockSpec((D, N_BRANCH * D), c2),
                  pl.BlockSpec((N_BRANCH, BRANCH_W, D), lambda b, i: (0, 0, 0)),
                  pl.BlockSpec((D, D), c2), pl.BlockSpec((1, D), c2), pl.BlockSpec((1, D), c2),
                  pl.BlockSpec((D, LANES), c2), pl.BlockSpec((D, LANES), c2)],
        out_specs=[pl.BlockSpec((1, tm, D), row), pl.BlockSpec((1, tm, D), row),
                   pl.BlockSpec((1, nb, N_EXPERTS, TOKEN_BLOCK), lambda b, i: (b, i, 0, 0))],
        out_shape=[jax.ShapeDtypeStruct((B, n, D), F32), jax.ShapeDtypeStruct((B, n, D), BF16),
                   jax.ShapeDtypeStruct((B, n // TOKEN_BLOCK, N_EXPERTS, TOKEN_BLOCK), F32)],
        compiler_params=_cparams("arbitrary", "arbitrary"),
        name="merge_router",
    )(x, *outs, *mods, lw["w_gates"], lw["w_branch"], lw["w_out"], lw["ln1_g"], lw["ln1_b"],
      lw["wr_hi"], lw["wr_lo"])


def _select_kernel(aff_ref, pos_ref, off_ref, tlo_ref, thi_ref, *, cap, slot_tile):
    a = aff_ref[0]
    nb = a.shape[0]
    rows = nb * N_EXPERTS
    bits = lax.bitcast_convert_type(a, jnp.int32)
    capf = jnp.float32(cap)

    def count(mask):
        c = jnp.sum(jnp.where(mask, 1.0, 0.0), axis=0)
        return jnp.broadcast_to(jnp.sum(c, axis=-1, keepdims=True), c.shape)

    def search(it, lo):
        cand = lo | lax.shift_left(jnp.int32(1), 30 - it)
        return jnp.where(count(bits >= cand[None]) >= capf, cand, lo)

    thr = lax.fori_loop(0, 31, search, jnp.zeros((N_EXPERTS, TOKEN_BLOCK), jnp.int32))

    r = lax.broadcasted_iota(jnp.int32, (rows, rows), 0)
    c = lax.broadcasted_iota(jnp.int32, (rows, rows), 1)
    earlier = jnp.where(((r & (N_EXPERTS - 1)) == (c & (N_EXPERTS - 1))) & ((c >> 4) < (r >> 4)), 1.0, 0.0).astype(BF16)
    ti = lax.broadcasted_iota(jnp.int32, (TOKEN_BLOCK, TOKEN_BLOCK), 0)
    tj = lax.broadcasted_iota(jnp.int32, (TOKEN_BLOCK, TOKEN_BLOCK), 1)
    tri = jnp.where(ti <= tj, 1.0, 0.0).astype(BF16)
    ones = jnp.ones((TOKEN_BLOCK, TOKEN_BLOCK), BF16)

    def prefix(mask):
        m2 = jnp.where(mask, 1.0, 0.0).reshape(rows, TOKEN_BLOCK)
        mb = m2.astype(BF16)
        within = _dot(mb, tri)
        tot = _dot(mb, ones)
        off = _dot(earlier, tot.astype(BF16))
        shp = (nb, N_EXPERTS, TOKEN_BLOCK)
        return (off + within - m2).reshape(shp), off.reshape(shp), tot.reshape(shp)

    gt = bits > thr[None]
    eq = bits == thr[None]
    need = capf - count(gt)
    eq_rank, _, _ = prefix(eq)
    sel = gt | (eq & (eq_rank < need[None]))
    excl, off, tot = prefix(sel)
    pos_ref[0] = jnp.where(sel, excl, -1.0)
    off_ref[0] = off.astype(jnp.int32)
    tile_start = (_lane((1, 1, TOKEN_BLOCK)) * slot_tile).astype(F32)
    tlo_ref[0] = jnp.sum(jnp.where(off + tot <= tile_start, 1, 0), axis=0).astype(jnp.int32)
    thi_ref[0] = jnp.sum(jnp.where(off < tile_start + slot_tile, 1, 0), axis=0).astype(jnp.int32)


def _select_call(aff, cap, slot_tile):
    B, nb, E, _ = aff.shape
    blk = pl.BlockSpec((1, nb, E, TOKEN_BLOCK), lambda b: (b, 0, 0, 0))
    rng = pl.BlockSpec((1, E, TOKEN_BLOCK), lambda b: (b, 0, 0))
    return pl.pallas_call(
        functools.partial(_select_kernel, cap=cap, slot_tile=slot_tile),
        grid=(B,),
        in_specs=[blk],
        out_specs=[blk, blk, rng, rng],
        out_shape=[jax.ShapeDtypeStruct(aff.shape, F32), jax.ShapeDtypeStruct(aff.shape, jnp.int32),
                   jax.ShapeDtypeStruct((B, E, TOKEN_BLOCK), jnp.int32),
                   jax.ShapeDtypeStruct((B, E, TOKEN_BLOCK), jnp.int32)],
        compiler_params=_cparams("arbitrary"),
        name="expert_select",
    )(aff)


def _gather_kernel(tlo_ref, thi_ref, pos_ref, aff_ref, h_ref, xg_ref, g_ref, acc_ref, gacc_ref, *,
                   slot_tile, group, in_flight):
    b = pl.program_id(0)
    e = pl.program_id(1)
    n_tiles = xg_ref.shape[2] // slot_tile
    for t in range(n_tiles):
        acc_ref[...] = jnp.zeros(acc_ref.shape, F32)
        gacc_ref[...] = jnp.zeros(gacc_ref.shape, F32)
        width = group * TOKEN_BLOCK
        slot = (t * slot_tile + lax.broadcasted_iota(jnp.int32, (slot_tile, width), 0)).astype(F32)
        shift = group.bit_length() - 1
        lo = tlo_ref[b, e, t] >> shift
        hi = (thi_ref[b, e, t] + group - 1) >> shift
        last = pos_ref.shape[1] // group - 1

        def one_product(pg, valid):
            p = jnp.concatenate([pos_ref[0, group * pg + u, pl.ds(e, 1), :] for u in range(group)], axis=-1)
            a = jnp.concatenate([aff_ref[0, group * pg + u, pl.ds(e, 1), :] for u in range(group)], axis=-1)
            hit = p == slot
            tok = pl.multiple_of(pg * width, width)
            rows = _dot(jnp.where(hit, valid, 0.0).astype(BF16), h_ref[0, pl.ds(tok, width), :])
            ga = jnp.where(hit, a * valid, 0.0)
            return rows, sum(ga[:, u * TOKEN_BLOCK:(u + 1) * TOKEN_BLOCK] for u in range(group))

        def body(it, carry):
            rows, gates = None, None
            for u in range(in_flight):
                pg = lo + in_flight * it + u
                r, g = one_product(jnp.minimum(pg, last), jnp.where(pg < hi, 1.0, 0.0).astype(F32))
                rows = r if rows is None else rows + r
                gates = g if gates is None else gates + g
            acc_ref[...] += rows
            gacc_ref[...] += gates
            return carry

        lax.fori_loop(0, lax.div(hi - lo + in_flight - 1, in_flight), body, 0)
        xg_ref[0, 0, t * slot_tile:(t + 1) * slot_tile, :] = acc_ref[...].astype(BF16)
        g_ref[0, 0, t * slot_tile:(t + 1) * slot_tile, :] = jnp.sum(gacc_ref[...], axis=-1, keepdims=True)


def _gather_call(tlo, thi, pos, aff, h2, cap_pad, slot_tile):
    B, n, D = h2.shape
    nb = n // TOKEN_BLOCK
    blk = pl.BlockSpec((1, nb, N_EXPERTS, TOKEN_BLOCK), lambda b, e, *_: (b, 0, 0, 0))
    return pl.pallas_call(
        functools.partial(_gather_kernel, slot_tile=slot_tile, group=2, in_flight=min(5, nb // 2)),
        grid_spec=pltpu.PrefetchScalarGridSpec(
            num_scalar_prefetch=2,
            grid=(B, N_EXPERTS),
            in_specs=[blk, blk, pl.BlockSpec((1, n, D), lambda b, e, *_: (b, 0, 0))],
            out_specs=[pl.BlockSpec((1, 1, cap_pad, D), lambda b, e, *_: (b, e, 0, 0)),
                       pl.BlockSpec((1, 1, cap_pad, 1), lambda b, e, *_: (b, e, 0, 0))],
            scratch_shapes=[pltpu.VMEM((slot_tile, D), F32), pltpu.VMEM((slot_tile, TOKEN_BLOCK), F32)]),
        out_shape=[jax.ShapeDtypeStruct((B, N_EXPERTS, cap_pad, D), BF16),
                   jax.ShapeDtypeStruct((B, N_EXPERTS, cap_pad, 1), F32)],
        compiler_params=_cparams("arbitrary", "arbitrary"),
        name="expert_gather",
    )(tlo, thi, pos, aff, h2)


def _ffn_kernel(xg_ref, g_ref, wg_ref, wu_ref, wd_ref, y_ref, wg_bf, wu_bf, wd_bf):
    @pl.when(pl.program_id(1) == 0)
    def _():
        wg_bf[...] = wg_ref[0, 0].astype(BF16)
        wu_bf[...] = wu_ref[0, 0].astype(BF16)
        wd_bf[...] = wd_ref[0, 0].astype(BF16)

    xg = xg_ref[0, 0]
    a = _dot(xg, wg_bf[...])
    u = _dot(xg, wu_bf[...])
    hmid = (a / (1.0 + jnp.exp(-a)) * u).astype(BF16)
    y_ref[0, 0] = (_dot(hmid, wd_bf[...]) * g_ref[0, 0]).astype(BF16)


def _ffn_call(xg, g, l, p):
    B, E, cp, D = xg.shape
    F = p["w_gate"].shape[-1]
    tok = lambda e, b: (b, e, 0, 0)
    wsp = lambda e, b: (l, e, 0, 0)
    return pl.pallas_call(
        _ffn_kernel,
        grid=(E, B),
        in_specs=[pl.BlockSpec((1, 1, cp, D), tok), pl.BlockSpec((1, 1, cp, 1), tok),
                  pl.BlockSpec((1, 1, D, F), wsp), pl.BlockSpec((1, 1, D, F), wsp),
                  pl.BlockSpec((1, 1, F, D), wsp)],
        out_specs=pl.BlockSpec((1, 1, cp, D), tok),
        out_shape=jax.ShapeDtypeStruct((B, E, cp, D), BF16),
        scratch_shapes=[pltpu.VMEM((D, F), BF16), pltpu.VMEM((D, F), BF16), pltpu.VMEM((F, D), BF16)],
        compiler_params=_cparams("arbitrary", "arbitrary"),
        name="expert_mlp",
    )(xg, g, p["w_gate"], p["w_up"], p["w_down"])


def _combine_kernel(off_ref, pos_ref, yw_ref, x_ref, g_ref, lg, lb, o_ref, *, window, blocks_per_step, alpha):
    b = pl.program_id(0)
    t = pl.program_id(1)
    cap_pad = yw_ref.shape[2]
    nb_total = pl.num_programs(1) * blocks_per_step
    for k in range(blocks_per_step):
        blk = t * blocks_per_step + k
        rows = slice(k * TOKEN_BLOCK, (k + 1) * TOKEN_BLOCK)
        acc = jnp.zeros((TOKEN_BLOCK, yw_ref.shape[-1]), F32)
        for e in range(N_EXPERTS):
            start = off_ref[(b * nb_total + blk) * N_EXPERTS + e]
            w0 = pl.multiple_of(jnp.minimum(start & -16, cap_pad - window), 16)
            p = pos_ref[0, k, e:e + 1, :]
            slot = (w0 + lax.broadcasted_iota(jnp.int32, (window, TOKEN_BLOCK), 0)).astype(F32)
            hit = jnp.where(p == slot, 1.0, 0.0).T.astype(BF16)
            acc = acc + _dot(hit, yw_ref[0, e, pl.ds(w0, window), :])
        o_ref[0, rows, :] = _ln(alpha * x_ref[0, rows, :] + g_ref[0] * acc) * lg[...] + lb[...]


def _combine_call(off_flat, pos, yw, x1, gf, lg, lb, alpha):
    B, E, cap_pad, D = yw.shape
    n = x1.shape[1]
    nb = n // TOKEN_BLOCK
    window = min(256, cap_pad)
    bps = min(4, nb)
    row = pl.BlockSpec((1, bps * TOKEN_BLOCK, D), lambda b, t, *_: (b, t, 0))
    vec = pl.BlockSpec((1, D), lambda b, t, *_: (0, 0))
    return pl.pallas_call(
        functools.partial(_combine_kernel, window=window, blocks_per_step=bps, alpha=alpha),
        grid_spec=pltpu.PrefetchScalarGridSpec(
            num_scalar_prefetch=1,
            grid=(B, nb // bps),
            in_specs=[pl.BlockSpec((1, bps, E, TOKEN_BLOCK), lambda b, t, *_: (b, t, 0, 0)),
                      pl.BlockSpec((1, E, cap_pad, D), lambda b, t, *_: (b, 0, 0, 0),
                                   pipeline_mode=pl.Buffered(1)),
                      row, pl.BlockSpec((1, 1, D), lambda b, t, *_: (b, 0, 0)), vec, vec],
            out_specs=row),
        out_shape=jax.ShapeDtypeStruct((B, n, D), F32),
        compiler_params=_cparams("arbitrary", "arbitrary"),
        name="expert_combine_ln",
    )(off_flat, pos, yw, x1, gf, lg, lb)


def _paired_lanes(dim):
    q = dim // 4
    return np.r_[0:q, 2 * q:3 * q], np.r_[q:2 * q, 3 * q:4 * q]


def _slot_pad(w, slot):
    z = jnp.zeros_like(w)
    return jnp.concatenate([w, z] if slot == 0 else [z, w], axis=-1)


def _prep_layer(l, p):
    w_in = p["w_in"][l]
    D = w_in.shape[0]
    hd = HEAD_DIM

    zc = lambda k: jnp.zeros((D, k), F32)

    def packed(w, a, b):
        ha, hb = w[:, a * hd:(a + 1) * hd], w[:, b * hd:(b + 1) * hd]
        first, second = _paired_lanes(hd)
        return jnp.concatenate([ha[:, first], hb[:, first], ha[:, second], hb[:, second]], axis=-1)

    def rot_slab(w):
        first, second = _paired_lanes(MLA_ROPE)
        z48 = jnp.zeros((w.shape[0], HEAD_DIM - MLA_ROPE // 2), F32)
        return jnp.concatenate([w[:, first], z48, w[:, second], z48], axis=-1)

    def gq_cols(base):
        q, k = w_in[:, base:base + 256], w_in[:, base + 256:base + 384]
        return [packed(q, 0, 2), packed(q, 1, 3), packed(k, 0, 1), w_in[:, base + 384:base + 512]]

    cols = [w_in[:, 0:768]] + gq_cols(768) + gq_cols(1280)
    cols += [w_in[:, 1792:1984], zc(64), w_in[:, 1984:2112], rot_slab(w_in[:, 2112:2144])]
    w_qkv = jnp.concatenate(cols, axis=-1).astype(BF16)
    assert w_qkv.shape[1] == _C_END

    gq = packed(p["gqa_q_norm"][l][None], 0, 0)
    gk = packed(p["gqa_k_norm"][l][None], 0, 0)
    gmq = jnp.concatenate([p["mla_q_norm"][l], jnp.zeros((64,), F32)])[None]
    gmkv = p["mla_kv_norm"][l][None]

    wq = p["mla_w_qb"][l]
    qcols = []
    for h in range(MLA_HEADS):
        nope = wq[:, h * 96: h * 96 + 64]
        rot = wq[:, h * 96 + 64: (h + 1) * 96]
        qcols += [_slot_pad(nope, h % 2), rot_slab(rot)]
    wqb = jnp.concatenate(qcols, axis=-1)
    wqb = jnp.concatenate([wqb, jnp.zeros((64, wqb.shape[1]), F32)], axis=0).astype(BF16)
    wkv = p["mla_w_kvb"][l]
    wkvb = jnp.concatenate([wkv[:, h * 128: h * 128 + 64] for h in range(MLA_HEADS)]
                           + [wkv[:, h * 128 + 64: (h + 1) * 128] for h in range(MLA_HEADS)],
                           axis=-1).astype(BF16)

    wb = p["w_branch"][l]
    perm = np.concatenate([np.arange(h * hd, (h + 1) * hd) for h in HEAD_PERM])
    w_branch = jnp.stack([wb[0], wb[1][perm], wb[2][perm], wb[3]]).astype(BF16)

    wr = jnp.concatenate([p["w_router"][l], jnp.zeros((D, LANES - N_EXPERTS), F32)], axis=-1)
    wr_hi = wr.astype(BF16)
    wr_lo = (wr - wr_hi.astype(F32)).astype(BF16)

    w = jnp.arange(GRID_W)
    col_start = jnp.clip(w - NA_KW // 2, 0, GRID_W - NA_KW)
    col_ok = (w[None, :] >= col_start[:, None]) & (w[None, :] < col_start[:, None] + NA_KW)
    dc_idx = jnp.clip(w[None, :] - w[:, None], 1 - NA_KW, NA_KW - 1) + NA_KW - 1
    t = jnp.where(col_ok[None, None], p["na_rpb"][l][:, :, dc_idx], NEG_INF)
    t = jnp.concatenate([t, jnp.full((NA_HEADS, 1, GRID_W, GRID_W), NEG_INF, F32)], axis=1)
    t = (t * LOG2E).transpose(0, 1, 3, 2).reshape(NA_HEADS * 16, GRID_W, GRID_W)
    zt = jnp.zeros_like(t)
    return dict(
        w_qkv=w_qkv, gq=gq, gk=gk, gmq=gmq, gmkv=gmkv, wqb=wqb, wkvb=wkvb,
        w_gates=w_in[:, 2144:].astype(BF16), w_branch=w_branch, w_out=p["w_out"][l].astype(BF16),
        ln1_g=p["ln1_g"][l][None], ln1_b=p["ln1_b"][l][None],
        ln2_g=p["ln2_g"][l][None], ln2_b=p["ln2_b"][l][None],
        wr_hi=wr_hi, wr_lo=wr_lo,
        tl=jnp.concatenate([t, zt], axis=-1), tr=jnp.concatenate([zt, t], axis=-1),
        sink=p["swa_sink"][l][np.array(HEAD_PERM)] * LOG2E,
    )


def _rope_tables(n, ctx_len):
    pos = jnp.arange(n, dtype=jnp.int32)
    row = (pos // GRID_W).astype(F32)
    col = (pos % GRID_W).astype(F32)

    def axial(dim):
        quarter = dim // 4
        freqs = ROPE_THETA ** (-jnp.arange(quarter, dtype=F32) / quarter)
        ang = jnp.concatenate([row[:, None] * freqs[None, :], col[:, None] * freqs[None, :]], axis=-1)
        return jnp.cos(ang), jnp.sin(ang)

    c, s = axial(HEAD_DIM)
    cos = jnp.concatenate([c, c, c, c], axis=-1)
    sin = jnp.concatenate([-s, -s, s, s], axis=-1)
    c, s = axial(MLA_ROPE)
    pad1 = jnp.ones((n, HEAD_DIM - MLA_ROPE // 2), F32)
    pad0 = jnp.zeros((n, HEAD_DIM - MLA_ROPE // 2), F32)
    cosd = jnp.concatenate([c, pad1, c, pad1], axis=-1)
    sind = jnp.concatenate([-s, pad0, s, pad0], axis=-1)
    one = jnp.ones((ctx_len, LANES), F32)
    zero = jnp.zeros((ctx_len, LANES), F32)
    return (cos, sin, cosd, sind), (one, zero, one, zero)


def _expert_ffn_ln(x1, h2, aff, gf, lw, l, p, alpha):
    B, n, D = h2.shape
    cap = CAPACITY * n // N_EXPERTS
    cap_pad = max(cap, TOKEN_BLOCK)
    slot_tile = TOKEN_BLOCK
    pos, off, tlo, thi = _select_call(aff, cap, slot_tile)
    xg, g = _gather_call(tlo, thi, pos, aff, h2, cap_pad, slot_tile)
    yw = _ffn_call(xg, g, l, p)
    return _combine_call(off[..., 0].reshape(-1), pos, yw, x1, gf, lw["ln2_g"], lw["ln2_b"], alpha)


def kernel(x, c, ctx, c_ctx, w_mod, b_mod, w_in, na_rpb, gqa_q_norm, gqa_k_norm, swa_sink, mla_q_norm, mla_kv_norm, mla_w_qb, mla_w_kvb, w_branch, w_out, ln1_g, ln1_b, ln2_g, ln2_b, w_router, w_gate, w_up, w_down):
    p = dict(w_in=w_in, na_rpb=na_rpb, gqa_q_norm=gqa_q_norm, gqa_k_norm=gqa_k_norm, swa_sink=swa_sink,
             mla_q_norm=mla_q_norm, mla_kv_norm=mla_kv_norm, mla_w_qb=mla_w_qb, mla_w_kvb=mla_w_kvb,
             w_branch=w_branch, w_out=w_out, ln1_g=ln1_g, ln1_b=ln1_b, ln2_g=ln2_g, ln2_b=ln2_b,
             w_router=w_router, w_gate=w_gate, w_up=w_up, w_down=w_down)
    B, n, D = x.shape
    depth = w_in.shape[0]
    C = ctx.shape[1]
    alpha = (2 * depth) ** 0.25
    assert B + 1 <= 8
    cc = jnp.concatenate([c, c_ctx[None], jnp.zeros((8 - B - 1, D), F32)], axis=0)
    mod_all = _mod_call(cc, w_mod, b_mod)
    tabs, tabs_ctx = _rope_tables(n, C)
    xc = ctx
    for l in range(depth):
        lw = _prep_layer(l, p)
        need_ctx = l < depth - 1
        mods = [mod_all[l, :B, k * D:(k + 1) * D][:, None, :] for k in range(6)]
        mods_c = [jnp.broadcast_to(mod_all[l, B, k * D:(k + 1) * D][None, None, :], (B, 1, D)) for k in range(6)]
        pl_ = _inproj_call(x, mods[0], mods[1], lw, tabs)
        pc = _inproj_call(xc, mods_c[0], mods_c[1], lw, tabs_ctx)
        (naq, nak, nav, bq, bk, bv, cq, ck, cv, dq, dk, dv) = pl_
        (_, nakc, navc, _, bkc, bvc, _, ckc, cvc, _, dkc, dvc) = pc
        o_a = _na_call(naq, nak, nav, nakc, navc, lw["tl"], lw["tr"])
        o_b = _flash_call(bq, bk[:, None], bv, bkc[:, None], bvc, per_pair=False)
        o_c = _win_call(lw["sink"], cq, ck, cv, ckc, cvc)
        o_d = _flash_call(dq, dk, dv, dkc, dvc, per_pair=True)
        x1, h2, aff = _merge_call(x, (o_a, o_b, o_c, o_d), mods[:5], lw, alpha)
        x = _expert_ffn_ln(x1, h2, aff, mods[5], lw, l, p, alpha)
        if need_ctx:
            oc = _ctx_attn_call(lw["sink"], pc)
            xc1, hc2, affc = _merge_call(xc, oc, mods_c[:5], lw, alpha)
            xc = _expert_ffn_ln(xc1, hc2, affc, mods_c[5], lw, l, p, alpha)
    return x
```

```python
import functools

import numpy as np
import jax
import jax.numpy as jnp
from jax import lax
from jax.experimental import pallas as pl
from jax.experimental.pallas import tpu as pltpu

F32 = jnp.float32
BF16 = jnp.bfloat16

GRID_W = 64
HEAD_DIM = 64
ROPE_THETA = 10000.0
EPS = 1e-6
NEG_INF = -1e30
NA_HEADS = 4
NA_KH = 8
NA_KW = 16
NA_WIN_ROWS = 10
SWA_WINDOW = 128
QBLOCK = 128
MLA_HEADS = 4
MLA_NOPE = 64
MLA_ROPE = 32
MLA_V = 64
MLA_Q_LORA = 192
MLA_KV_LORA = 128
N_BRANCH = 4
BRANCH_W = 256
N_EXPERTS = 16
CAPACITY = 2
LANES = 128
TOKEN_BLOCK = 128
VMEM_LIMIT = 56 * 1024 * 1024
HEAD_PERM = (0, 2, 1, 3)
VT_ROWS = 80
LOG2E = 1.4426950408889634
DENSE_KEY_CHUNK = 512
LOCAL_BLOCKS_PER_STEP = 4


def _cparams(*sem):
    return pltpu.CompilerParams(dimension_semantics=sem, vmem_limit_bytes=VMEM_LIMIT)


def _dot(a, b):
    return jnp.dot(a, b, preferred_element_type=F32)


def _dot_nt(a, b):
    return lax.dot_general(a, b, (((1,), (1,)), ((), ())), preferred_element_type=F32)


def _ln(x):
    mu = jnp.mean(x, axis=-1, keepdims=True)
    xc = x - mu
    var = jnp.mean(xc * xc, axis=-1, keepdims=True)
    return xc * lax.rsqrt(var + EPS)


def _modulate(x, shift, scale):
    return _ln(x) * (1.0 + scale) + shift


def _lane(shape, dim=None):
    return lax.broadcasted_iota(jnp.int32, shape, len(shape) - 1 if dim is None else dim)


def _rope(y, cos, sin_signed):
    return y * cos + pltpu.roll(y, HEAD_DIM, 1) * sin_signed


def _mod_kernel(c_ref, w_ref, b_ref, o_ref):
    c = c_ref[...]
    s = c / (1.0 + jnp.exp(-c))
    o_ref[0] = _dot(s.astype(BF16), w_ref[0].astype(BF16)) + b_ref[0]


def _mod_call(cc, w_mod, b_mod):
    L, D, N = w_mod.shape
    tn = N // 4
    return pl.pallas_call(
        _mod_kernel,
        grid=(L, N // tn),
        in_specs=[pl.BlockSpec((8, D), lambda l, j: (0, 0)),
                  pl.BlockSpec((1, D, tn), lambda l, j: (l, 0, j)),
                  pl.BlockSpec((1, 1, tn), lambda l, j: (l, 0, j))],
        out_specs=pl.BlockSpec((1, 8, tn), lambda l, j: (l, 0, j)),
        out_shape=jax.ShapeDtypeStruct((L, 8, N), F32),
        compiler_params=_cparams("arbitrary", "arbitrary"),
        name="mod_vectors",
    )(cc, w_mod, b_mod.reshape(L, 1, N))


_C_NA = 0
_C_BQ = 768
_C_BKV = 1024
_C_CQ = 1280
_C_CKV = 1536
_C_DQ = 1792
_C_DKV = 2048
_C_END = 2304


def _inproj_kernel(x_ref, sh_ref, sc_ref, w_ref, cos_ref, sin_ref, cosd_ref, sind_ref,
                   gq_ref, gk_ref, gmq_ref, gmkv_ref, wqb_ref, wkvb_ref,
                   naq, nak, nav, bq, bk, bv, cq, ck, cv, dq, dk, dv):
    hb = _modulate(x_ref[0], sh_ref[0], sc_ref[0]).astype(BF16)

    def seg(a, b):
        return _dot(hb, w_ref[:, a:b])

    cos = cos_ref[...]
    sin = sin_ref[...]
    cosd = cosd_ref[...]
    sind = sind_ref[...]
    qscale = HEAD_DIM ** -0.5 * LOG2E

    def put_values_blocks(ref, v, nheads):
        vt = v.T
        ones = jnp.ones((VT_ROWS - HEAD_DIM, TOKEN_BLOCK), F32)
        for g in range(vt.shape[1] // TOKEN_BLOCK):
            for h in range(nheads):
                blk = vt[h * HEAD_DIM:(h + 1) * HEAD_DIM, g * TOKEN_BLOCK:(g + 1) * TOKEN_BLOCK]
                ref[0, g, h] = jnp.concatenate([blk, ones], axis=0).astype(BF16)

    def put_values_t(ref, pair, v):
        vt = v.T
        ones = jnp.ones((VT_ROWS - HEAD_DIM, vt.shape[1]), F32)
        for s in range(2):
            ref[0, pair, 0, s] = jnp.concatenate([vt[s * HEAD_DIM:(s + 1) * HEAD_DIM], ones], axis=0).astype(BF16)

    z = seg(_C_NA, _C_NA + 768)
    naq[0] = (z[:, 0:256] * qscale).astype(BF16)
    nak[0] = z[:, 256:512].astype(BF16)
    put_values_blocks(nav, z[:, 512:768], NA_HEADS)

    slot_a = (_lane((1, LANES)) & 32) == 0

    def slabs(y):
        return [jnp.where(slot_a, y, 0.0), jnp.where(slot_a, 0.0, y)]

    def head_rms(z, gain):
        z2 = z * z
        ss_a = jnp.sum(jnp.where(slot_a, z2, 0.0), axis=-1, keepdims=True)
        ss_b = jnp.sum(jnp.where(slot_a, 0.0, z2), axis=-1, keepdims=True)
        inv = jnp.where(slot_a, lax.rsqrt(ss_a * (1.0 / HEAD_DIM) + EPS), lax.rsqrt(ss_b * (1.0 / HEAD_DIM) + EPS))
        return z * inv * gain

    z = seg(_C_BQ, _C_BKV)
    for c in range(2):
        y = _rope(head_rms(z[:, c * LANES:(c + 1) * LANES], gq_ref[...]), cos, sin) * qscale
        for j, slab in enumerate(slabs(y)):
            bq[0, 2 * c + j] = slab.astype(BF16)
    kv = seg(_C_BKV, _C_CQ)
    bk[0] = _rope(head_rms(kv[:, :LANES], gk_ref[...]), cos, sin).astype(BF16)
    put_values_t(bv, 0, kv[:, LANES:])

    z = seg(_C_CQ, _C_CKV)
    for c in range(2):
        y = _rope(z[:, c * LANES:(c + 1) * LANES], cos, sin) * qscale
        for j, slab in enumerate(slabs(y)):
            cq[0, 2 * c + j] = slab.astype(BF16)
    kv = seg(_C_CKV, _C_DQ)
    ck[0] = _rope(kv[:, :LANES], cos, sin).astype(BF16)
    put_values_blocks(cv, kv[:, LANES:], 2)

    z = seg(_C_DQ, _C_DKV)
    ss = jnp.sum(z * z, axis=-1, keepdims=True) * (1.0 / MLA_Q_LORA)
    cqn = (z * lax.rsqrt(ss + EPS) * gmq_ref[...]).astype(BF16)
    qd = _dot(cqn, wqb_ref[...])
    dscale = (MLA_NOPE + MLA_ROPE) ** -0.5 * LOG2E
    for h in range(MLA_HEADS):
        nope = qd[:, h * 256:h * 256 + LANES] * dscale
        rot = _rope(qd[:, h * 256 + LANES:(h + 1) * 256], cosd, sind) * dscale
        dq[0, h] = jnp.concatenate([nope, rot], axis=-1).astype(BF16)
    zz = seg(_C_DKV, _C_END)
    z = zz[:, :LANES]
    ss = jnp.sum(z * z, axis=-1, keepdims=True) * (1.0 / MLA_KV_LORA)
    ckvn = (z * lax.rsqrt(ss + EPS) * gmkv_ref[...]).astype(BF16)
    kv = _dot(ckvn, wkvb_ref[...])
    kr = _rope(zz[:, LANES:], cosd, sind)
    for p in range(2):
        dk[0, p] = jnp.concatenate([kv[:, p * LANES:(p + 1) * LANES], kr], axis=-1).astype(BF16)
        put_values_t(dv, p, kv[:, 256 + p * LANES:256 + (p + 1) * LANES])


def _inproj_call(x, shift, scale, lw, tabs):
    B, n, D = x.shape
    tm = min(512, n)
    cos, sin, cosd, sind = tabs
    row = lambda b, i: (b, i, 0)
    hrow = lambda b, i: (b, 0, i, 0)
    const2 = lambda b, i: (0, 0)
    tab = pl.BlockSpec((tm, LANES), lambda b, i: (i, 0))
    mod = pl.BlockSpec((1, 1, D), lambda b, i: (b, 0, 0))
    nt = n // tm
    vt_spec = lambda pairs: ((B, pairs, nt, 2, VT_ROWS, tm),
                             pl.BlockSpec((1, pairs, 1, 2, VT_ROWS, tm), lambda b, i: (b, 0, i, 0, 0, 0)))
    slab = lambda kd: ((B, 4, n, kd), pl.BlockSpec((1, 4, tm, kd), hrow))
    tokm = lambda w: ((B, n, w), pl.BlockSpec((1, tm, w), row))
    vblk = lambda heads: ((B, n // TOKEN_BLOCK, heads, VT_ROWS, TOKEN_BLOCK),
                          pl.BlockSpec((1, tm // TOKEN_BLOCK, heads, VT_ROWS, TOKEN_BLOCK),
                                       lambda b, i: (b, i, 0, 0, 0)))
    outs = [tokm(256), tokm(256), vblk(NA_HEADS)]
    outs += [slab(LANES), tokm(LANES), vt_spec(1)]
    outs += [slab(LANES), tokm(LANES), vblk(2)]
    outs += [slab(256), ((B, 2, n, 256), pl.BlockSpec((1, 2, tm, 256), hrow)), vt_spec(2)]
    return pl.pallas_call(
        _inproj_kernel,
        grid=(B, n // tm),
        in_specs=[pl.BlockSpec((1, tm, D), row), mod, mod,
                  pl.BlockSpec((D, _C_END), const2), tab, tab, tab, tab,
                  pl.BlockSpec((1, LANES), const2), pl.BlockSpec((1, LANES), const2),
                  pl.BlockSpec((1, 256), const2), pl.BlockSpec((1, LANES), const2),
                  pl.BlockSpec((256, 1024), const2), pl.BlockSpec((LANES, 512), const2)],
        out_specs=[o[1] for o in outs],
        out_shape=[jax.ShapeDtypeStruct(o[0], BF16) for o in outs],
        compiler_params=_cparams("arbitrary", "arbitrary"),
        name="in_proj",
    )(x, shift, scale, lw["w_qkv"], cos, sin, cosd, sind,
      lw["gq"], lw["gk"], lw["gmq"], lw["gmkv"], lw["wqb"], lw["wkvb"])


def _flash_kernel(q_ref, k_ref, vt_ref, kc_ref, vct_ref, o_ref, acc_ref, m_ref, s_ref, *, tq, nkk, vpc, nq):
    kd = q_ref.shape[-1]
    ctx_len = kc_ref.shape[3]

    def load_q(i):
        return q_ref[0, :, pl.ds(pl.multiple_of(i * tq, tq), tq), :].reshape(2 * tq, kd)

    def update(s, values):
        m_prev = m_ref[...]
        m_new = jnp.maximum(m_prev, jnp.max(s, axis=0, keepdims=True))
        a = jnp.exp2(m_prev - m_new)
        p = jnp.exp2(s - m_new).astype(BF16)
        for h in range(2):
            cols = slice(h * tq, (h + 1) * tq)
            pv, row = None, 0
            for vt in values:
                term = _dot(vt[h], p[row:row + vt.shape[-1], cols])
                pv = term if pv is None else pv + term
                row += vt.shape[-1]
            acc_ref[h] = a[:, cols] * acc_ref[h] + pv
        m_ref[...] = m_new

    def query_block(i, buf):
        q = load_q(i)
        m_ref[...] = jnp.full(m_ref.shape, NEG_INF, F32)
        acc_ref[...] = jnp.zeros(acc_ref.shape, F32)
        for j in range(nkk):
            cur, nxt = (buf + j) % 2, (buf + j + 1) % 2
            if j + 1 < nkk:
                s_ref[nxt] = _dot_nt(k_ref[0, 0, j + 1], q)
            else:
                s_ref[nxt, :ctx_len] = _dot_nt(kc_ref[0, 0, 0], q)
            update(s_ref[cur], [vt_ref[0, 0, vpc * j + u] for u in range(vpc)])
        s_ref[1 - buf] = _dot_nt(k_ref[0, 0, 0], load_q(jnp.minimum(i + 1, nq - 1)))
        update(s_ref[buf, :ctx_len], [vct_ref[0, 0, 0]])
        o = [acc_ref[h][:HEAD_DIM] / acc_ref[h][HEAD_DIM:HEAD_DIM + 1] for h in range(2)]
        o_ref[0, pl.ds(pl.multiple_of(i * tq, tq), tq), :] = jnp.concatenate(o, axis=0).T.astype(BF16)

    s_ref[0] = _dot_nt(k_ref[0, 0, 0], load_q(0))

    def two_blocks(ii, carry):
        query_block(2 * ii, 0)
        query_block(2 * ii + 1, 1)
        return carry

    lax.fori_loop(0, nq // 2, two_blocks, 0)


def _flash_call(q, k, vt, kc, vct, *, per_pair, key_chunk):
    B, _, n, kd = q.shape
    P = k.shape[1]
    C = kc.shape[2]
    nv, tv = vt.shape[2], vt.shape[5]
    tkk = max(tv, min(key_chunk, n // 2))
    nkk, vpc = n // tkk, tkk // tv
    tq = min(512, n)
    nq = n // tq
    assert nq % 2 == 0 and nkk % 2 == 0 and nkk * vpc == nv and C <= tkk
    pidx = (lambda b, p: (b, p, 0, 0, 0)) if per_pair else (lambda b, p: (b, 0, 0, 0, 0))
    vidx = (lambda b, p: (b, p, 0, 0, 0, 0)) if per_pair else (lambda b, p: (b, 0, 0, 0, 0, 0))
    kern = functools.partial(_flash_kernel, tq=tq, nkk=nkk, vpc=vpc, nq=nq)
    return pl.pallas_call(
        kern,
        grid=(B, 2),
        in_specs=[pl.BlockSpec((1, 2, n, kd), lambda b, p: (b, p, 0, 0)),
                  pl.BlockSpec((1, 1, nkk, tkk, kd), pidx),
                  pl.BlockSpec((1, 1, nv, 2, VT_ROWS, tv), vidx),
                  pl.BlockSpec((1, 1, 1, C, kd), pidx),
                  pl.BlockSpec((1, 1, 1, 2, VT_ROWS, C), vidx)],
        out_specs=pl.BlockSpec((1, n, LANES), lambda b, p: (b, 0, p)),
        out_shape=jax.ShapeDtypeStruct((B, n, 256), BF16),
        scratch_shapes=[pltpu.VMEM((2, VT_ROWS, tq), F32), pltpu.VMEM((1, 2 * tq), F32),
                        pltpu.VMEM((2, tkk, 2 * tq), F32)],
        compiler_params=_cparams("arbitrary", "arbitrary"),
        name="dense_attn",
    )(q, k.reshape(B, P, nkk, tkk, kd), vt, kc[:, :, None], vct)


def _local_values(vt_ref, g0, blocks, vct_ref, head):
    lat = jnp.concatenate([vt_ref[0, g0 + g, head] for g in range(blocks)], axis=1)
    ctx = jnp.concatenate([vct_ref[0, g, head] for g in range(vct_ref.shape[1])], axis=1)
    return lat, ctx


def _na_kernel(q_ref, k_ref, vt_ref, kc_ref, vct_ref, tl_ref, tr_ref, o_ref, *, rows):
    for sub in range(LOCAL_BLOCKS_PER_STEP):
        rows_q = slice(sub * 2 * GRID_W, (sub + 1) * 2 * GRID_W)
        o_ref[0, rows_q, :] = _na_block(LOCAL_BLOCKS_PER_STEP * pl.program_id(1) + sub, q_ref[0, rows_q, :],
                                        k_ref, vt_ref, kc_ref, vct_ref, tl_ref, tr_ref, rows)


def _na_block(i, q, k_ref, vt_ref, kc_ref, vct_ref, tl_ref, tr_ref, rows):
    r0 = 2 * i
    w0 = jnp.clip(r0 - NA_KH // 2, 0, rows - NA_WIN_ROWS)
    win = NA_WIN_ROWS * GRID_W
    start = pl.multiple_of(w0 * GRID_W, 2 * GRID_W)
    k_all = jnp.concatenate([k_ref[0, pl.ds(start, win), :], kc_ref[0]], axis=0)
    head_of_lane = _lane((1, 256)) >> 6
    q_stack = jnp.concatenate([jnp.where(head_of_lane == h, q, jnp.zeros_like(q)) for h in range(NA_HEADS)], axis=0)
    s = _dot_nt(k_all, q_stack)

    def table_index(a, j):
        qr = r0 + a
        kr = w0 + j
        st = jnp.clip(qr - NA_KH // 2, 0, rows - NA_KH)
        ok = (kr >= st) & (kr < st + NA_KH)
        return jnp.where(ok, kr - qr + NA_KH - 1, 2 * NA_KH - 1)

    idx = [[table_index(a, j) for j in range(NA_WIN_ROWS)] for a in range(2)]
    bias = jnp.concatenate(
        [jnp.concatenate([tl_ref[h * 16 + idx[0][j]] + tr_ref[h * 16 + idx[1][j]] for j in range(NA_WIN_ROWS)], axis=0)
         for h in range(NA_HEADS)], axis=1)
    s_lat = s[:win] + bias
    s_ctx = s[win:]
    m = jnp.maximum(jnp.max(s_lat, axis=0, keepdims=True), jnp.max(s_ctx, axis=0, keepdims=True))
    p_lat = jnp.exp2(s_lat - m).astype(BF16)
    p_ctx = jnp.exp2(s_ctx - m).astype(BF16)
    outs = []
    for h in range(NA_HEADS):
        cols = slice(h * LANES, (h + 1) * LANES)
        v_lat, v_ctx = _local_values(vt_ref, w0 >> 1, win // TOKEN_BLOCK, vct_ref, h)
        o = _dot(v_lat, p_lat[:, cols]) + _dot(v_ctx, p_ctx[:, cols])
        outs.append(o[:HEAD_DIM] / o[HEAD_DIM:HEAD_DIM + 1])
    return jnp.concatenate(outs, axis=0).T.astype(BF16)


def _na_call(q, k, vt, kc, vct, tl, tr):
    B, n, _ = q.shape
    C = kc.shape[1]
    rows = n // GRID_W
    step_rows = 2 * LOCAL_BLOCKS_PER_STEP
    assert rows >= NA_WIN_ROWS and rows % step_rows == 0
    full = lambda b, i: (b, 0, 0)
    full5 = lambda b, i: (b, 0, 0, 0, 0)
    return pl.pallas_call(
        functools.partial(_na_kernel, rows=rows),
        grid=(B, rows // step_rows),
        in_specs=[pl.BlockSpec((1, step_rows * GRID_W, 256), lambda b, i: (b, i, 0)),
                  pl.BlockSpec((1, n, 256), full), pl.BlockSpec((1,) + vt.shape[1:], full5),
                  pl.BlockSpec((1, C, 256), full), pl.BlockSpec((1,) + vct.shape[1:], full5),
                  pl.BlockSpec(tl.shape, lambda b, i: (0, 0, 0)),
                  pl.BlockSpec(tr.shape, lambda b, i: (0, 0, 0))],
        out_specs=pl.BlockSpec((1, step_rows * GRID_W, 256), lambda b, i: (b, i, 0)),
        out_shape=jax.ShapeDtypeStruct((B, n, 256), BF16),
        compiler_params=_cparams("arbitrary", "arbitrary"),
        name="nbr_attn",
    )(q, k, vt, kc, vct, tl, tr)


def _win_kernel(sink_ref, q_ref, k_ref, vt_ref, kc_ref, vct_ref, o_ref, *, n):
    for sub in range(LOCAL_BLOCKS_PER_STEP):
        rows_q = slice(sub * QBLOCK, (sub + 1) * QBLOCK)
        o_ref[0, rows_q, :] = _win_block(LOCAL_BLOCKS_PER_STEP * pl.program_id(1) + sub, q_ref[0, :, rows_q, :],
                                         sink_ref, k_ref, vt_ref, kc_ref, vct_ref, n)


def _win_block(i, q, sink_ref, k_ref, vt_ref, kc_ref, vct_ref, n):
    band = 3 * QBLOCK
    ws = pl.multiple_of(jnp.clip((i - 1) * QBLOCK, 0, n - band), QBLOCK)
    k_all = jnp.concatenate([k_ref[0, pl.ds(ws, band), :], kc_ref[0]], axis=0)
    q_stack = q.reshape(4 * QBLOCK, LANES)
    s = _dot_nt(k_all, q_stack)
    kpos = ws + lax.broadcasted_iota(jnp.int32, (band, QBLOCK), 0)
    qpos = i * QBLOCK + lax.broadcasted_iota(jnp.int32, (band, QBLOCK), 1)
    ok = jnp.abs(qpos - kpos) <= SWA_WINDOW
    s_lat = jnp.concatenate([jnp.where(ok, s[:band, j * QBLOCK:(j + 1) * QBLOCK], NEG_INF) for j in range(4)], axis=1)
    s_ctx = s[band:]
    snk = jnp.concatenate([jnp.full((1, QBLOCK), sink_ref[j], F32) for j in range(4)], axis=1)
    m = jnp.maximum(jnp.maximum(jnp.max(s_lat, axis=0, keepdims=True), jnp.max(s_ctx, axis=0, keepdims=True)), snk)
    p_lat = jnp.exp2(s_lat - m).astype(BF16)
    p_ctx = jnp.exp2(s_ctx - m).astype(BF16)
    p_snk = jnp.exp2(snk - m)
    g0 = ws >> 7
    outs = []
    for j in range(4):
        cols = slice(j * QBLOCK, (j + 1) * QBLOCK)
        v_lat, v_ctx = _local_values(vt_ref, g0, band // TOKEN_BLOCK, vct_ref, HEAD_PERM[j] // 2)
        o = _dot(v_lat, p_lat[:, cols]) + _dot(v_ctx, p_ctx[:, cols])
        outs.append(o[:HEAD_DIM] / (o[HEAD_DIM:HEAD_DIM + 1] + p_snk[:, cols]))
    return jnp.concatenate(outs, axis=0).T.astype(BF16)


def _win_call(sink, q, k, vt, kc, vct):
    B, _, n, _ = q.shape
    C = kc.shape[1]
    step = LOCAL_BLOCKS_PER_STEP * QBLOCK
    assert n >= 3 * QBLOCK and QBLOCK == TOKEN_BLOCK and n % step == 0
    full = lambda b, i, s: (b, 0, 0)
    full5 = lambda b, i, s: (b, 0, 0, 0, 0)
    return pl.pallas_call(
        functools.partial(_win_kernel, n=n),
        grid_spec=pltpu.PrefetchScalarGridSpec(
            num_scalar_prefetch=1,
            grid=(B, n // step),
            in_specs=[pl.BlockSpec((1, 4, step, LANES), lambda b, i, s: (b, 0, i, 0)),
                      pl.BlockSpec((1, n, LANES), full), pl.BlockSpec((1,) + vt.shape[1:], full5),
                      pl.BlockSpec((1, C, LANES), full), pl.BlockSpec((1,) + vct.shape[1:], full5)],
            out_specs=pl.BlockSpec((1, step, 256), lambda b, i, s: (b, i, 0))),
        out_shape=jax.ShapeDtypeStruct((B, n, 256), BF16),
        compiler_params=_cparams("arbitrary", "arbitrary"),
        name="window_attn",
    )(sink, q, k, vt, kc, vct)


def _attend(q, k, vt, sink=None):
    s = _dot_nt(q, k)
    m = jnp.max(s, axis=-1, keepdims=True)
    if sink is not None:
        m = jnp.maximum(m, sink)
    p = jnp.exp2(s - m)
    l = jnp.sum(p, axis=-1, keepdims=True)
    if sink is not None:
        l = l + jnp.exp2(sink - m)
    return _dot_nt(p.astype(BF16), vt) / l


def _ctx_attn_kernel(sink_ref, naq, nak, nav, bq, bk, bv, cq, ck, cv, dq, dk, dv, oa, ob, oc, od):
    left = _lane((1, LANES)) < HEAD_DIM
    head_of_lane = _lane((1, 256)) >> 6

    def pair_out(o):
        return jnp.concatenate([jnp.where(left, o[0], o[1]), jnp.where(left, o[2], o[3])], axis=-1).astype(BF16)

    def blocks_t(ref, heads):
        return jnp.concatenate(
            [jnp.concatenate([ref[0, g, h][:HEAD_DIM] for g in range(ref.shape[1])], axis=1)
             for h in range(heads)], axis=0)

    def both_heads(vt):
        return jnp.concatenate([vt[0, :HEAD_DIM], vt[1, :HEAD_DIM]], axis=0)

    q = naq[0]
    v_na = blocks_t(nav, NA_HEADS)
    out = jnp.zeros(q.shape, F32)
    for h in range(NA_HEADS):
        qm = jnp.where(head_of_lane == h, q, jnp.zeros_like(q))
        out = jnp.where(head_of_lane == h, _attend(qm, nak[0], v_na), out)
    oa[0] = out.astype(BF16)
    ob[0] = pair_out([_attend(bq[0, s], bk[0], both_heads(bv[0, 0, 0])) for s in range(4)])
    v_c = blocks_t(cv, 2)
    oc[0] = pair_out([_attend(cq[0, s], ck[0], v_c, sink_ref[s]) for s in range(4)])
    od[0] = pair_out([_attend(dq[0, h], dk[0, h // 2], both_heads(dv[0, h // 2, 0])) for h in range(MLA_HEADS)])


def _ctx_attn_call(sink, pc):
    B, C, _ = pc[0].shape
    names = pc
    specs = []
    for a in names:
        nd = a.ndim
        specs.append(pl.BlockSpec((1,) + a.shape[1:], (lambda b, s, nd=nd: (b,) + (0,) * (nd - 1))))
    out_spec = pl.BlockSpec((1, C, 256), lambda b, s: (b, 0, 0))
    return pl.pallas_call(
        _ctx_attn_kernel,
        grid_spec=pltpu.PrefetchScalarGridSpec(
            num_scalar_prefetch=1, grid=(B,), in_specs=specs, out_specs=[out_spec] * 4),
        out_shape=[jax.ShapeDtypeStruct((B, C, 256), BF16)] * 4,
        compiler_params=_cparams("arbitrary"),
        name="ctx_attn",
    )(sink, *pc)


def _merge_kernel(x_ref, oa, ob, oc, od, sha, sca, ga, shf, scf, wg_ref, wb_ref, wo_ref,
                  l1g, l1b, wrh_ref, wrl_ref, x1_ref, h2_ref, aff_ref, *, alpha):
    x = x_ref[0]
    D = x.shape[-1]
    hb = _modulate(x, sha[0], sca[0]).astype(BF16)
    merged = None
    for i, o in enumerate((oa, ob, oc, od)):
        g = 1.0 / (1.0 + jnp.exp(-_dot(hb, wg_ref[:, i * D:(i + 1) * D])))
        term = g * _dot(o[0], wb_ref[i])
        merged = term if merged is None else merged + term
    y = _dot(merged.astype(BF16), wo_ref[...])
    x1 = _ln(alpha * x + ga[0] * y) * l1g[...] + l1b[...]
    x1_ref[0] = x1
    h2 = _modulate(x1, shf[0], scf[0])
    h2_hi = h2.astype(BF16)
    h2_ref[0] = h2_hi
    h2_lo = (h2 - h2_hi.astype(F32)).astype(BF16)
    logits = _dot(h2_hi, wrh_ref[...]) + _dot(h2_hi, wrl_ref[...]) + _dot(h2_lo, wrh_ref[...])
    logits = jnp.where(_lane((1, LANES)) < N_EXPERTS, logits, NEG_INF)
    e = jnp.exp(logits - jnp.max(logits, axis=-1, keepdims=True))
    aff_t = (e / jnp.sum(e, axis=-1, keepdims=True)).T
    for k in range(aff_ref.shape[1]):
        aff_ref[0, k] = aff_t[:N_EXPERTS, k * TOKEN_BLOCK:(k + 1) * TOKEN_BLOCK]


def _merge_call(x, outs, mods, lw, alpha):
    B, n, D = x.shape
    tm = min(256, n)
    nb = tm // TOKEN_BLOCK
    row = lambda b, i: (b, i, 0)
    mod = pl.BlockSpec((1, 1, D), lambda b, i: (b, 0, 0))
    c2 = lambda b, i: (0, 0)
    obr = pl.BlockSpec((1, tm, 256), row)
    return pl.pallas_call(
        functools.partial(_merge_kernel, alpha=alpha),
        grid=(B, n // tm),
        in_specs=[pl.BlockSpec((1, tm, D), row), obr, obr, obr, obr, mod, mod, mod, mod, mod,
                  pl.BlockSpec((D, N_BRANCH * D), c2),
                  pl.BlockSpec((N_BRANCH, BRANCH_W, D), lambda b, i: (0, 0, 0)),
                  pl.BlockSpec((D, D), c2), pl.BlockSpec((1, D), c2), pl.BlockSpec((1, D), c2),
                  pl.BlockSpec((D, LANES), c2), pl.BlockSpec((D, LANES), c2)],
        out_specs=[pl.BlockSpec((1, tm, D), row), pl.BlockSpec((1, tm, D), row),
                   pl.BlockSpec((1, nb, N_EXPERTS, TOKEN_BLOCK), lambda b, i: (b, i, 0, 0))],
        out_shape=[jax.ShapeDtypeStruct((B, n, D), F32), jax.ShapeDtypeStruct((B, n, D), BF16),
                   jax.ShapeDtypeStruct((B, n // TOKEN_BLOCK, N_EXPERTS, TOKEN_BLOCK), F32)],
        compiler_params=_cparams("arbitrary", "arbitrary"),
        name="merge_router",
    )(x, *outs, *mods, lw["w_gates"], lw["w_branch"], lw["w_out"], lw["ln1_g"], lw["ln1_b"],
      lw["wr_hi"], lw["wr_lo"])


def _select_kernel(aff_ref, pos_ref, off_ref, tlo_ref, thi_ref, *, cap, slot_tile):
    a = aff_ref[0]
    nb = a.shape[0]
    rows = nb * N_EXPERTS
    bits = lax.bitcast_convert_type(a, jnp.int32)
    capf = jnp.float32(cap)

    def count(mask):
        c = jnp.sum(jnp.where(mask, 1.0, 0.0), axis=0)
        return jnp.broadcast_to(jnp.sum(c, axis=-1, keepdims=True), c.shape)

    def search(it, lo):
        cand = lo | lax.shift_left(jnp.int32(1), 30 - it)
        return jnp.where(count(bits >= cand[None]) >= capf, cand, lo)

    thr = lax.fori_loop(0, 31, search, jnp.zeros((N_EXPERTS, TOKEN_BLOCK), jnp.int32))

    r = lax.broadcasted_iota(jnp.int32, (rows, rows), 0)
    c = lax.broadcasted_iota(jnp.int32, (rows, rows), 1)
    earlier = jnp.where(((r & (N_EXPERTS - 1)) == (c & (N_EXPERTS - 1))) & ((c >> 4) < (r >> 4)), 1.0, 0.0).astype(BF16)
    ti = lax.broadcasted_iota(jnp.int32, (TOKEN_BLOCK, TOKEN_BLOCK), 0)
    tj = lax.broadcasted_iota(jnp.int32, (TOKEN_BLOCK, TOKEN_BLOCK), 1)
    tri = jnp.where(ti <= tj, 1.0, 0.0).astype(BF16)
    ones = jnp.ones((TOKEN_BLOCK, TOKEN_BLOCK), BF16)

    def prefix(mask):
        m2 = jnp.where(mask, 1.0, 0.0).reshape(rows, TOKEN_BLOCK)
        mb = m2.astype(BF16)
        within = _dot(mb, tri)
        tot = _dot(mb, ones)
        off = _dot(earlier, tot.astype(BF16))
        shp = (nb, N_EXPERTS, TOKEN_BLOCK)
        return (off + within - m2).reshape(shp), off.reshape(shp), tot.reshape(shp)

    gt = bits > thr[None]
    eq = bits == thr[None]
    need = capf - count(gt)
    eq_rank, _, _ = prefix(eq)
    sel = gt | (eq & (eq_rank < need[None]))
    excl, off, tot = prefix(sel)
    pos_ref[0] = jnp.where(sel, excl, -1.0)
    off_ref[0] = off.astype(jnp.int32)
    tile_start = (_lane((1, 1, TOKEN_BLOCK)) * slot_tile).astype(F32)
    tlo_ref[0] = jnp.sum(jnp.where(off + tot <= tile_start, 1, 0), axis=0).astype(jnp.int32)
    thi_ref[0] = jnp.sum(jnp.where(off < tile_start + slot_tile, 1, 0), axis=0).astype(jnp.int32)


def _select_call(aff, cap, slot_tile):
    B, nb, E, _ = aff.shape
    blk = pl.BlockSpec((1, nb, E, TOKEN_BLOCK), lambda b: (b, 0, 0, 0))
    rng = pl.BlockSpec((1, E, TOKEN_BLOCK), lambda b: (b, 0, 0))
    return pl.pallas_call(
        functools.partial(_select_kernel, cap=cap, slot_tile=slot_tile),
        grid=(B,),
        in_specs=[blk],
        out_specs=[blk, blk, rng, rng],
        out_shape=[jax.ShapeDtypeStruct(aff.shape, F32), jax.ShapeDtypeStruct(aff.shape, jnp.int32),
                   jax.ShapeDtypeStruct((B, E, TOKEN_BLOCK), jnp.int32),
                   jax.ShapeDtypeStruct((B, E, TOKEN_BLOCK), jnp.int32)],
        compiler_params=_cparams("arbitrary"),
        name="expert_select",
    )(aff)


def _gather_kernel(tlo_ref, thi_ref, pos_ref, aff_ref, h_ref, xg_ref, g_ref, acc_ref, gacc_ref, *,
                   slot_tile, group, in_flight):
    b = pl.program_id(0)
    e = pl.program_id(1)
    n_tiles = xg_ref.shape[2] // slot_tile
    for t in range(n_tiles):
        acc_ref[...] = jnp.zeros(acc_ref.shape, F32)
        gacc_ref[...] = jnp.zeros(gacc_ref.shape, F32)
        width = group * TOKEN_BLOCK
        slot = (t * slot_tile + lax.broadcasted_iota(jnp.int32, (slot_tile, width), 0)).astype(F32)
        shift = group.bit_length() - 1
        lo = tlo_ref[b, e, t] >> shift
        hi = (thi_ref[b, e, t] + group - 1) >> shift
        last = pos_ref.shape[1] // group - 1

        def one_product(pg, valid):
            p = jnp.concatenate([pos_ref[0, group * pg + u, pl.ds(e, 1), :] for u in range(group)], axis=-1)
            a = jnp.concatenate([aff_ref[0, group * pg + u, pl.ds(e, 1), :] for u in range(group)], axis=-1)
            hit = p == slot
            tok = pl.multiple_of(pg * width, width)
            rows = _dot(jnp.where(hit, valid, 0.0).astype(BF16), h_ref[0, pl.ds(tok, width), :])
            ga = jnp.where(hit, a * valid, 0.0)
            return rows, sum(ga[:, u * TOKEN_BLOCK:(u + 1) * TOKEN_BLOCK] for u in range(group))

        def body(it, carry):
            rows, gates = None, None
            for u in range(in_flight):
                pg = lo + in_flight * it + u
                r, g = one_product(jnp.minimum(pg, last), jnp.where(pg < hi, 1.0, 0.0).astype(F32))
                rows = r if rows is None else rows + r
                gates = g if gates is None else gates + g
            acc_ref[...] += rows
            gacc_ref[...] += gates
            return carry

        lax.fori_loop(0, lax.div(hi - lo + in_flight - 1, in_flight), body, 0)
        xg_ref[0, 0, t * slot_tile:(t + 1) * slot_tile, :] = acc_ref[...].astype(BF16)
        g_ref[0, 0, t * slot_tile:(t + 1) * slot_tile, :] = jnp.sum(gacc_ref[...], axis=-1, keepdims=True)


def _gather_call(tlo, thi, pos, aff, h2, cap_pad, slot_tile):
    B, n, D = h2.shape
    nb = n // TOKEN_BLOCK
    blk = pl.BlockSpec((1, nb, N_EXPERTS, TOKEN_BLOCK), lambda b, e, *_: (b, 0, 0, 0))
    return pl.pallas_call(
        functools.partial(_gather_kernel, slot_tile=slot_tile, group=2, in_flight=min(5, nb // 2)),
        grid_spec=pltpu.PrefetchScalarGridSpec(
            num_scalar_prefetch=2,
            grid=(B, N_EXPERTS),
            in_specs=[blk, blk, pl.BlockSpec((1, n, D), lambda b, e, *_: (b, 0, 0))],
            out_specs=[pl.BlockSpec((1, 1, cap_pad, D), lambda b, e, *_: (b, e, 0, 0)),
                       pl.BlockSpec((1, 1, cap_pad, 1), lambda b, e, *_: (b, e, 0, 0))],
            scratch_shapes=[pltpu.VMEM((slot_tile, D), F32), pltpu.VMEM((slot_tile, TOKEN_BLOCK), F32)]),
        out_shape=[jax.ShapeDtypeStruct((B, N_EXPERTS, cap_pad, D), BF16),
                   jax.ShapeDtypeStruct((B, N_EXPERTS, cap_pad, 1), F32)],
        compiler_params=_cparams("arbitrary", "arbitrary"),
        name="expert_gather",
    )(tlo, thi, pos, aff, h2)


def _ffn_kernel(xg_ref, g_ref, wg_ref, wu_ref, wd_ref, y_ref, wg_bf, wu_bf, wd_bf):
    @pl.when(pl.program_id(1) == 0)
    def _():
        wg_bf[...] = wg_ref[0, 0].astype(BF16)
        wu_bf[...] = wu_ref[0, 0].astype(BF16)
        wd_bf[...] = wd_ref[0, 0].astype(BF16)

    bb, _, cp, d = xg_ref.shape
    xg = xg_ref[:, 0].reshape(bb * cp, d)
    a = _dot(xg, wg_bf[...])
    u = _dot(xg, wu_bf[...])
    hmid = (a / (1.0 + jnp.exp(-a)) * u).astype(BF16)
    y = _dot(hmid, wd_bf[...]) * g_ref[:, 0].reshape(bb * cp, 1)
    y_ref[:, 0] = y.reshape(bb, cp, d).astype(BF16)


def _ffn_call(xg, g, l, p):
    B, E, cp, D = xg.shape
    F = p["w_gate"].shape[-1]
    bb = B if B * cp <= 1024 else 1
    tok = lambda e, b: (b, e, 0, 0)
    wsp = lambda e, b: (l, e, 0, 0)
    return pl.pallas_call(
        _ffn_kernel,
        grid=(E, B // bb),
        in_specs=[pl.BlockSpec((bb, 1, cp, D), tok), pl.BlockSpec((bb, 1, cp, 1), tok),
                  pl.BlockSpec((1, 1, D, F), wsp), pl.BlockSpec((1, 1, D, F), wsp),
                  pl.BlockSpec((1, 1, F, D), wsp)],
        out_specs=pl.BlockSpec((bb, 1, cp, D), tok),
        out_shape=jax.ShapeDtypeStruct((B, E, cp, D), BF16),
        scratch_shapes=[pltpu.VMEM((D, F), BF16), pltpu.VMEM((D, F), BF16), pltpu.VMEM((F, D), BF16)],
        compiler_params=_cparams("arbitrary", "arbitrary"),
        name="expert_mlp",
    )(xg, g, p["w_gate"], p["w_up"], p["w_down"])


def _combine_kernel(off_ref, pos_ref, yw_ref, x_ref, g_ref, lg, lb, o_ref, *, window, blocks_per_step, alpha):
    b = pl.program_id(0)
    t = pl.program_id(1)
    cap_pad = yw_ref.shape[2]
    nb_total = pl.num_programs(1) * blocks_per_step
    for k in range(blocks_per_step):
        blk = t * blocks_per_step + k
        rows = slice(k * TOKEN_BLOCK, (k + 1) * TOKEN_BLOCK)
        acc = jnp.zeros((TOKEN_BLOCK, yw_ref.shape[-1]), F32)
        for e in range(N_EXPERTS):
            start = off_ref[(b * nb_total + blk) * N_EXPERTS + e]
            w0 = pl.multiple_of(jnp.minimum(start & -16, cap_pad - window), 16)
            p = pos_ref[0, k, e:e + 1, :]
            slot = (w0 + lax.broadcasted_iota(jnp.int32, (window, TOKEN_BLOCK), 0)).astype(F32)
            hit = jnp.where(p == slot, 1.0, 0.0).T.astype(BF16)
            acc = acc + _dot(hit, yw_ref[0, e, pl.ds(w0, window), :])
        o_ref[0, rows, :] = _ln(alpha * x_ref[0, rows, :] + g_ref[0] * acc) * lg[...] + lb[...]


def _combine_call(off_flat, pos, yw, x1, gf, lg, lb, alpha):
    B, E, cap_pad, D = yw.shape
    n = x1.shape[1]
    nb = n // TOKEN_BLOCK
    window = min(256, cap_pad)
    bps = min(4, nb)
    row = pl.BlockSpec((1, bps * TOKEN_BLOCK, D), lambda b, t, *_: (b, t, 0))
    vec = pl.BlockSpec((1, D), lambda b, t, *_: (0, 0))
    return pl.pallas_call(
        functools.partial(_combine_kernel, window=window, blocks_per_step=bps, alpha=alpha),
        grid_spec=pltpu.PrefetchScalarGridSpec(
            num_scalar_prefetch=1,
            grid=(B, nb // bps),
            in_specs=[pl.BlockSpec((1, bps, E, TOKEN_BLOCK), lambda b, t, *_: (b, t, 0, 0)),
                      pl.BlockSpec((1, E, cap_pad, D), lambda b, t, *_: (b, 0, 0, 0),
                                   pipeline_mode=pl.Buffered(1)),
                      row, pl.BlockSpec((1, 1, D), lambda b, t, *_: (b, 0, 0)), vec, vec],
            out_specs=row),
        out_shape=jax.ShapeDtypeStruct((B, n, D), F32),
        compiler_params=_cparams("arbitrary", "arbitrary"),
        name="expert_combine_ln",
    )(off_flat, pos, yw, x1, gf, lg, lb)


def _paired_lanes(dim):
    q = dim // 4
    return np.r_[0:q, 2 * q:3 * q], np.r_[q:2 * q, 3 * q:4 * q]


def _slot_pad(w, slot):
    z = jnp.zeros_like(w)
    return jnp.concatenate([w, z] if slot == 0 else [z, w], axis=-1)


def _prep_layer(l, p):
    w_in = p["w_in"][l]
    D = w_in.shape[0]
    hd = HEAD_DIM

    zc = lambda k: jnp.zeros((D, k), F32)

    def packed(w, a, b):
        ha, hb = w[:, a * hd:(a + 1) * hd], w[:, b * hd:(b + 1) * hd]
        first, second = _paired_lanes(hd)
        return jnp.concatenate([ha[:, first], hb[:, first], ha[:, second], hb[:, second]], axis=-1)

    def rot_slab(w):
        first, second = _paired_lanes(MLA_ROPE)
        z48 = jnp.zeros((w.shape[0], HEAD_DIM - MLA_ROPE // 2), F32)
        return jnp.concatenate([w[:, first], z48, w[:, second], z48], axis=-1)

    def gq_cols(base):
        q, k = w_in[:, base:base + 256], w_in[:, base + 256:base + 384]
        return [packed(q, 0, 2), packed(q, 1, 3), packed(k, 0, 1), w_in[:, base + 384:base + 512]]

    cols = [w_in[:, 0:768]] + gq_cols(768) + gq_cols(1280)
    cols += [w_in[:, 1792:1984], zc(64), w_in[:, 1984:2112], rot_slab(w_in[:, 2112:2144])]
    w_qkv = jnp.concatenate(cols, axis=-1).astype(BF16)
    assert w_qkv.shape[1] == _C_END

    gq = packed(p["gqa_q_norm"][l][None], 0, 0)
    gk = packed(p["gqa_k_norm"][l][None], 0, 0)
    gmq = jnp.concatenate([p["mla_q_norm"][l], jnp.zeros((64,), F32)])[None]
    gmkv = p["mla_kv_norm"][l][None]

    wq = p["mla_w_qb"][l]
    qcols = []
    for h in range(MLA_HEADS):
        nope = wq[:, h * 96: h * 96 + 64]
        rot = wq[:, h * 96 + 64: (h + 1) * 96]
        qcols += [_slot_pad(nope, h % 2), rot_slab(rot)]
    wqb = jnp.concatenate(qcols, axis=-1)
    wqb = jnp.concatenate([wqb, jnp.zeros((64, wqb.shape[1]), F32)], axis=0).astype(BF16)
    wkv = p["mla_w_kvb"][l]
    wkvb = jnp.concatenate([wkv[:, h * 128: h * 128 + 64] for h in range(MLA_HEADS)]
                           + [wkv[:, h * 128 + 64: (h + 1) * 128] for h in range(MLA_HEADS)],
                           axis=-1).astype(BF16)

    wb = p["w_branch"][l]
    perm = np.concatenate([np.arange(h * hd, (h + 1) * hd) for h in HEAD_PERM])
    w_branch = jnp.stack([wb[0], wb[1][perm], wb[2][perm], wb[3]]).astype(BF16)

    wr = jnp.concatenate([p["w_router"][l], jnp.zeros((D, LANES - N_EXPERTS), F32)], axis=-1)
    wr_hi = wr.astype(BF16)
    wr_lo = (wr - wr_hi.astype(F32)).astype(BF16)

    w = jnp.arange(GRID_W)
    col_start = jnp.clip(w - NA_KW // 2, 0, GRID_W - NA_KW)
    col_ok = (w[None, :] >= col_start[:, None]) & (w[None, :] < col_start[:, None] + NA_KW)
    dc_idx = jnp.clip(w[None, :] - w[:, None], 1 - NA_KW, NA_KW - 1) + NA_KW - 1
    t = jnp.where(col_ok[None, None], p["na_rpb"][l][:, :, dc_idx], NEG_INF)
    t = jnp.concatenate([t, jnp.full((NA_HEADS, 1, GRID_W, GRID_W), NEG_INF, F32)], axis=1)
    t = (t * LOG2E).transpose(0, 1, 3, 2).reshape(NA_HEADS * 16, GRID_W, GRID_W)
    zt = jnp.zeros_like(t)
    return dict(
        w_qkv=w_qkv, gq=gq, gk=gk, gmq=gmq, gmkv=gmkv, wqb=wqb, wkvb=wkvb,
        w_gates=w_in[:, 2144:].astype(BF16), w_branch=w_branch, w_out=p["w_out"][l].astype(BF16),
        ln1_g=p["ln1_g"][l][None], ln1_b=p["ln1_b"][l][None],
        ln2_g=p["ln2_g"][l][None], ln2_b=p["ln2_b"][l][None],
        wr_hi=wr_hi, wr_lo=wr_lo,
        tl=jnp.concatenate([t, zt], axis=-1), tr=jnp.concatenate([zt, t], axis=-1),
        sink=p["swa_sink"][l][np.array(HEAD_PERM)] * LOG2E,
    )


def _rope_tables(n, ctx_len):
    pos = jnp.arange(n, dtype=jnp.int32)
    row = (pos // GRID_W).astype(F32)
    col = (pos % GRID_W).astype(F32)

    def axial(dim):
        quarter = dim // 4
        freqs = ROPE_THETA ** (-jnp.arange(quarter, dtype=F32) / quarter)
        ang = jnp.concatenate([row[:, None] * freqs[None, :], col[:, None] * freqs[None, :]], axis=-1)
        return jnp.cos(ang), jnp.sin(ang)

    c, s = axial(HEAD_DIM)
    cos = jnp.concatenate([c, c, c, c], axis=-1)
    sin = jnp.concatenate([-s, -s, s, s], axis=-1)
    c, s = axial(MLA_ROPE)
    pad1 = jnp.ones((n, HEAD_DIM - MLA_ROPE // 2), F32)
    pad0 = jnp.zeros((n, HEAD_DIM - MLA_ROPE // 2), F32)
    cosd = jnp.concatenate([c, pad1, c, pad1], axis=-1)
    sind = jnp.concatenate([-s, pad0, s, pad0], axis=-1)
    one = jnp.ones((ctx_len, LANES), F32)
    zero = jnp.zeros((ctx_len, LANES), F32)
    return (cos, sin, cosd, sind), (one, zero, one, zero)


def _expert_ffn_ln(x1, h2, aff, gf, lw, l, p, alpha):
    B, n, D = h2.shape
    cap = CAPACITY * n // N_EXPERTS
    cap_pad = max(cap, TOKEN_BLOCK)
    slot_tile = TOKEN_BLOCK
    pos, off, tlo, thi = _select_call(aff, cap, slot_tile)
    xg, g = _gather_call(tlo, thi, pos, aff, h2, cap_pad, slot_tile)
    yw = _ffn_call(xg, g, l, p)
    return _combine_call(off[..., 0].reshape(-1), pos, yw, x1, gf, lw["ln2_g"], lw["ln2_b"], alpha)


def kernel(x, c, ctx, c_ctx, w_mod, b_mod, w_in, na_rpb, gqa_q_norm, gqa_k_norm, swa_sink, mla_q_norm, mla_kv_norm, mla_w_qb, mla_w_kvb, w_branch, w_out, ln1_g, ln1_b, ln2_g, ln2_b, w_router, w_gate, w_up, w_down):
    p = dict(w_in=w_in, na_rpb=na_rpb, gqa_q_norm=gqa_q_norm, gqa_k_norm=gqa_k_norm, swa_sink=swa_sink,
             mla_q_norm=mla_q_norm, mla_kv_norm=mla_kv_norm, mla_w_qb=mla_w_qb, mla_w_kvb=mla_w_kvb,
             w_branch=w_branch, w_out=w_out, ln1_g=ln1_g, ln1_b=ln1_b, ln2_g=ln2_g, ln2_b=ln2_b,
             w_router=w_router, w_gate=w_gate, w_up=w_up, w_down=w_down)
    B, n, D = x.shape
    depth = w_in.shape[0]
    C = ctx.shape[1]
    alpha = (2 * depth) ** 0.25
    assert B + 1 <= 8
    cc = jnp.concatenate([c, c_ctx[None], jnp.zeros((8 - B - 1, D), F32)], axis=0)
    mod_all = _mod_call(cc, w_mod, b_mod)
    tabs, tabs_ctx = _rope_tables(n, C)
    xc = ctx
    for l in range(depth):
        lw = _prep_layer(l, p)
        need_ctx = l < depth - 1
        mods = [mod_all[l, :B, k * D:(k + 1) * D][:, None, :] for k in range(6)]
        mods_c = [jnp.broadcast_to(mod_all[l, B, k * D:(k + 1) * D][None, None, :], (B, 1, D)) for k in range(6)]
        pl_ = _inproj_call(x, mods[0], mods[1], lw, tabs)
        pc = _inproj_call(xc, mods_c[0], mods_c[1], lw, tabs_ctx)
        (naq, nak, nav, bq, bk, bv, cq, ck, cv, dq, dk, dv) = pl_
        (_, nakc, navc, _, bkc, bvc, _, ckc, cvc, _, dkc, dvc) = pc
        o_a = _na_call(naq, nak, nav, nakc, navc, lw["tl"], lw["tr"])
        o_b = _flash_call(bq, bk[:, None], bv, bkc[:, None], bvc, per_pair=False, key_chunk=DENSE_KEY_CHUNK)
        o_c = _win_call(lw["sink"], cq, ck, cv, ckc, cvc)
        o_d = _flash_call(dq, dk, dv, dkc, dvc, per_pair=True, key_chunk=DENSE_KEY_CHUNK)
        x1, h2, aff = _merge_call(x, (o_a, o_b, o_c, o_d), mods[:5], lw, alpha)
        x = _expert_ffn_ln(x1, h2, aff, mods[5], lw, l, p, alpha)
        if need_ctx:
            oc = _ctx_attn_call(lw["sink"], pc)
            xc1, hc2, affc = _merge_call(xc, oc, mods_c[:5], lw, alpha)
            xc = _expert_ffn_ln(xc1, hc2, affc, mods_c[5], lw, l, p, alpha)
    return x
```

```python
import functools

import numpy as np
import jax
import jax.numpy as jnp
from jax import lax
from jax.experimental import pallas as pl
from jax.experimental.pallas import tpu as pltpu

F32 = jnp.float32
BF16 = jnp.bfloat16

GRID_W = 64
HEAD_DIM = 64
ROPE_THETA = 10000.0
EPS = 1e-6
NEG_INF = -1e30
NA_HEADS = 4
NA_KH = 8
NA_KW = 16
NA_WIN_ROWS = 10
SWA_WINDOW = 128
QBLOCK = 128
MLA_HEADS = 4
MLA_NOPE = 64
MLA_ROPE = 32
MLA_V = 64
MLA_Q_LORA = 192
MLA_KV_LORA = 128
N_BRANCH = 4
BRANCH_W = 256
N_EXPERTS = 16
CAPACITY = 2
LANES = 128
TOKEN_BLOCK = 128
VMEM_LIMIT = 56 * 1024 * 1024
HEAD_PERM = (0, 2, 1, 3)
VT_ROWS = 80
LOG2E = 1.4426950408889634
DENSE_KEY_CHUNK = 1024
LOCAL_BLOCKS_PER_STEP = 4


def _cparams(*sem):
    return pltpu.CompilerParams(dimension_semantics=sem, vmem_limit_bytes=VMEM_LIMIT)


def _dot(a, b):
    return jnp.dot(a, b, preferred_element_type=F32)


def _dot_nt(a, b):
    return lax.dot_general(a, b, (((1,), (1,)), ((), ())), preferred_element_type=F32)


def _ln(x):
    mu = jnp.mean(x, axis=-1, keepdims=True)
    xc = x - mu
    var = jnp.mean(xc * xc, axis=-1, keepdims=True)
    return xc * lax.rsqrt(var + EPS)


def _modulate(x, shift, scale):
    return _ln(x) * (1.0 + scale) + shift


def _lane(shape, dim=None):
    return lax.broadcasted_iota(jnp.int32, shape, len(shape) - 1 if dim is None else dim)


def _rope(y, cos, sin_signed):
    return y * cos + pltpu.roll(y, HEAD_DIM, 1) * sin_signed


def _mod_kernel(c_ref, w_ref, b_ref, o_ref):
    c = c_ref[...]
    s = c / (1.0 + jnp.exp(-c))
    o_ref[0] = _dot(s.astype(BF16), w_ref[0].astype(BF16)) + b_ref[0]


def _mod_call(cc, w_mod, b_mod):
    L, D, N = w_mod.shape
    tn = N // 4
    return pl.pallas_call(
        _mod_kernel,
        grid=(L, N // tn),
        in_specs=[pl.BlockSpec((8, D), lambda l, j: (0, 0)),
                  pl.BlockSpec((1, D, tn), lambda l, j: (l, 0, j)),
                  pl.BlockSpec((1, 1, tn), lambda l, j: (l, 0, j))],
        out_specs=pl.BlockSpec((1, 8, tn), lambda l, j: (l, 0, j)),
        out_shape=jax.ShapeDtypeStruct((L, 8, N), F32),
        compiler_params=_cparams("arbitrary", "arbitrary"),
        name="mod_vectors",
    )(cc, w_mod, b_mod.reshape(L, 1, N))


_C_NA = 0
_C_BQ = 768
_C_BKV = 1024
_C_CQ = 1280
_C_CKV = 1536
_C_DQ = 1792
_C_DKV = 2048
_C_END = 2304


def _inproj_kernel(x_ref, sh_ref, sc_ref, w_ref, cos_ref, sin_ref, cosd_ref, sind_ref,
                   gq_ref, gk_ref, gmq_ref, gmkv_ref, wqb_ref, wkvb_ref,
                   naq, nak, nav, bq, bk, bv, cq, ck, cv, dq, dk, dv):
    hb = _modulate(x_ref[0], sh_ref[0], sc_ref[0]).astype(BF16)

    def seg(a, b):
        return _dot(hb, w_ref[:, a:b])

    cos = cos_ref[...]
    sin = sin_ref[...]
    cosd = cosd_ref[...]
    sind = sind_ref[...]
    qscale = HEAD_DIM ** -0.5 * LOG2E

    def put_values_blocks(ref, v, nheads):
        vt = v.T
        ones = jnp.ones((VT_ROWS - HEAD_DIM, TOKEN_BLOCK), F32)
        for g in range(vt.shape[1] // TOKEN_BLOCK):
            for h in range(nheads):
                blk = vt[h * HEAD_DIM:(h + 1) * HEAD_DIM, g * TOKEN_BLOCK:(g + 1) * TOKEN_BLOCK]
                ref[0, g, h] = jnp.concatenate([blk, ones], axis=0).astype(BF16)

    def put_values_t(ref, pair, v):
        vt = v.T
        ones = jnp.ones((VT_ROWS - HEAD_DIM, vt.shape[1]), F32)
        for s in range(2):
            ref[0, pair, 0, s] = jnp.concatenate([vt[s * HEAD_DIM:(s + 1) * HEAD_DIM], ones], axis=0).astype(BF16)

    z = seg(_C_NA, _C_NA + 768)
    naq[0] = (z[:, 0:256] * qscale).astype(BF16)
    nak[0] = z[:, 256:512].astype(BF16)
    put_values_blocks(nav, z[:, 512:768], NA_HEADS)

    slot_a = (_lane((1, LANES)) & 32) == 0

    def slabs(y):
        return [jnp.where(slot_a, y, 0.0), jnp.where(slot_a, 0.0, y)]

    def head_rms(z, gain):
        z2 = z * z
        ss_a = jnp.sum(jnp.where(slot_a, z2, 0.0), axis=-1, keepdims=True)
        ss_b = jnp.sum(jnp.where(slot_a, 0.0, z2), axis=-1, keepdims=True)
        inv = jnp.where(slot_a, lax.rsqrt(ss_a * (1.0 / HEAD_DIM) + EPS), lax.rsqrt(ss_b * (1.0 / HEAD_DIM) + EPS))
        return z * inv * gain

    z = seg(_C_BQ, _C_BKV)
    for c in range(2):
        y = _rope(head_rms(z[:, c * LANES:(c + 1) * LANES], gq_ref[...]), cos, sin) * qscale
        for j, slab in enumerate(slabs(y)):
            bq[0, 2 * c + j] = slab.astype(BF16)
    kv = seg(_C_BKV, _C_CQ)
    bk[0] = _rope(head_rms(kv[:, :LANES], gk_ref[...]), cos, sin).astype(BF16)
    put_values_t(bv, 0, kv[:, LANES:])

    z = seg(_C_CQ, _C_CKV)
    for c in range(2):
        y = _rope(z[:, c * LANES:(c + 1) * LANES], cos, sin) * qscale
        for j, slab in enumerate(slabs(y)):
            cq[0, 2 * c + j] = slab.astype(BF16)
    kv = seg(_C_CKV, _C_DQ)
    ck[0] = _rope(kv[:, :LANES], cos, sin).astype(BF16)
    put_values_blocks(cv, kv[:, LANES:], 2)

    z = seg(_C_DQ, _C_DKV)
    ss = jnp.sum(z * z, axis=-1, keepdims=True) * (1.0 / MLA_Q_LORA)
    cqn = (z * lax.rsqrt(ss + EPS) * gmq_ref[...]).astype(BF16)
    qd = _dot(cqn, wqb_ref[...])
    dscale = (MLA_NOPE + MLA_ROPE) ** -0.5 * LOG2E
    for h in range(MLA_HEADS):
        nope = qd[:, h * 256:h * 256 + LANES] * dscale
        rot = _rope(qd[:, h * 256 + LANES:(h + 1) * 256], cosd, sind) * dscale
        dq[0, h] = jnp.concatenate([nope, rot], axis=-1).astype(BF16)
    zz = seg(_C_DKV, _C_END)
    z = zz[:, :LANES]
    ss = jnp.sum(z * z, axis=-1, keepdims=True) * (1.0 / MLA_KV_LORA)
    ckvn = (z * lax.rsqrt(ss + EPS) * gmkv_ref[...]).astype(BF16)
    kv = _dot(ckvn, wkvb_ref[...])
    kr = _rope(zz[:, LANES:], cosd, sind)
    for p in range(2):
        dk[0, p] = jnp.concatenate([kv[:, p * LANES:(p + 1) * LANES], kr], axis=-1).astype(BF16)
        put_values_t(dv, p, kv[:, 256 + p * LANES:256 + (p + 1) * LANES])


def _inproj_call(x, shift, scale, lw, tabs):
    B, n, D = x.shape
    tm = min(512, n)
    cos, sin, cosd, sind = tabs
    row = lambda b, i: (b, i, 0)
    hrow = lambda b, i: (b, 0, i, 0)
    const2 = lambda b, i: (0, 0)
    tab = pl.BlockSpec((tm, LANES), lambda b, i: (i, 0))
    mod = pl.BlockSpec((1, 1, D), lambda b, i: (b, 0, 0))
    nt = n // tm
    vt_spec = lambda pairs: ((B, pairs, nt, 2, VT_ROWS, tm),
                             pl.BlockSpec((1, pairs, 1, 2, VT_ROWS, tm), lambda b, i: (b, 0, i, 0, 0, 0)))
    slab = lambda kd: ((B, 4, n, kd), pl.BlockSpec((1, 4, tm, kd), hrow))
    tokm = lambda w: ((B, n, w), pl.BlockSpec((1, tm, w), row))
    vblk = lambda heads: ((B, n // TOKEN_BLOCK, heads, VT_ROWS, TOKEN_BLOCK),
                          pl.BlockSpec((1, tm // TOKEN_BLOCK, heads, VT_ROWS, TOKEN_BLOCK),
                                       lambda b, i: (b, i, 0, 0, 0)))
    outs = [tokm(256), tokm(256), vblk(NA_HEADS)]
    outs += [slab(LANES), tokm(LANES), vt_spec(1)]
    outs += [slab(LANES), tokm(LANES), vblk(2)]
    outs += [slab(256), ((B, 2, n, 256), pl.BlockSpec((1, 2, tm, 256), hrow)), vt_spec(2)]
    return pl.pallas_call(
        _inproj_kernel,
        grid=(B, n // tm),
        in_specs=[pl.BlockSpec((1, tm, D), row), mod, mod,
                  pl.BlockSpec((D, _C_END), const2), tab, tab, tab, tab,
                  pl.BlockSpec((1, LANES), const2), pl.BlockSpec((1, LANES), const2),
                  pl.BlockSpec((1, 256), const2), pl.BlockSpec((1, LANES), const2),
                  pl.BlockSpec((256, 1024), const2), pl.BlockSpec((LANES, 512), const2)],
        out_specs=[o[1] for o in outs],
        out_shape=[jax.ShapeDtypeStruct(o[0], BF16) for o in outs],
        compiler_params=_cparams("arbitrary", "arbitrary"),
        name="in_proj",
    )(x, shift, scale, lw["w_qkv"], cos, sin, cosd, sind,
      lw["gq"], lw["gk"], lw["gmq"], lw["gmkv"], lw["wqb"], lw["wkvb"])


def _flash_kernel(q_ref, k_ref, vt_ref, kc_ref, vct_ref, o_ref, acc_ref, m_ref, s_ref, *, tq, nkk, vpc, nq):
    kd = q_ref.shape[-1]
    ctx_len = kc_ref.shape[3]

    def load_q(i):
        return q_ref[0, :, pl.ds(pl.multiple_of(i * tq, tq), tq), :].reshape(2 * tq, kd)

    def update(s, values):
        m_prev = m_ref[...]
        m_new = jnp.maximum(m_prev, jnp.max(s, axis=0, keepdims=True))
        a = jnp.exp2(m_prev - m_new)
        p = jnp.exp2(s - m_new).astype(BF16)
        for h in range(2):
            cols = slice(h * tq, (h + 1) * tq)
            pv, row = None, 0
            for vt in values:
                term = _dot(vt[h], p[row:row + vt.shape[-1], cols])
                pv = term if pv is None else pv + term
                row += vt.shape[-1]
            acc_ref[h] = a[:, cols] * acc_ref[h] + pv
        m_ref[...] = m_new

    def query_block(i, buf):
        q = load_q(i)
        m_ref[...] = jnp.full(m_ref.shape, NEG_INF, F32)
        acc_ref[...] = jnp.zeros(acc_ref.shape, F32)
        for j in range(nkk):
            cur, nxt = (buf + j) % 2, (buf + j + 1) % 2
            if j + 1 < nkk:
                s_ref[nxt] = _dot_nt(k_ref[0, 0, j + 1], q)
            else:
                s_ref[nxt, :ctx_len] = _dot_nt(kc_ref[0, 0, 0], q)
            update(s_ref[cur], [vt_ref[0, 0, vpc * j + u] for u in range(vpc)])
        s_ref[1 - buf] = _dot_nt(k_ref[0, 0, 0], load_q(jnp.minimum(i + 1, nq - 1)))
        update(s_ref[buf, :ctx_len], [vct_ref[0, 0, 0]])
        o = [acc_ref[h][:HEAD_DIM] / acc_ref[h][HEAD_DIM:HEAD_DIM + 1] for h in range(2)]
        o_ref[0, pl.ds(pl.multiple_of(i * tq, tq), tq), :] = jnp.concatenate(o, axis=0).T.astype(BF16)

    s_ref[0] = _dot_nt(k_ref[0, 0, 0], load_q(0))

    def two_blocks(ii, carry):
        query_block(2 * ii, 0)
        query_block(2 * ii + 1, 1)
        return carry

    lax.fori_loop(0, nq // 2, two_blocks, 0)


def _flash_call(q, k, vt, kc, vct, *, per_pair, key_chunk):
    B, _, n, kd = q.shape
    P = k.shape[1]
    C = kc.shape[2]
    nv, tv = vt.shape[2], vt.shape[5]
    tkk = max(tv, min(key_chunk, n // 2))
    nkk, vpc = n // tkk, tkk // tv
    tq = min(512, n)
    nq = n // tq
    assert nq % 2 == 0 and nkk % 2 == 0 and nkk * vpc == nv and C <= tkk
    pidx = (lambda b, p: (b, p, 0, 0, 0)) if per_pair else (lambda b, p: (b, 0, 0, 0, 0))
    vidx = (lambda b, p: (b, p, 0, 0, 0, 0)) if per_pair else (lambda b, p: (b, 0, 0, 0, 0, 0))
    kern = functools.partial(_flash_kernel, tq=tq, nkk=nkk, vpc=vpc, nq=nq)
    return pl.pallas_call(
        kern,
        grid=(B, 2),
        in_specs=[pl.BlockSpec((1, 2, n, kd), lambda b, p: (b, p, 0, 0)),
                  pl.BlockSpec((1, 1, nkk, tkk, kd), pidx),
                  pl.BlockSpec((1, 1, nv, 2, VT_ROWS, tv), vidx),
                  pl.BlockSpec((1, 1, 1, C, kd), pidx),
                  pl.BlockSpec((1, 1, 1, 2, VT_ROWS, C), vidx)],
        out_specs=pl.BlockSpec((1, n, LANES), lambda b, p: (b, 0, p)),
        out_shape=jax.ShapeDtypeStruct((B, n, 256), BF16),
        scratch_shapes=[pltpu.VMEM((2, VT_ROWS, tq), F32), pltpu.VMEM((1, 2 * tq), F32),
                        pltpu.VMEM((2, tkk, 2 * tq), F32)],
        compiler_params=_cparams("arbitrary", "arbitrary"),
        name="dense_attn",
    )(q, k.reshape(B, P, nkk, tkk, kd), vt, kc[:, :, None], vct)


def _local_values(vt_ref, g0, blocks, vct_ref, head):
    lat = jnp.concatenate([vt_ref[0, g0 + g, head] for g in range(blocks)], axis=1)
    ctx = jnp.concatenate([vct_ref[0, g, head] for g in range(vct_ref.shape[1])], axis=1)
    return lat, ctx


def _na_kernel(q_ref, k_ref, vt_ref, kc_ref, vct_ref, tl_ref, tr_ref, o_ref, *, rows):
    for sub in range(LOCAL_BLOCKS_PER_STEP):
        rows_q = slice(sub * 2 * GRID_W, (sub + 1) * 2 * GRID_W)
        o_ref[0, rows_q, :] = _na_block(LOCAL_BLOCKS_PER_STEP * pl.program_id(1) + sub, q_ref[0, rows_q, :],
                                        k_ref, vt_ref, kc_ref, vct_ref, tl_ref, tr_ref, rows)


def _na_block(i, q, k_ref, vt_ref, kc_ref, vct_ref, tl_ref, tr_ref, rows):
    r0 = 2 * i
    w0 = jnp.clip(r0 - NA_KH // 2, 0, rows - NA_WIN_ROWS)
    win = NA_WIN_ROWS * GRID_W
    start = pl.multiple_of(w0 * GRID_W, 2 * GRID_W)
    k_all = jnp.concatenate([k_ref[0, pl.ds(start, win), :], kc_ref[0]], axis=0)
    head_of_lane = _lane((1, 256)) >> 6
    q_stack = jnp.concatenate([jnp.where(head_of_lane == h, q, jnp.zeros_like(q)) for h in range(NA_HEADS)], axis=0)
    s = _dot_nt(k_all, q_stack)

    def table_index(a, j):
        qr = r0 + a
        kr = w0 + j
        st = jnp.clip(qr - NA_KH // 2, 0, rows - NA_KH)
        ok = (kr >= st) & (kr < st + NA_KH)
        return jnp.where(ok, kr - qr + NA_KH - 1, 2 * NA_KH - 1)

    idx = [[table_index(a, j) for j in range(NA_WIN_ROWS)] for a in range(2)]
    bias = jnp.concatenate(
        [jnp.concatenate([tl_ref[h * 16 + idx[0][j]] + tr_ref[h * 16 + idx[1][j]] for j in range(NA_WIN_ROWS)], axis=0)
         for h in range(NA_HEADS)], axis=1)
    s_lat = s[:win] + bias
    s_ctx = s[win:]
    m = jnp.maximum(jnp.max(s_lat, axis=0, keepdims=True), jnp.max(s_ctx, axis=0, keepdims=True))
    p_lat = jnp.exp2(s_lat - m).astype(BF16)
    p_ctx = jnp.exp2(s_ctx - m).astype(BF16)
    outs = []
    for h in range(NA_HEADS):
        cols = slice(h * LANES, (h + 1) * LANES)
        v_lat, v_ctx = _local_values(vt_ref, w0 >> 1, win // TOKEN_BLOCK, vct_ref, h)
        o = _dot(v_lat, p_lat[:, cols]) + _dot(v_ctx, p_ctx[:, cols])
        outs.append(o[:HEAD_DIM] / o[HEAD_DIM:HEAD_DIM + 1])
    return jnp.concatenate(outs, axis=0).T.astype(BF16)


def _na_call(q, k, vt, kc, vct, tl, tr):
    B, n, _ = q.shape
    C = kc.shape[1]
    rows = n // GRID_W
    step_rows = 2 * LOCAL_BLOCKS_PER_STEP
    assert rows >= NA_WIN_ROWS and rows % step_rows == 0
    full = lambda b, i: (b, 0, 0)
    full5 = lambda b, i: (b, 0, 0, 0, 0)
    return pl.pallas_call(
        functools.partial(_na_kernel, rows=rows),
        grid=(B, rows // step_rows),
        in_specs=[pl.BlockSpec((1, step_rows * GRID_W, 256), lambda b, i: (b, i, 0)),
                  pl.BlockSpec((1, n, 256), full), pl.BlockSpec((1,) + vt.shape[1:], full5),
                  pl.BlockSpec((1, C, 256), full), pl.BlockSpec((1,) + vct.shape[1:], full5),
                  pl.BlockSpec(tl.shape, lambda b, i: (0, 0, 0)),
                  pl.BlockSpec(tr.shape, lambda b, i: (0, 0, 0))],
        out_specs=pl.BlockSpec((1, step_rows * GRID_W, 256), lambda b, i: (b, i, 0)),
        out_shape=jax.ShapeDtypeStruct((B, n, 256), BF16),
        compiler_params=_cparams("arbitrary", "arbitrary"),
        name="nbr_attn",
    )(q, k, vt, kc, vct, tl, tr)


def _win_kernel(sink_ref, q_ref, k_ref, vt_ref, kc_ref, vct_ref, o_ref, *, n):
    for sub in range(LOCAL_BLOCKS_PER_STEP):
        rows_q = slice(sub * QBLOCK, (sub + 1) * QBLOCK)
        o_ref[0, rows_q, :] = _win_block(LOCAL_BLOCKS_PER_STEP * pl.program_id(1) + sub, q_ref[0, :, rows_q, :],
                                         sink_ref, k_ref, vt_ref, kc_ref, vct_ref, n)


def _win_block(i, q, sink_ref, k_ref, vt_ref, kc_ref, vct_ref, n):
    band = 3 * QBLOCK
    ws = pl.multiple_of(jnp.clip((i - 1) * QBLOCK, 0, n - band), QBLOCK)
    k_all = jnp.concatenate([k_ref[0, pl.ds(ws, band), :], kc_ref[0]], axis=0)
    q_stack = q.reshape(4 * QBLOCK, LANES)
    s = _dot_nt(k_all, q_stack)
    kpos = ws + lax.broadcasted_iota(jnp.int32, (band, QBLOCK), 0)
    qpos = i * QBLOCK + lax.broadcasted_iota(jnp.int32, (band, QBLOCK), 1)
    ok = jnp.abs(qpos - kpos) <= SWA_WINDOW
    s_lat = jnp.concatenate([jnp.where(ok, s[:band, j * QBLOCK:(j + 1) * QBLOCK], NEG_INF) for j in range(4)], axis=1)
    s_ctx = s[band:]
    snk = jnp.concatenate([jnp.full((1, QBLOCK), sink_ref[j], F32) for j in range(4)], axis=1)
    m = jnp.maximum(jnp.maximum(jnp.max(s_lat, axis=0, keepdims=True), jnp.max(s_ctx, axis=0, keepdims=True)), snk)
    p_lat = jnp.exp2(s_lat - m).astype(BF16)
    p_ctx = jnp.exp2(s_ctx - m).astype(BF16)
    p_snk = jnp.exp2(snk - m)
    g0 = ws >> 7
    outs = []
    for j in range(4):
        cols = slice(j * QBLOCK, (j + 1) * QBLOCK)
        v_lat, v_ctx = _local_values(vt_ref, g0, band // TOKEN_BLOCK, vct_ref, HEAD_PERM[j] // 2)
        o = _dot(v_lat, p_lat[:, cols]) + _dot(v_ctx, p_ctx[:, cols])
        outs.append(o[:HEAD_DIM] / (o[HEAD_DIM:HEAD_DIM + 1] + p_snk[:, cols]))
    return jnp.concatenate(outs, axis=0).T.astype(BF16)


def _win_call(sink, q, k, vt, kc, vct):
    B, _, n, _ = q.shape
    C = kc.shape[1]
    step = LOCAL_BLOCKS_PER_STEP * QBLOCK
    assert n >= 3 * QBLOCK and QBLOCK == TOKEN_BLOCK and n % step == 0
    full = lambda b, i, s: (b, 0, 0)
    full5 = lambda b, i, s: (b, 0, 0, 0, 0)
    return pl.pallas_call(
        functools.partial(_win_kernel, n=n),
        grid_spec=pltpu.PrefetchScalarGridSpec(
            num_scalar_prefetch=1,
            grid=(B, n // step),
            in_specs=[pl.BlockSpec((1, 4, step, LANES), lambda b, i, s: (b, 0, i, 0)),
                      pl.BlockSpec((1, n, LANES), full), pl.BlockSpec((1,) + vt.shape[1:], full5),
                      pl.BlockSpec((1, C, LANES), full), pl.BlockSpec((1,) + vct.shape[1:], full5)],
            out_specs=pl.BlockSpec((1, step, 256), lambda b, i, s: (b, i, 0))),
        out_shape=jax.ShapeDtypeStruct((B, n, 256), BF16),
        compiler_params=_cparams("arbitrary", "arbitrary"),
        name="window_attn",
    )(sink, q, k, vt, kc, vct)


def _attend(q, k, vt, sink=None):
    s = _dot_nt(q, k)
    m = jnp.max(s, axis=-1, keepdims=True)
    if sink is not None:
        m = jnp.maximum(m, sink)
    p = jnp.exp2(s - m)
    l = jnp.sum(p, axis=-1, keepdims=True)
    if sink is not None:
        l = l + jnp.exp2(sink - m)
    return _dot_nt(p.astype(BF16), vt) / l


def _ctx_attn_kernel(sink_ref, naq, nak, nav, bq, bk, bv, cq, ck, cv, dq, dk, dv, oa, ob, oc, od):
    left = _lane((1, LANES)) < HEAD_DIM
    head_of_lane = _lane((1, 256)) >> 6

    def pair_out(o):
        return jnp.concatenate([jnp.where(left, o[0], o[1]), jnp.where(left, o[2], o[3])], axis=-1).astype(BF16)

    def blocks_t(ref, heads):
        return jnp.concatenate(
            [jnp.concatenate([ref[0, g, h][:HEAD_DIM] for g in range(ref.shape[1])], axis=1)
             for h in range(heads)], axis=0)

    def both_heads(vt):
        return jnp.concatenate([vt[0, :HEAD_DIM], vt[1, :HEAD_DIM]], axis=0)

    q = naq[0]
    v_na = blocks_t(nav, NA_HEADS)
    out = jnp.zeros(q.shape, F32)
    for h in range(NA_HEADS):
        qm = jnp.where(head_of_lane == h, q, jnp.zeros_like(q))
        out = jnp.where(head_of_lane == h, _attend(qm, nak[0], v_na), out)
    oa[0] = out.astype(BF16)
    ob[0] = pair_out([_attend(bq[0, s], bk[0], both_heads(bv[0, 0, 0])) for s in range(4)])
    v_c = blocks_t(cv, 2)
    oc[0] = pair_out([_attend(cq[0, s], ck[0], v_c, sink_ref[s]) for s in range(4)])
    od[0] = pair_out([_attend(dq[0, h], dk[0, h // 2], both_heads(dv[0, h // 2, 0])) for h in range(MLA_HEADS)])


def _ctx_attn_call(sink, pc):
    B, C, _ = pc[0].shape
    names = pc
    specs = []
    for a in names:
        nd = a.ndim
        specs.append(pl.BlockSpec((1,) + a.shape[1:], (lambda b, s, nd=nd: (b,) + (0,) * (nd - 1))))
    out_spec = pl.BlockSpec((1, C, 256), lambda b, s: (b, 0, 0))
    return pl.pallas_call(
        _ctx_attn_kernel,
        grid_spec=pltpu.PrefetchScalarGridSpec(
            num_scalar_prefetch=1, grid=(B,), in_specs=specs, out_specs=[out_spec] * 4),
        out_shape=[jax.ShapeDtypeStruct((B, C, 256), BF16)] * 4,
        compiler_params=_cparams("arbitrary"),
        name="ctx_attn",
    )(sink, *pc)


def _merge_kernel(x_ref, oa, ob, oc, od, sha, sca, ga, shf, scf, wg_ref, wb_ref, wo_ref,
                  l1g, l1b, wr_ref, x1_ref, h2_ref, aff_ref, *, alpha):
    x = x_ref[0]
    D = x.shape[-1]
    hb = _modulate(x, sha[0], sca[0]).astype(BF16)
    merged = None
    for i, o in enumerate((oa, ob, oc, od)):
        g = 1.0 / (1.0 + jnp.exp(-_dot(hb, wg_ref[:, i * D:(i + 1) * D])))
        term = g * _dot(o[0], wb_ref[i])
        merged = term if merged is None else merged + term
    y = _dot(merged.astype(BF16), wo_ref[...])
    x1 = _ln(alpha * x + ga[0] * y) * l1g[...] + l1b[...]
    x1_ref[0] = x1
    h2 = _modulate(x1, shf[0], scf[0])
    h2_hi = h2.astype(BF16)
    h2_ref[0] = h2_hi
    h2_lo = (h2 - h2_hi.astype(F32)).astype(BF16)
    hh_hl = _dot(h2_hi, wr_ref[...])
    logits = hh_hl[:, :LANES] + hh_hl[:, LANES:] + _dot(h2_lo, wr_ref[:, :LANES])
    logits = jnp.where(_lane((1, LANES)) < N_EXPERTS, logits, NEG_INF)
    e = jnp.exp(logits - jnp.max(logits, axis=-1, keepdims=True))
    aff_t = (e / jnp.sum(e, axis=-1, keepdims=True)).T
    for k in range(aff_ref.shape[1]):
        aff_ref[0, k] = aff_t[:N_EXPERTS, k * TOKEN_BLOCK:(k + 1) * TOKEN_BLOCK]


def _merge_call(x, outs, mods, lw, alpha):
    B, n, D = x.shape
    tm = min(256, n)
    nb = tm // TOKEN_BLOCK
    row = lambda b, i: (b, i, 0)
    mod = pl.BlockSpec((1, 1, D), lambda b, i: (b, 0, 0))
    c2 = lambda b, i: (0, 0)
    obr = pl.BlockSpec((1, tm, 256), row)
    return pl.pallas_call(
        functools.partial(_merge_kernel, alpha=alpha),
        grid=(B, n // tm),
        in_specs=[pl.BlockSpec((1, tm, D), row), obr, obr, obr, obr, mod, mod, mod, mod, mod,
                  pl.BlockSpec((D, N_BRANCH * D), c2),
                  pl.BlockSpec((N_BRANCH, BRANCH_W, D), lambda b, i: (0, 0, 0)),
                  pl.BlockSpec((D, D), c2), pl.BlockSpec((1, D), c2), pl.BlockSpec((1, D), c2),
                  pl.BlockSpec((D, 2 * LANES), c2)],
        out_specs=[pl.BlockSpec((1, tm, D), row), pl.BlockSpec((1, tm, D), row),
                   pl.BlockSpec((1, nb, N_EXPERTS, TOKEN_BLOCK), lambda b, i: (b, i, 0, 0))],
        out_shape=[jax.ShapeDtypeStruct((B, n, D), F32), jax.ShapeDtypeStruct((B, n, D), BF16),
                   jax.ShapeDtypeStruct((B, n // TOKEN_BLOCK, N_EXPERTS, TOKEN_BLOCK), F32)],
        compiler_params=_cparams("arbitrary", "arbitrary"),
        name="merge_router",
    )(x, *outs, *mods, lw["w_gates"], lw["w_branch"], lw["w_out"], lw["ln1_g"], lw["ln1_b"],
      lw["wr"])


def _select_kernel(aff_ref, pos_ref, off_ref, tlo_ref, thi_ref, *, cap, slot_tile):
    a = aff_ref[0]
    nb = a.shape[0]
    rows = nb * N_EXPERTS
    bits = lax.bitcast_convert_type(a, jnp.int32)
    capf = jnp.float32(cap)

    def count(mask):
        c = jnp.sum(jnp.where(mask, 1.0, 0.0), axis=0)
        return jnp.broadcast_to(jnp.sum(c, axis=-1, keepdims=True), c.shape)

    def search(it, lo):
        cand = lo | lax.shift_left(jnp.int32(1), 30 - it)
        return jnp.where(count(bits >= cand[None]) >= capf, cand, lo)

    thr = lax.fori_loop(0, 31, search, jnp.zeros((N_EXPERTS, TOKEN_BLOCK), jnp.int32))

    r = lax.broadcasted_iota(jnp.int32, (rows, rows), 0)
    c = lax.broadcasted_iota(jnp.int32, (rows, rows), 1)
    earlier = jnp.where(((r & (N_EXPERTS - 1)) == (c & (N_EXPERTS - 1))) & ((c >> 4) < (r >> 4)), 1.0, 0.0).astype(BF16)
    ti = lax.broadcasted_iota(jnp.int32, (TOKEN_BLOCK, TOKEN_BLOCK), 0)
    tj = lax.broadcasted_iota(jnp.int32, (TOKEN_BLOCK, TOKEN_BLOCK), 1)
    tri = jnp.where(ti <= tj, 1.0, 0.0).astype(BF16)
    ones = jnp.ones((TOKEN_BLOCK, TOKEN_BLOCK), BF16)

    def prefix(mask):
        m2 = jnp.where(mask, 1.0, 0.0).reshape(rows, TOKEN_BLOCK)
        mb = m2.astype(BF16)
        within = _dot(mb, tri)
        tot = _dot(mb, ones)
        off = _dot(earlier, tot.astype(BF16))
        shp = (nb, N_EXPERTS, TOKEN_BLOCK)
        return (off + within - m2).reshape(shp), off.reshape(shp), tot.reshape(shp)

    gt = bits > thr[None]
    eq = bits == thr[None]
    need = capf - count(gt)
    eq_rank, _, _ = prefix(eq)
    sel = gt | (eq & (eq_rank < need[None]))
    excl, off, tot = prefix(sel)
    pos_ref[0] = jnp.where(sel, excl, -1.0)
    off_ref[0] = off.astype(jnp.int32)
    tile_start = (_lane((1, 1, TOKEN_BLOCK)) * slot_tile).astype(F32)
    tlo_ref[0] = jnp.sum(jnp.where(off + tot <= tile_start, 1, 0), axis=0).astype(jnp.int32)
    thi_ref[0] = jnp.sum(jnp.where(off < tile_start + slot_tile, 1, 0), axis=0).astype(jnp.int32)


def _select_call(aff, cap, slot_tile):
    B, nb, E, _ = aff.shape
    blk = pl.BlockSpec((1, nb, E, TOKEN_BLOCK), lambda b: (b, 0, 0, 0))
    rng = pl.BlockSpec((1, E, TOKEN_BLOCK), lambda b: (b, 0, 0))
    return pl.pallas_call(
        functools.partial(_select_kernel, cap=cap, slot_tile=slot_tile),
        grid=(B,),
        in_specs=[blk],
        out_specs=[blk, blk, rng, rng],
        out_shape=[jax.ShapeDtypeStruct(aff.shape, F32), jax.ShapeDtypeStruct(aff.shape, jnp.int32),
                   jax.ShapeDtypeStruct((B, E, TOKEN_BLOCK), jnp.int32),
                   jax.ShapeDtypeStruct((B, E, TOKEN_BLOCK), jnp.int32)],
        compiler_params=_cparams("arbitrary"),
        name="expert_select",
    )(aff)


def _gather_kernel(tlo_ref, thi_ref, pos_ref, aff_ref, h_ref, xg_ref, g_ref, acc_ref, gacc_ref, *,
                   slot_tile, group, in_flight):
    b = pl.program_id(0)
    e = pl.program_id(1)
    n_tiles = xg_ref.shape[2] // slot_tile
    for t in range(n_tiles):
        acc_ref[...] = jnp.zeros(acc_ref.shape, F32)
        gacc_ref[...] = jnp.zeros(gacc_ref.shape, F32)
        width = group * TOKEN_BLOCK
        slot = (t * slot_tile + lax.broadcasted_iota(jnp.int32, (slot_tile, width), 0)).astype(F32)
        shift = group.bit_length() - 1
        lo = tlo_ref[b, e, t] >> shift
        hi = (thi_ref[b, e, t] + group - 1) >> shift
        last = pos_ref.shape[1] // group - 1

        def one_product(pg, valid):
            p = jnp.concatenate([pos_ref[0, group * pg + u, pl.ds(e, 1), :] for u in range(group)], axis=-1)
            a = jnp.concatenate([aff_ref[0, group * pg + u, pl.ds(e, 1), :] for u in range(group)], axis=-1)
            hit = p == slot
            tok = pl.multiple_of(pg * width, width)
            rows = _dot(jnp.where(hit, valid, 0.0).astype(BF16), h_ref[0, pl.ds(tok, width), :])
            ga = jnp.where(hit, a * valid, 0.0)
            return rows, sum(ga[:, u * TOKEN_BLOCK:(u + 1) * TOKEN_BLOCK] for u in range(group))

        def body(it, carry):
            rows, gates = None, None
            for u in range(in_flight):
                pg = lo + in_flight * it + u
                r, g = one_product(jnp.minimum(pg, last), jnp.where(pg < hi, 1.0, 0.0).astype(F32))
                rows = r if rows is None else rows + r
                gates = g if gates is None else gates + g
            acc_ref[...] += rows
            gacc_ref[...] += gates
            return carry

        lax.fori_loop(0, lax.div(hi - lo + in_flight - 1, in_flight), body, 0)
        xg_ref[0, 0, t * slot_tile:(t + 1) * slot_tile, :] = acc_ref[...].astype(BF16)
        g_ref[0, 0, t * slot_tile:(t + 1) * slot_tile, :] = jnp.sum(gacc_ref[...], axis=-1, keepdims=True)


def _gather_call(tlo, thi, pos, aff, h2, cap_pad, slot_tile):
    B, n, D = h2.shape
    nb = n // TOKEN_BLOCK
    blk = pl.BlockSpec((1, nb, N_EXPERTS, TOKEN_BLOCK), lambda b, e, *_: (b, 0, 0, 0))
    return pl.pallas_call(
        functools.partial(_gather_kernel, slot_tile=slot_tile, group=2, in_flight=min(5, nb // 2)),
        grid_spec=pltpu.PrefetchScalarGridSpec(
            num_scalar_prefetch=2,
            grid=(B, N_EXPERTS),
            in_specs=[blk, blk, pl.BlockSpec((1, n, D), lambda b, e, *_: (b, 0, 0))],
            out_specs=[pl.BlockSpec((1, 1, cap_pad, D), lambda b, e, *_: (b, e, 0, 0)),
                       pl.BlockSpec((1, 1, cap_pad, 1), lambda b, e, *_: (b, e, 0, 0))],
            scratch_shapes=[pltpu.VMEM((slot_tile, D), F32), pltpu.VMEM((slot_tile, TOKEN_BLOCK), F32)]),
        out_shape=[jax.ShapeDtypeStruct((B, N_EXPERTS, cap_pad, D), BF16),
                   jax.ShapeDtypeStruct((B, N_EXPERTS, cap_pad, 1), F32)],
        compiler_params=_cparams("arbitrary", "arbitrary"),
        name="expert_gather",
    )(tlo, thi, pos, aff, h2)


def _ffn_kernel(xg_ref, g_ref, wg_ref, wu_ref, wd_ref, y_ref, wg_bf, wu_bf, wd_bf):
    @pl.when(pl.program_id(1) == 0)
    def _():
        wg_bf[...] = wg_ref[0, 0].astype(BF16)
        wu_bf[...] = wu_ref[0, 0].astype(BF16)
        wd_bf[...] = wd_ref[0, 0].astype(BF16)

    bb, _, cp, d = xg_ref.shape
    xg = xg_ref[:, 0].reshape(bb * cp, d)
    a = _dot(xg, wg_bf[...])
    u = _dot(xg, wu_bf[...])
    hmid = (a / (1.0 + jnp.exp(-a)) * u).astype(BF16)
    y = _dot(hmid, wd_bf[...]) * g_ref[:, 0].reshape(bb * cp, 1)
    y_ref[:, 0] = y.reshape(bb, cp, d).astype(BF16)


def _ffn_call(xg, g, l, p):
    B, E, cp, D = xg.shape
    F = p["w_gate"].shape[-1]
    bb = B if B * cp <= 1024 else 1
    tok = lambda e, b: (b, e, 0, 0)
    wsp = lambda e, b: (l, e, 0, 0)
    return pl.pallas_call(
        _ffn_kernel,
        grid=(E, B // bb),
        in_specs=[pl.BlockSpec((bb, 1, cp, D), tok), pl.BlockSpec((bb, 1, cp, 1), tok),
                  pl.BlockSpec((1, 1, D, F), wsp), pl.BlockSpec((1, 1, D, F), wsp),
                  pl.BlockSpec((1, 1, F, D), wsp)],
        out_specs=pl.BlockSpec((bb, 1, cp, D), tok),
        out_shape=jax.ShapeDtypeStruct((B, E, cp, D), BF16),
        scratch_shapes=[pltpu.VMEM((D, F), BF16), pltpu.VMEM((D, F), BF16), pltpu.VMEM((F, D), BF16)],
        compiler_params=_cparams("arbitrary", "arbitrary"),
        name="expert_mlp",
    )(xg, g, p["w_gate"], p["w_up"], p["w_down"])


def _combine_kernel(off_ref, pos_ref, yw_ref, x_ref, g_ref, lg, lb, o_ref, *, window, blocks_per_step, alpha):
    b = pl.program_id(0)
    t = pl.program_id(1)
    cap_pad = yw_ref.shape[2]
    nb_total = pl.num_programs(1) * blocks_per_step
    for k in range(blocks_per_step):
        blk = t * blocks_per_step + k
        rows = slice(k * TOKEN_BLOCK, (k + 1) * TOKEN_BLOCK)
        acc = jnp.zeros((TOKEN_BLOCK, yw_ref.shape[-1]), F32)
        for e in range(N_EXPERTS):
            start = off_ref[(b * nb_total + blk) * N_EXPERTS + e]
            w0 = pl.multiple_of(jnp.minimum(start & -16, cap_pad - window), 16)
            p = pos_ref[0, k, e:e + 1, :]
            slot = (w0 + lax.broadcasted_iota(jnp.int32, (window, TOKEN_BLOCK), 0)).astype(F32)
            hit = jnp.where(p == slot, 1.0, 0.0).T.astype(BF16)
            acc = acc + _dot(hit, yw_ref[0, e, pl.ds(w0, window), :])
        o_ref[0, rows, :] = _ln(alpha * x_ref[0, rows, :] + g_ref[0] * acc) * lg[...] + lb[...]


def _combine_call(off_flat, pos, yw, x1, gf, lg, lb, alpha):
    B, E, cap_pad, D = yw.shape
    n = x1.shape[1]
    nb = n // TOKEN_BLOCK
    window = min(256, cap_pad)
    bps = min(4, nb)
    row = pl.BlockSpec((1, bps * TOKEN_BLOCK, D), lambda b, t, *_: (b, t, 0))
    vec = pl.BlockSpec((1, D), lambda b, t, *_: (0, 0))
    return pl.pallas_call(
        functools.partial(_combine_kernel, window=window, blocks_per_step=bps, alpha=alpha),
        grid_spec=pltpu.PrefetchScalarGridSpec(
            num_scalar_prefetch=1,
            grid=(B, nb // bps),
            in_specs=[pl.BlockSpec((1, bps, E, TOKEN_BLOCK), lambda b, t, *_: (b, t, 0, 0)),
                      pl.BlockSpec((1, E, cap_pad, D), lambda b, t, *_: (b, 0, 0, 0),
                                   pipeline_mode=pl.Buffered(1)),
                      row, pl.BlockSpec((1, 1, D), lambda b, t, *_: (b, 0, 0)), vec, vec],
            out_specs=row),
        out_shape=jax.ShapeDtypeStruct((B, n, D), F32),
        compiler_params=_cparams("arbitrary", "arbitrary"),
        name="expert_combine_ln",
    )(off_flat, pos, yw, x1, gf, lg, lb)


def _paired_lanes(dim):
    q = dim // 4
    return np.r_[0:q, 2 * q:3 * q], np.r_[q:2 * q, 3 * q:4 * q]


def _slot_pad(w, slot):
    z = jnp.zeros_like(w)
    return jnp.concatenate([w, z] if slot == 0 else [z, w], axis=-1)


def _prep_layer(l, p):
    w_in = p["w_in"][l]
    D = w_in.shape[0]
    hd = HEAD_DIM

    zc = lambda k: jnp.zeros((D, k), F32)

    def packed(w, a, b):
        ha, hb = w[:, a * hd:(a + 1) * hd], w[:, b * hd:(b + 1) * hd]
        first, second = _paired_lanes(hd)
        return jnp.concatenate([ha[:, first], hb[:, first], ha[:, second], hb[:, second]], axis=-1)

    def rot_slab(w):
        first, second = _paired_lanes(MLA_ROPE)
        z48 = jnp.zeros((w.shape[0], HEAD_DIM - MLA_ROPE // 2), F32)
        return jnp.concatenate([w[:, first], z48, w[:, second], z48], axis=-1)

    def gq_cols(base):
        q, k = w_in[:, base:base + 256], w_in[:, base + 256:base + 384]
        return [packed(q, 0, 2), packed(q, 1, 3), packed(k, 0, 1), w_in[:, base + 384:base + 512]]

    cols = [w_in[:, 0:768]] + gq_cols(768) + gq_cols(1280)
    cols += [w_in[:, 1792:1984], zc(64), w_in[:, 1984:2112], rot_slab(w_in[:, 2112:2144])]
    w_qkv = jnp.concatenate(cols, axis=-1).astype(BF16)
    assert w_qkv.shape[1] == _C_END

    gq = packed(p["gqa_q_norm"][l][None], 0, 0)
    gk = packed(p["gqa_k_norm"][l][None], 0, 0)
    gmq = jnp.concatenate([p["mla_q_norm"][l], jnp.zeros((64,), F32)])[None]
    gmkv = p["mla_kv_norm"][l][None]

    wq = p["mla_w_qb"][l]
    qcols = []
    for h in range(MLA_HEADS):
        nope = wq[:, h * 96: h * 96 + 64]
        rot = wq[:, h * 96 + 64: (h + 1) * 96]
        qcols += [_slot_pad(nope, h % 2), rot_slab(rot)]
    wqb = jnp.concatenate(qcols, axis=-1)
    wqb = jnp.concatenate([wqb, jnp.zeros((64, wqb.shape[1]), F32)], axis=0).astype(BF16)
    wkv = p["mla_w_kvb"][l]
    wkvb = jnp.concatenate([wkv[:, h * 128: h * 128 + 64] for h in range(MLA_HEADS)]
                           + [wkv[:, h * 128 + 64: (h + 1) * 128] for h in range(MLA_HEADS)],
                           axis=-1).astype(BF16)

    wb = p["w_branch"][l]
    perm = np.concatenate([np.arange(h * hd, (h + 1) * hd) for h in HEAD_PERM])
    w_branch = jnp.stack([wb[0], wb[1][perm], wb[2][perm], wb[3]]).astype(BF16)

    wr = jnp.concatenate([p["w_router"][l], jnp.zeros((D, LANES - N_EXPERTS), F32)], axis=-1)
    wr_hi = wr.astype(BF16)
    wr_lo = (wr - wr_hi.astype(F32)).astype(BF16)

    w = jnp.arange(GRID_W)
    col_start = jnp.clip(w - NA_KW // 2, 0, GRID_W - NA_KW)
    col_ok = (w[None, :] >= col_start[:, None]) & (w[None, :] < col_start[:, None] + NA_KW)
    dc_idx = jnp.clip(w[None, :] - w[:, None], 1 - NA_KW, NA_KW - 1) + NA_KW - 1
    t = jnp.where(col_ok[None, None], p["na_rpb"][l][:, :, dc_idx], NEG_INF)
    t = jnp.concatenate([t, jnp.full((NA_HEADS, 1, GRID_W, GRID_W), NEG_INF, F32)], axis=1)
    t = (t * LOG2E).transpose(0, 1, 3, 2).reshape(NA_HEADS * 16, GRID_W, GRID_W)
    zt = jnp.zeros_like(t)
    return dict(
        w_qkv=w_qkv, gq=gq, gk=gk, gmq=gmq, gmkv=gmkv, wqb=wqb, wkvb=wkvb,
        w_gates=w_in[:, 2144:].astype(BF16), w_branch=w_branch, w_out=p["w_out"][l].astype(BF16),
        ln1_g=p["ln1_g"][l][None], ln1_b=p["ln1_b"][l][None],
        ln2_g=p["ln2_g"][l][None], ln2_b=p["ln2_b"][l][None],
        wr=jnp.concatenate([wr_hi, wr_lo], axis=-1),
        tl=jnp.concatenate([t, zt], axis=-1), tr=jnp.concatenate([zt, t], axis=-1),
        sink=p["swa_sink"][l][np.array(HEAD_PERM)] * LOG2E,
    )


def _rope_tables(n, ctx_len):
    pos = jnp.arange(n, dtype=jnp.int32)
    row = (pos // GRID_W).astype(F32)
    col = (pos % GRID_W).astype(F32)

    def axial(dim):
        quarter = dim // 4
        freqs = ROPE_THETA ** (-jnp.arange(quarter, dtype=F32) / quarter)
        ang = jnp.concatenate([row[:, None] * freqs[None, :], col[:, None] * freqs[None, :]], axis=-1)
        return jnp.cos(ang), jnp.sin(ang)

    c, s = axial(HEAD_DIM)
    cos = jnp.concatenate([c, c, c, c], axis=-1)
    sin = jnp.concatenate([-s, -s, s, s], axis=-1)
    c, s = axial(MLA_ROPE)
    pad1 = jnp.ones((n, HEAD_DIM - MLA_ROPE // 2), F32)
    pad0 = jnp.zeros((n, HEAD_DIM - MLA_ROPE // 2), F32)
    cosd = jnp.concatenate([c, pad1, c, pad1], axis=-1)
    sind = jnp.concatenate([-s, pad0, s, pad0], axis=-1)
    one = jnp.ones((ctx_len, LANES), F32)
    zero = jnp.zeros((ctx_len, LANES), F32)
    return (cos, sin, cosd, sind), (one, zero, one, zero)


def _expert_ffn_ln(x1, h2, aff, gf, lw, l, p, alpha):
    B, n, D = h2.shape
    cap = CAPACITY * n // N_EXPERTS
    cap_pad = max(cap, TOKEN_BLOCK)
    slot_tile = TOKEN_BLOCK
    pos, off, tlo, thi = _select_call(aff, cap, slot_tile)
    xg, g = _gather_call(tlo, thi, pos, aff, h2, cap_pad, slot_tile)
    yw = _ffn_call(xg, g, l, p)
    return _combine_call(off[..., 0].reshape(-1), pos, yw, x1, gf, lw["ln2_g"], lw["ln2_b"], alpha)


def kernel(x, c, ctx, c_ctx, w_mod, b_mod, w_in, na_rpb, gqa_q_norm, gqa_k_norm, swa_sink, mla_q_norm, mla_kv_norm, mla_w_qb, mla_w_kvb, w_branch, w_out, ln1_g, ln1_b, ln2_g, ln2_b, w_router, w_gate, w_up, w_down):
    p = dict(w_in=w_in, na_rpb=na_rpb, gqa_q_norm=gqa_q_norm, gqa_k_norm=gqa_k_norm, swa_sink=swa_sink,
             mla_q_norm=mla_q_norm, mla_kv_norm=mla_kv_norm, mla_w_qb=mla_w_qb, mla_w_kvb=mla_w_kvb,
             w_branch=w_branch, w_out=w_out, ln1_g=ln1_g, ln1_b=ln1_b, ln2_g=ln2_g, ln2_b=ln2_b,
             w_router=w_router, w_gate=w_gate, w_up=w_up, w_down=w_down)
    B, n, D = x.shape
    depth = w_in.shape[0]
    C = ctx.shape[1]
    alpha = (2 * depth) ** 0.25
    assert B + 1 <= 8
    cc = jnp.concatenate([c, c_ctx[None], jnp.zeros((8 - B - 1, D), F32)], axis=0)
    mod_all = _mod_call(cc, w_mod, b_mod)
    tabs, tabs_ctx = _rope_tables(n, C)
    xc = ctx
    for l in range(depth):
        lw = _prep_layer(l, p)
        need_ctx = l < depth - 1
        mods = [mod_all[l, :B, k * D:(k + 1) * D][:, None, :] for k in range(6)]
        mods_c = [jnp.broadcast_to(mod_all[l, B, k * D:(k + 1) * D][None, None, :], (B, 1, D)) for k in range(6)]
        pl_ = _inproj_call(x, mods[0], mods[1], lw, tabs)
        pc = _inproj_call(xc, mods_c[0], mods_c[1], lw, tabs_ctx)
        (naq, nak, nav, bq, bk, bv, cq, ck, cv, dq, dk, dv) = pl_
        (_, nakc, navc, _, bkc, bvc, _, ckc, cvc, _, dkc, dvc) = pc
        o_a = _na_call(naq, nak, nav, nakc, navc, lw["tl"], lw["tr"])
        o_b = _flash_call(bq, bk[:, None], bv, bkc[:, None], bvc, per_pair=False, key_chunk=DENSE_KEY_CHUNK)
        o_c = _win_call(lw["sink"], cq, ck, cv, ckc, cvc)
        o_d = _flash_call(dq, dk, dv, dkc, dvc, per_pair=True, key_chunk=DENSE_KEY_CHUNK)
        x1, h2, aff = _merge_call(x, (o_a, o_b, o_c, o_d), mods[:5], lw, alpha)
        x = _expert_ffn_ln(x1, h2, aff, mods[5], lw, l, p, alpha)
        if need_ctx:
            oc = _ctx_attn_call(lw["sink"], pc)
            xc1, hc2, affc = _merge_call(xc, oc, mods_c[:5], lw, alpha)
            xc = _expert_ffn_ln(xc1, hc2, affc, mods_c[5], lw, l, p, alpha)
    return x
```
